```python
import jax, jax.numpy as jnp
from jax import lax
import numpy as np

D_MODEL = 2048
BATCH = 4
SEQ = 2048
DEPTH = 2
DEC_BATCH = 128
DEC_SEQ = 8
PAST_LEN = 16384
PAGE_SIZE = 128

N_META = 16
POOL_WIDTH = D_MODEL
POOL_WINDOWS = (2, 4, 8, 16)
N_POOL_GROUPS = len(POOL_WINDOWS)
POOL_GROUP = POOL_WIDTH // N_POOL_GROUPS
POOL_HIST = max(POOL_WINDOWS) - 1
RET_HEADS = 8
RET_QK_DIM = D_MODEL // RET_HEADS
RET_V_DIM = D_MODEL // RET_HEADS
RET_QK_WIDTH = RET_HEADS * RET_QK_DIM
RET_V_WIDTH = RET_HEADS * RET_V_DIM
RET_CHUNK = 128
ROPE_BASE = 10000.0
EPS = 1e-6
IN_SIZES = (POOL_WIDTH, POOL_WIDTH, RET_QK_WIDTH, RET_QK_WIDTH, RET_V_WIDTH, RET_V_WIDTH, D_MODEL, D_MODEL)
IN_OFFSETS = tuple(int(o) for o in np.cumsum(IN_SIZES)[:-1])
N_IN = sum(IN_SIZES)

kernel_name = 'hybrid_pool_retention_step'

F32 = jnp.float32


def rms_norm(x, g):
    xf = x.astype(F32)
    y = xf * lax.rsqrt(jnp.mean(xf * xf, axis=-1, keepdims=True) + EPS)
    return (y * g.astype(F32)).astype(x.dtype)


def head_norm(o):
    mu = jnp.mean(o, axis=-1, keepdims=True)
    oc = o - mu
    return oc * lax.rsqrt(jnp.mean(oc * oc, axis=-1, keepdims=True) + EPS)


def log_decay():
    return jnp.log1p(-jnp.exp2(-5.0 - jnp.arange(RET_HEADS, dtype=F32)))


def rope(t, pos):
    half = t.shape[-1] // 2
    inv_freq = ROPE_BASE ** (-jnp.arange(half, dtype=F32) / half)
    ang = pos[:, None] * inv_freq[None, :]
    cos = jnp.cos(ang)[None, :, None, :]
    sin = jnp.sin(ang)[None, :, None, :]
    t1, t2 = t[..., :half], t[..., half:]
    return jnp.concatenate([t1 * cos - t2 * sin, t2 * cos + t1 * sin], axis=-1)


def retention_chunk(S, q, k, v):
    n = q.shape[1]
    lg = log_decay()
    idx = jnp.arange(n, dtype=F32)
    diff = idx[:, None] - idx[None, :]
    causal = diff >= 0
    decay = jnp.where(causal[None], jnp.exp(jnp.where(causal, diff, 0.0)[None] * lg[:, None, None]), 0.0)
    scores = jnp.einsum('bihd,bjhd->bhij', q, k) * decay[None]
    intra = jnp.einsum('bhij,bjhe->bihe', scores, v)
    q_decay = jnp.exp((idx[:, None] + 1.0) * lg[None, :])
    inter = jnp.einsum('bihd,bhde->bihe', q, S) * q_decay[None, :, :, None]
    k_decay = jnp.exp((n - 1.0 - idx)[:, None] * lg[None, :])
    S_new = S * jnp.exp(n * lg)[None, :, None, None] + jnp.einsum('bjhd,bjhe->bhde', k * k_decay[None, :, :, None], v)
    return S_new, intra + inter


def retention_prompt(q, k, v):
    b, L = q.shape[:2]
    S0 = jnp.zeros((b, RET_HEADS, RET_QK_DIM, RET_V_DIM), F32)
    S1, o_meta = retention_chunk(S0, q[:, :N_META], k[:, :N_META], v[:, :N_META])
    n_chunks = (L - N_META) // RET_CHUNK

    def to_chunks(t):
        return t[:, N_META:].reshape(b, n_chunks, RET_CHUNK, *t.shape[2:]).swapaxes(0, 1)

    def step(S, qkv):
        return retention_chunk(S, *qkv)

    S_fin, o = lax.scan(step, S1, (to_chunks(q), to_chunks(k), to_chunks(v)))
    o = o.swapaxes(0, 1).reshape(b, n_chunks * RET_CHUNK, RET_HEADS, RET_V_DIM)
    return S_fin, jnp.concatenate([o_meta, o], axis=1)


def pool_mix(u, hist, pos0, pool_w, pool_scale):
    b, n = u.shape[:2]
    uf = u.astype(F32)
    ext = jnp.concatenate([hist.astype(F32), uf], axis=1)
    cs = jnp.concatenate([jnp.zeros_like(ext[:, :1]), jnp.cumsum(ext, axis=1)], axis=1)
    avail = pos0 + jnp.arange(n, dtype=F32) + 1.0
    groups = []
    for g, w in enumerate(POOL_WINDOWS):
        sl = slice(g * POOL_GROUP, (g + 1) * POOL_GROUP)
        wsum = cs[:, POOL_HIST + 1:POOL_HIST + 1 + n, sl] - cs[:, POOL_HIST + 1 - w:POOL_HIST + 1 - w + n, sl]
        cnt = jnp.minimum(float(w), avail)
        groups.append(wsum / cnt[None, :, None])
    pooled = jnp.concatenate(groups, axis=-1) - uf
    mixed = jnp.einsum('bngc,gcd->bngd', pooled.reshape(b, n, N_POOL_GROUPS, POOL_GROUP), pool_w.astype(F32))
    out = mixed.reshape(b, n, POOL_WIDTH) * pool_scale.astype(F32)
    return out.astype(u.dtype), ext[:, -POOL_HIST:].astype(hist.dtype)


def mixer_layer(x, pos, pos0, pool_hist, ret_state, is_prompt, norm_g, w_in, pool_w, pool_scale,
                ret_gn, proj_pool, proj_ret, w_out):
    b, n = x.shape[:2]
    h = rms_norm(x, norm_g)
    z = h @ w_in
    u, pg, q, k, v, rg, gp, gr = jnp.split(z, IN_OFFSETS, axis=-1)
    pool_out, new_hist = pool_mix(u, pool_hist, pos0, pool_w, pool_scale)
    pool_branch = (pool_out * jax.nn.silu(pg)) @ proj_pool
    qh = rope(q.reshape(b, n, RET_HEADS, RET_QK_DIM).astype(F32), pos)
    kh = rope(k.reshape(b, n, RET_HEADS, RET_QK_DIM).astype(F32), pos) * (RET_QK_DIM ** -0.5)
    vh = v.reshape(b, n, RET_HEADS, RET_V_DIM).astype(F32)
    if is_prompt:
        S_new, o = retention_prompt(qh, kh, vh)
    else:
        S_new, o = retention_chunk(ret_state.astype(F32), qh, kh, vh)
    o = head_norm(o).reshape(b, n, RET_V_WIDTH) * ret_gn.astype(F32)
    ret_branch = (o.astype(x.dtype) * jax.nn.silu(rg)) @ proj_ret
    merged = jax.nn.sigmoid(gp) * pool_branch + jax.nn.sigmoid(gr) * ret_branch
    return x + merged @ w_out, new_hist, S_new


def setup_inputs(seed: int = 0) -> dict:
    key = jax.random.key(seed)
    ks = jax.random.split(key, 14)

    def nrm(k, shape, scale):
        return jax.random.normal(k, shape, F32) * scale

    gam = 1.0 - 2.0 ** (-5.0 - np.arange(RET_HEADS))
    ret_scale = jnp.asarray((RET_QK_DIM ** -0.5) / np.sqrt(1.0 - gam ** 2), F32)
    state_ret = nrm(ks[3], (DEPTH, DEC_BATCH, RET_HEADS, RET_QK_DIM, RET_V_DIM), 1.0) * ret_scale[None, None, :, None, None]
    return {
        'x_prompt': nrm(ks[0], (BATCH, SEQ, D_MODEL), 1.0),
        'x_sample': nrm(ks[1], (DEC_BATCH, DEC_SEQ, D_MODEL), 1.0),
        'state_pool': nrm(ks[2], (DEPTH, DEC_BATCH, POOL_HIST, POOL_WIDTH), 1.0),
        'state_ret': state_ret,
        'meta_tokens': nrm(ks[4], (N_META, D_MODEL), 1.0),
        'norm_gain': 1.0 + nrm(ks[5], (DEPTH, D_MODEL), 0.02),
        'w_in': nrm(ks[6], (DEPTH, D_MODEL, N_IN), D_MODEL ** -0.5),
        'pool_w': nrm(ks[7], (DEPTH, N_POOL_GROUPS, POOL_GROUP, POOL_GROUP), POOL_GROUP ** -0.5),
        'pool_scale': 1.0 + nrm(ks[8], (DEPTH, POOL_WIDTH), 0.02),
        'ret_gn_gain': 1.0 + nrm(ks[9], (DEPTH, RET_V_WIDTH), 0.02),
        'proj_pool': nrm(ks[10], (DEPTH, POOL_WIDTH, D_MODEL), POOL_WIDTH ** -0.5),
        'proj_ret': nrm(ks[11], (DEPTH, RET_V_WIDTH, D_MODEL), RET_V_WIDTH ** -0.5),
        'w_out': nrm(ks[12], (DEPTH, D_MODEL, D_MODEL), D_MODEL ** -0.5),
        'final_norm': 1.0 + nrm(ks[13], (D_MODEL,), 0.02),
    }


def reference(x_prompt, x_sample, state_pool, state_ret, meta_tokens, norm_gain, w_in, pool_w,
              pool_scale, ret_gn_gain, proj_pool, proj_ret, w_out, final_norm):
    b = x_prompt.shape[0]
    meta = jnp.broadcast_to(meta_tokens.astype(x_prompt.dtype)[None], (b, N_META, D_MODEL))
    xp = jnp.concatenate([meta, x_prompt], axis=1)
    pos_p = jnp.arange(xp.shape[1], dtype=F32)
    zero_hist = jnp.zeros((b, POOL_HIST, POOL_WIDTH), x_prompt.dtype)
    xs = x_sample
    pos_s = PAST_LEN + jnp.arange(x_sample.shape[1], dtype=F32)
    pool_p, ret_p, pool_s, ret_s = [], [], [], []
    for l in range(DEPTH):
        xp, hp, Sp = mixer_layer(xp, pos_p, 0, zero_hist, None, True, norm_gain[l], w_in[l], pool_w[l],
                                 pool_scale[l], ret_gn_gain[l], proj_pool[l], proj_ret[l], w_out[l])
        xs, hs, Ss = mixer_layer(xs, pos_s, PAST_LEN, state_pool[l], state_ret[l], False, norm_gain[l], w_in[l],
                                 pool_w[l], pool_scale[l], ret_gn_gain[l], proj_pool[l], proj_ret[l], w_out[l])
        pool_p.append(hp)
        ret_p.append(Sp.astype(x_prompt.dtype))
        pool_s.append(hs)
        ret_s.append(Ss.astype(state_ret.dtype))
    y_prompt = rms_norm(xp, final_norm)[:, N_META:]
    y_sample = rms_norm(xs, final_norm)
    new_pool_prompt = jnp.stack(pool_p)
    new_ret_prompt = jnp.stack(ret_p)
    new_pool_sample = jnp.stack(pool_s)
    new_ret_sample = jnp.stack(ret_s)
    return (y_prompt, y_sample, new_pool_prompt, new_ret_prompt, new_pool_sample, new_ret_sample)
```

```python
import functools

import numpy as np
import jax
import jax.numpy as jnp
from jax import lax
from jax.experimental import pallas as pl
from jax.experimental.pallas import tpu as pltpu

F32 = jnp.float32
BF16 = jnp.bfloat16

D_MODEL = 2048
N_META = 16
POOL_WINDOWS = (2, 4, 8, 16)
POOL_GROUP = D_MODEL // len(POOL_WINDOWS)
POOL_HIST = max(POOL_WINDOWS) - 1
HEADS = 8
HEAD_DIM = D_MODEL // HEADS
HALF = HEAD_DIM // 2
CHUNK = 128
ROPE_BASE = 10000.0
EPS = 1e-6
N_SEG = 8
SEG_U, SEG_PG, SEG_Q, SEG_K, SEG_V, SEG_RG, SEG_GP, SEG_GR = range(N_SEG)

TM = 512
VMEM_LIMIT = 56 * 1024 * 1024


def _cparams(sem):
    return pltpu.CompilerParams(dimension_semantics=sem, vmem_limit_bytes=VMEM_LIMIT)


def _dot(a, b):
    return jnp.dot(a, b, preferred_element_type=F32)


def _dot_nt(a, b):
    return lax.dot_general(a, b, (((1,), (1,)), ((), ())), preferred_element_type=F32)


def _dot_tn(a, b):
    return lax.dot_general(a, b, (((0,), (0,)), ((), ())), preferred_element_type=F32)


def _norm_kernel(x_ref, g_ref, h_ref):
    x = x_ref[...]
    y = x * lax.rsqrt(jnp.mean(x * x, axis=-1, keepdims=True) + EPS)
    h_ref[...] = (y * g_ref[...]).astype(BF16)


def _rms_norm_bf16(x, g):
    mp = x.shape[0]
    return pl.pallas_call(
        _norm_kernel,
        out_shape=jax.ShapeDtypeStruct((mp, D_MODEL), BF16),
        grid=(mp // TM,),
        in_specs=[pl.BlockSpec((TM, D_MODEL), lambda i: (i, 0)),
                  pl.BlockSpec((1, D_MODEL), lambda i: (0, 0))],
        out_specs=pl.BlockSpec((TM, D_MODEL), lambda i: (i, 0)),
        compiler_params=_cparams(("arbitrary",)),
        name="rms_norm",
    )(x, g.reshape(1, D_MODEL))


def _inproj_kernel(h_ref, w_ref, cos_ref, sin_ref, o_ref):
    seg = pl.program_id(0)
    o_ref[...] = _dot(h_ref[...], w_ref[...])

    @pl.when((seg == SEG_PG) | (seg == SEG_RG))
    def _():
        z = o_ref[...]
        o_ref[...] = z * jax.nn.sigmoid(z)

    @pl.when(seg >= SEG_GP)
    def _():
        o_ref[...] = jax.nn.sigmoid(o_ref[...])

    @pl.when((seg == SEG_Q) | (seg == SEG_K))
    def _():
        scale = jnp.where(seg == SEG_K, HEAD_DIM ** -0.5, 1.0).astype(F32)
        cos = cos_ref[...]
        sin = sin_ref[...]
        for h in range(HEADS):
            lo = slice(h * HEAD_DIM, h * HEAD_DIM + HALF)
            hi = slice(h * HEAD_DIM + HALF, (h + 1) * HEAD_DIM)
            t1 = o_ref[:, lo]
            t2 = o_ref[:, hi]
            o_ref[:, lo] = (t1 * cos - t2 * sin) * scale
            o_ref[:, hi] = (t2 * cos + t1 * sin) * scale


def _in_projection(h, w_in, cos, sin):
    mp = h.shape[0]
    return pl.pallas_call(
        _inproj_kernel,
        out_shape=jax.ShapeDtypeStruct((mp, N_SEG * D_MODEL), F32),
        grid=(N_SEG, mp // TM),
        in_specs=[pl.BlockSpec((TM, D_MODEL), lambda s, i: (i, 0)),
                  pl.BlockSpec((D_MODEL, D_MODEL), lambda s, i: (0, s)),
                  pl.BlockSpec((TM, HALF), lambda s, i: (i, 0)),
                  pl.BlockSpec((TM, HALF), lambda s, i: (i, 0))],
        out_specs=pl.BlockSpec((TM, D_MODEL), lambda s, i: (i, s)),
        compiler_params=_cparams(("arbitrary", "arbitrary")),
        name="in_projection",
    )(h, w_in, cos, sin)


def _window_sum(ext, w, base, rows):
    s = ext
    size = 1
    while size < w:
        s = s[size:] + s[:-size]
        size *= 2
    start = base - (w - 1)
    return s[start:start + rows]


def _pool_group(g, wsum, u, inv_cnt, pw_ref, ps_ref, spg):
    sl = slice(g * POOL_GROUP, (g + 1) * POOL_GROUP)
    pooled = wsum * inv_cnt - u
    mixed = _dot(pooled.astype(BF16), pw_ref[g])
    return (mixed * ps_ref[:, sl] * spg).astype(BF16)


PT = 256


def _pool_prompt_kernel(prev_ref, meta_ref, u_ref, spg_ref, pw_ref, ps_ref, alias_ref, o_ref):
    del alias_ref
    first = pl.program_id(1) == 0
    for g, w in enumerate(POOL_WINDOWS):
        sl = slice(g * POOL_GROUP, (g + 1) * POOL_GROUP)
        prev = jnp.where(first, meta_ref[:, sl], prev_ref[:, sl])
        u = u_ref[:, sl]
        wsum = _window_sum(jnp.concatenate([prev, u], axis=0), w, N_META, PT)
        o_ref[:, sl] = _pool_group(g, wsum, u, 1.0 / w, pw_ref, ps_ref, spg_ref[:, sl])


def _pool_prompt(z, a_in, pool_w, pool_scale, n_batch, seq, meta_row):
    tiles = seq // PT
    sub = PT // N_META

    def prev_map(b, t):
        return (jnp.maximum(b * (seq // N_META) + t * sub - 1, 0), SEG_U)

    return pl.pallas_call(
        _pool_prompt_kernel,
        out_shape=jax.ShapeDtypeStruct(a_in.shape, BF16),
        grid=(n_batch, tiles),
        in_specs=[pl.BlockSpec((N_META, D_MODEL), prev_map),
                  pl.BlockSpec((N_META, D_MODEL), lambda b, t: (meta_row // N_META, SEG_U)),
                  pl.BlockSpec((PT, D_MODEL), lambda b, t: (b * tiles + t, SEG_U)),
                  pl.BlockSpec((PT, D_MODEL), lambda b, t: (b * tiles + t, SEG_PG)),
                  pl.BlockSpec((len(POOL_WINDOWS), POOL_GROUP, POOL_GROUP), lambda b, t: (0, 0, 0)),
                  pl.BlockSpec((1, D_MODEL), lambda b, t: (0, 0)),
                  pl.BlockSpec(memory_space=pl.ANY)],
        out_specs=pl.BlockSpec((PT, D_MODEL), lambda b, t: (b * tiles + t, 0)),
        input_output_aliases={6: 0},
        compiler_params=_cparams(("arbitrary", "arbitrary")),
        name="pool_prompt",
    )(z, z, z, z, pool_w, pool_scale, a_in)


def _pool_meta_kernel(u_ref, spg_ref, pw_ref, ps_ref, alias_ref, o_ref):
    del alias_ref
    avail = lax.broadcasted_iota(jnp.int32, (N_META, 1), 0).astype(F32) + 1.0
    for g, w in enumerate(POOL_WINDOWS):
        sl = slice(g * POOL_GROUP, (g + 1) * POOL_GROUP)
        u = u_ref[:, sl]
        wsum = _window_sum(jnp.concatenate([jnp.zeros_like(u), u], axis=0), w, N_META, N_META)
        inv_cnt = 1.0 / jnp.minimum(float(w), avail)
        o_ref[:, sl] = _pool_group(g, wsum, u, inv_cnt, pw_ref, ps_ref, spg_ref[:, sl])


def _pool_meta(z, a_in, pool_w, pool_scale, meta_row):
    blk = meta_row // N_META
    return pl.pallas_call(
        _pool_meta_kernel,
        out_shape=jax.ShapeDtypeStruct(a_in.shape, BF16),
        grid=(1,),
        in_specs=[pl.BlockSpec((N_META, D_MODEL), lambda i: (blk, SEG_U)),
                  pl.BlockSpec((N_META, D_MODEL), lambda i: (blk, SEG_PG)),
                  pl.BlockSpec((len(POOL_WINDOWS), POOL_GROUP, POOL_GROUP), lambda i: (0, 0, 0)),
                  pl.BlockSpec((1, D_MODEL), lambda i: (0, 0)),
                  pl.BlockSpec(memory_space=pl.ANY)],
        out_specs=pl.BlockSpec((N_META, D_MODEL), lambda i: (blk, 0)),
        input_output_aliases={4: 0},
        compiler_params=_cparams(("arbitrary",)),
        name="pool_meta",
    )(z, z, pool_w, pool_scale, a_in)


SB = 16
HPAD = POOL_HIST + 1


def _pool_sample_kernel(hist_ref, u_ref, spg_ref, pw_ref, ps_ref, alias_ref, o_ref, *, n_new):
    del alias_ref
    per = HPAD + n_new
    for g, w in enumerate(POOL_WINDOWS):
        sl = slice(g * POOL_GROUP, (g + 1) * POOL_GROUP)
        u = u_ref[:, sl]
        pieces = []
        for b in range(SB):
            pieces.append(hist_ref[b * HPAD:(b + 1) * HPAD, sl])
            pieces.append(u[b * n_new:(b + 1) * n_new])
        ext = jnp.concatenate(pieces, axis=0)
        s = _window_sum(ext, w, HPAD, SB * per - HPAD)
        wsum = jnp.concatenate([s[b * per:b * per + n_new] for b in range(SB)], axis=0)
        o_ref[:, sl] = _pool_group(g, wsum, u, 1.0 / w, pw_ref, ps_ref, spg_ref[:, sl])


def _pool_sample(z, hist_pad, a_in, pool_w, pool_scale, n_seq, n_new, row0):
    rows = SB * n_new
    blk0 = row0 // rows
    return pl.pallas_call(
        functools.partial(_pool_sample_kernel, n_new=n_new),
        out_shape=jax.ShapeDtypeStruct(a_in.shape, BF16),
        grid=(n_seq // SB,),
        in_specs=[pl.BlockSpec((SB * HPAD, D_MODEL), lambda i: (i, 0)),
                  pl.BlockSpec((rows, D_MODEL), lambda i: (blk0 + i, SEG_U)),
                  pl.BlockSpec((rows, D_MODEL), lambda i: (blk0 + i, SEG_PG)),
                  pl.BlockSpec((len(POOL_WINDOWS), POOL_GROUP, POOL_GROUP), lambda i: (0, 0, 0)),
                  pl.BlockSpec((1, D_MODEL), lambda i: (0, 0)),
                  pl.BlockSpec(memory_space=pl.ANY)],
        out_specs=pl.BlockSpec((rows, D_MODEL), lambda i: (blk0 + i, 0)),
        input_output_aliases={5: 0},
        compiler_params=_cparams(("arbitrary",)),
        name="pool_sample",
    )(hist_pad, z, z, pool_w, pool_scale, a_in)


def _head_norm_gate(o, gain, gate):
    mu = jnp.mean(o, axis=-1, keepdims=True)
    oc = o - mu
    on = oc * lax.rsqrt(jnp.mean(oc * oc, axis=-1, keepdims=True) + EPS)
    return ((on * gain) * gate).astype(BF16)


def _ret_chunk_kernel(dec_ref, q_ref, k_ref, v_ref, srg_ref, gn_ref, s0_ref, alias_ref,
                      o_ref, sfin_ref, s_scr, *, n_valid):
    del alias_ref
    c = pl.program_id(1)

    @pl.when(c == 0)
    def _():
        s_scr[...] = s0_ref[...]

    n = CHUNK
    row = lax.broadcasted_iota(jnp.int32, (n, n), 0)
    col = lax.broadcasted_iota(jnp.int32, (n, n), 1)
    causal = row >= col
    diff = jnp.where(causal, row - col, 0).astype(F32)
    ridx = lax.broadcasted_iota(jnp.int32, (n, 1), 0).astype(F32)
    for h in range(HEADS):
        hs = slice(h * HEAD_DIM, (h + 1) * HEAD_DIM)
        lg = dec_ref[0, h]
        q = q_ref[:, hs].astype(BF16)
        k = k_ref[:, hs]
        v = v_ref[:, hs].astype(BF16)
        s_old = s_scr[h]
        decay = jnp.where(causal, jnp.exp(diff * lg), 0.0)
        scores = _dot_nt(q, k.astype(BF16)) * decay
        intra = _dot(scores.astype(BF16), v)
        inter = _dot(q, s_old.astype(BF16)) * jnp.exp((ridx + 1.0) * lg)
        k_dec = (k * jnp.exp((n_valid - 1.0 - ridx) * lg)).astype(BF16)
        s_scr[h] = s_old * dec_ref[1, h] + _dot_tn(k_dec, v)
        o_ref[:, hs] = _head_norm_gate(intra + inter, gn_ref[:, hs], srg_ref[:, hs])

    @pl.when(c == pl.num_programs(1) - 1)
    def _():
        sfin_ref[0] = s_scr[...]


def _ret_chunks(z, b_in, lg, gn, s0, n_batch, n_chunks, row0, n_valid):
    blk0 = row0 // CHUNK
    dec = jnp.stack([lg, jnp.exp(n_valid * lg)])

    def zmap(seg):
        return lambda b, c: (blk0 + b * n_chunks + c, seg)

    state = (HEADS, HEAD_DIM, HEAD_DIM)
    return pl.pallas_call(
        functools.partial(_ret_chunk_kernel, n_valid=n_valid),
        out_shape=(jax.ShapeDtypeStruct(b_in.shape, BF16),
                   jax.ShapeDtypeStruct((n_batch,) + state, F32)),
        grid=(n_batch, n_chunks),
        in_specs=[pl.BlockSpec(memory_space=pltpu.SMEM),
                  pl.BlockSpec((CHUNK, D_MODEL), zmap(SEG_Q)),
                  pl.BlockSpec((CHUNK, D_MODEL), zmap(SEG_K)),
                  pl.BlockSpec((CHUNK, D_MODEL), zmap(SEG_V)),
                  pl.BlockSpec((CHUNK, D_MODEL), zmap(SEG_RG)),
                  pl.BlockSpec((1, D_MODEL), lambda b, c: (0, 0)),
                  pl.BlockSpec(state, lambda b, c: (0, 0, 0)),
                  pl.BlockSpec(memory_space=pl.ANY)],
        out_specs=(pl.BlockSpec((CHUNK, D_MODEL), zmap(0)),
                   pl.BlockSpec((1,) + state, lambda b, c: (b, 0, 0, 0))),
        scratch_shapes=[pltpu.VMEM(state, F32)],
        input_output_aliases={7: 0},
        compiler_params=_cparams(("arbitrary", "arbitrary")),
        name="retention_chunks",
    )(dec, z, z, z, z, gn, s0, b_in)


RB = 2


def _ret_sample_kernel(gam_ref, q_ref, k_ref, v_ref, srg_ref, gn_ref, dmask_ref, qdec_ref, kdec_ref,
                       s_ref, alias_ref, o_ref, sout_ref, *, n_new):
    del alias_ref
    pairs = [(b, h) for b in range(RB) for h in range(HEADS)]

    def stack(ref):
        return jnp.concatenate(
            [ref[b * n_new:(b + 1) * n_new, h * HEAD_DIM:(h + 1) * HEAD_DIM] for b, h in pairs], axis=0)

    k = stack(k_ref)
    qb = stack(q_ref).astype(BF16)
    v = stack(v_ref)
    vb = v.astype(BF16)
    scores = _dot_nt(qb, k.astype(BF16)) * dmask_ref[...]
    intra = _dot(scores.astype(BF16), vb)
    k_dec = (k * kdec_ref[...]).astype(BF16)
    rows = lax.broadcasted_iota(jnp.int32, v.shape, 0)
    is_even = (rows & n_new) == 0
    v_even = jnp.where(is_even, v, 0.0).astype(BF16)
    v_odd = jnp.where(is_even, 0.0, v).astype(BF16)
    grp = 2 * n_new
    inter_parts = []
    for idx, (b, h) in enumerate(pairs):
        g0 = (idx // 2) * grp
        off = (idx % 2) * n_new
        s_old = s_ref[b, h]
        inter = _dot(qb[g0:g0 + grp], s_old.astype(BF16))
        inter_parts.append(inter[off:off + n_new])
        v_sel = v_even if idx % 2 == 0 else v_odd
        sout_ref[b, h] = s_old * gam_ref[h] + _dot_tn(k_dec[g0:g0 + grp], v_sel[g0:g0 + grp])
    o = intra + jnp.concatenate(inter_parts, axis=0) * qdec_ref[...]
    gain = jnp.concatenate(
        [jnp.broadcast_to(gn_ref[:, h * HEAD_DIM:(h + 1) * HEAD_DIM], (n_new, HEAD_DIM)) for _, h in pairs],
        axis=0)
    out = _head_norm_gate(o, gain, stack(srg_ref)).astype(F32)
    seqs = []
    for b in range(RB):
        seqs.append(jnp.concatenate(
            [out[(b * HEADS + h) * n_new:(b * HEADS + h + 1) * n_new] for h in range(HEADS)], axis=1))
    o_ref[...] = jnp.concatenate(seqs, axis=0).astype(BF16)


def _ret_sample(z, b_in, gam, gn, dmask, qdec, kdec, state, n_seq, n_new, row0):
    rows = RB * n_new
    blk0 = row0 // rows
    stack_rows = RB * HEADS * n_new

    def zmap(seg):
        return lambda i: (blk0 + i, seg)

    sblk = (RB, HEADS, HEAD_DIM, HEAD_DIM)
    return pl.pallas_call(
        functools.partial(_ret_sample_kernel, n_new=n_new),
        out_shape=(jax.ShapeDtypeStruct(b_in.shape, BF16),
                   jax.ShapeDtypeStruct(state.shape, F32)),
        grid=(n_seq // RB,),
        in_specs=[pl.BlockSpec(memory_space=pltpu.SMEM),
                  pl.BlockSpec((rows, D_MODEL), zmap(SEG_Q)),
                  pl.BlockSpec((rows, D_MODEL), zmap(SEG_K)),
                  pl.BlockSpec((rows, D_MODEL), zmap(SEG_V)),
                  pl.BlockSpec((rows, D_MODEL), zmap(SEG_RG)),
                  pl.BlockSpec((1, D_MODEL), lambda i: (0, 0)),
                  pl.BlockSpec((stack_rows, stack_rows), lambda i: (0, 0)),
                  pl.BlockSpec((stack_rows, HEAD_DIM), lambda i: (0, 0)),
                  pl.BlockSpec((stack_rows, HEAD_DIM), lambda i: (0, 0)),
                  pl.BlockSpec(sblk, lambda i: (i, 0, 0, 0)),
                  pl.BlockSpec(memory_space=pl.ANY)],
        out_specs=(pl.BlockSpec((rows, D_MODEL), zmap(0)),
                   pl.BlockSpec(sblk, lambda i: (i, 0, 0, 0))),
        input_output_aliases={10: 0},
        compiler_params=_cparams(("arbitrary",)),
        name="retention_sample",
    )(gam, z, z, z, z, gn, dmask, qdec, kdec, state, b_in)


TN_MERGE = 1024


def _merge_kernel(a_ref, b_ref, gp_ref, gr_ref, wp_ref, wr_ref, o_ref):
    pool_branch = _dot(a_ref[...], wp_ref[...])
    ret_branch = _dot(b_ref[...], wr_ref[...])
    o_ref[...] = (gp_ref[...] * pool_branch + gr_ref[...] * ret_branch).astype(BF16)


def _merge(a_in, b_in, z, proj_pool, proj_ret):
    mp = a_in.shape[0]
    nt = D_MODEL // TN_MERGE
    return pl.pallas_call(
        _merge_kernel,
        out_shape=jax.ShapeDtypeStruct((mp, D_MODEL), BF16),
        grid=(nt, mp // TM),
        in_specs=[pl.BlockSpec((TM, D_MODEL), lambda j, i: (i, 0)),
                  pl.BlockSpec((TM, D_MODEL), lambda j, i: (i, 0)),
                  pl.BlockSpec((TM, TN_MERGE), lambda j, i: (i, SEG_GP * nt + j)),
                  pl.BlockSpec((TM, TN_MERGE), lambda j, i: (i, SEG_GR * nt + j)),
                  pl.BlockSpec((D_MODEL, TN_MERGE), lambda j, i: (0, j)),
                  pl.BlockSpec((D_MODEL, TN_MERGE), lambda j, i: (0, j))],
        out_specs=pl.BlockSpec((TM, TN_MERGE), lambda j, i: (i, j)),
        compiler_params=_cparams(("arbitrary", "arbitrary")),
        name="merge_branches",
    )(a_in, b_in, z, z, proj_pool, proj_ret)


def _out_kernel(m_ref, w_ref, x_ref, g_ref, xo_ref, ho_ref):
    x = x_ref[...] + _dot(m_ref[...], w_ref[...])
    xo_ref[...] = x
    y = x * lax.rsqrt(jnp.mean(x * x, axis=-1, keepdims=True) + EPS)
    ho_ref[...] = (y * g_ref[...]).astype(ho_ref.dtype)


def _out_projection(merged, w_out, x, g_next, norm_dtype):
    mp = x.shape[0]
    return pl.pallas_call(
        _out_kernel,
        out_shape=(jax.ShapeDtypeStruct((mp, D_MODEL), F32),
                   jax.ShapeDtypeStruct((mp, D_MODEL), norm_dtype)),
        grid=(mp // TM,),
        in_specs=[pl.BlockSpec((TM, D_MODEL), lambda i: (i, 0)),
                  pl.BlockSpec((D_MODEL, D_MODEL), lambda i: (0, 0)),
                  pl.BlockSpec((TM, D_MODEL), lambda i: (i, 0)),
                  pl.BlockSpec((1, D_MODEL), lambda i: (0, 0))],
        out_specs=(pl.BlockSpec((TM, D_MODEL), lambda i: (i, 0)),
                   pl.BlockSpec((TM, D_MODEL), lambda i: (i, 0))),
        compiler_params=_cparams(("arbitrary",)),
        name="out_projection",
    )(merged, w_out, x, g_next.reshape(1, D_MODEL))


def _log_decay():
    return jnp.log1p(-jnp.exp2(-5.0 - jnp.arange(HEADS, dtype=F32)))


def _rope_tables(pos):
    inv_freq = ROPE_BASE ** (-jnp.arange(HALF, dtype=F32) / HALF)
    ang = pos[:, None] * inv_freq[None, :]
    return jnp.cos(ang), jnp.sin(ang)


def _sample_decay_tables(lg, n_new):
    r = jnp.arange(RB * HEADS * n_new)
    head = (r // n_new) % HEADS
    tok = (r % n_new).astype(F32)
    lg_r = lg[head]
    same = (r[:, None] // n_new) == (r[None, :] // n_new)
    diff = tok[:, None] - tok[None, :]
    keep = same & (diff >= 0)
    dmask = jnp.where(keep, jnp.exp(jnp.where(keep, diff, 0.0) * lg_r[:, None]), 0.0)
    qdec = jnp.broadcast_to(jnp.exp((tok + 1.0) * lg_r)[:, None], (r.shape[0], HEAD_DIM))
    kdec = jnp.broadcast_to(jnp.exp((n_new - 1.0 - tok) * lg_r)[:, None], (r.shape[0], HEAD_DIM))
    return dmask, qdec, kdec


def kernel(x_prompt, x_sample, state_pool, state_ret, meta_tokens, norm_gain, w_in, pool_w, pool_scale,
           ret_gn_gain, proj_pool, proj_ret, w_out, final_norm):
    n_batch, seq = x_prompt.shape[:2]
    n_seq, n_new = x_sample.shape[:2]
    depth = norm_gain.shape[0]
    past_len = 16384
    assert seq % PT == 0 and seq % CHUNK == 0 and n_seq % SB == 0 and n_seq % RB == 0
    assert n_new == 8, "sample tokens of one sequence must fill one f32 sublane tile"
    rows_prompt = n_batch * seq
    rows_sample = n_seq * n_new
    row_sample = rows_prompt
    row_meta = rows_prompt + rows_sample
    assert row_meta % CHUNK == 0 and row_sample % (SB * n_new) == 0
    mp = -(-(row_meta + CHUNK) // TM) * TM

    x = jnp.concatenate([
        x_prompt.reshape(rows_prompt, D_MODEL),
        x_sample.reshape(rows_sample, D_MODEL),
        meta_tokens.astype(x_prompt.dtype),
        jnp.zeros((mp - row_meta - N_META, D_MODEL), x_prompt.dtype)], axis=0)

    pos = jnp.concatenate([
        jnp.tile(N_META + jnp.arange(seq, dtype=F32), n_batch),
        jnp.tile(past_len + jnp.arange(n_new, dtype=F32), n_seq),
        jnp.arange(N_META, dtype=F32),
        jnp.zeros((mp - row_meta - N_META,), F32)])
    cos, sin = _rope_tables(pos)
    lg = _log_decay()
    gam_new = jnp.exp(n_new * lg)
    dmask, qdec, kdec = _sample_decay_tables(lg, n_new)
    hist_pad = jnp.pad(state_pool.astype(F32), ((0, 0), (0, 0), (1, 0), (0, 0))).reshape(
        depth, n_seq * HPAD, D_MODEL)
    zero_state = jnp.zeros((HEADS, HEAD_DIM, HEAD_DIM), F32)

    h = _rms_norm_bf16(x, norm_gain[0])
    pool_p, ret_p, pool_s, ret_s = [], [], [], []
    for l in range(depth):
        z = _in_projection(h, w_in[l].astype(BF16), cos, sin)
        pw = pool_w[l].astype(BF16)
        ps = pool_scale[l].reshape(1, D_MODEL).astype(F32)
        gn = ret_gn_gain[l].reshape(1, D_MODEL).astype(F32)

        a_in = jnp.zeros((mp, D_MODEL), BF16)
        a_in = _pool_meta(z, a_in, pw, ps, row_meta)
        a_in = _pool_prompt(z, a_in, pw, ps, n_batch, seq, row_meta)
        a_in = _pool_sample(z, hist_pad[l], a_in, pw, ps, n_seq, n_new, row_sample)

        b_in = jnp.zeros((mp, D_MODEL), BF16)
        b_in, s_meta = _ret_chunks(z, b_in, lg, gn, zero_state, 1, 1, row_meta, N_META)
        b_in, s_prompt = _ret_chunks(z, b_in, lg, gn, s_meta[0], n_batch, seq // CHUNK, 0, CHUNK)
        b_in, s_sample = _ret_sample(z, b_in, gam_new, gn, dmask, qdec, kdec, state_ret[l].astype(F32),
                                     n_seq, n_new, row_sample)

        merged = _merge(a_in, b_in, z, proj_pool[l].astype(BF16), proj_ret[l].astype(BF16))
        last = l == depth - 1
        g_next = final_norm if last else norm_gain[l + 1]
        x, h = _out_projection(merged, w_out[l].astype(BF16), x, g_next, F32 if last else BF16)

        u_prompt = z[:rows_prompt, :D_MODEL].reshape(n_batch, seq, D_MODEL)
        u_sample = z[row_sample:row_meta, :D_MODEL].reshape(n_seq, n_new, D_MODEL)
        pool_p.append(u_prompt[:, seq - POOL_HIST:])
        pool_s.append(jnp.concatenate([state_pool[l], u_sample], axis=1)[:, -POOL_HIST:])
        ret_p.append(s_prompt)
        ret_s.append(s_sample.astype(state_ret.dtype))

    y_prompt = h[:rows_prompt].reshape(n_batch, seq, D_MODEL)
    y_sample = h[row_sample:row_meta].reshape(n_seq, n_new, D_MODEL)
    return (y_prompt, y_sample, jnp.stack(pool_p), jnp.stack(ret_p), jnp.stack(pool_s), jnp.stack(ret_s))
```

```python
import functools

import jax
import jax.numpy as jnp
from jax import lax
from jax.experimental import pallas as pl
from jax.experimental.pallas import tpu as pltpu

F32 = jnp.float32
BF16 = jnp.bfloat16

D_MODEL = 2048
N_META = 16
POOL_WINDOWS = (2, 4, 8, 16)
N_GROUPS = len(POOL_WINDOWS)
POOL_GROUP = D_MODEL // N_GROUPS
POOL_HIST = max(POOL_WINDOWS) - 1
HEADS = 8
HEAD_DIM = D_MODEL // HEADS
HALF = HEAD_DIM // 2
CHUNK = 128
ROPE_BASE = 10000.0
EPS = 1e-6
PAST_LEN = 16384
SEG_U, SEG_PG, SEG_Q, SEG_K, SEG_V, SEG_RG, SEG_GP, SEG_GR = range(8)

TM = 512
TN_IN = 1024
VMEM_LIMIT = 56 * 1024 * 1024


def _cparams(*sem):
    return pltpu.CompilerParams(dimension_semantics=sem, vmem_limit_bytes=VMEM_LIMIT)


def _dot(a, b):
    return jnp.dot(a, b, preferred_element_type=F32)


def _dot_nt(a, b):
    return lax.dot_general(a, b, (((1,), (1,)), ((), ())), preferred_element_type=F32)


def _dot_tn(a, b):
    return lax.dot_general(a, b, (((0,), (0,)), ((), ())), preferred_element_type=F32)


def _rms(x, g):
    return x * lax.rsqrt(jnp.mean(x * x, axis=-1, keepdims=True) + EPS) * g


def _x_specs(n_p, n_s):
    return [pl.BlockSpec((TM, D_MODEL), lambda i: (jnp.minimum(i, n_p - 1), 0)),
            pl.BlockSpec((TM, D_MODEL), lambda i: (jnp.clip(i - n_p, 0, n_s - 1), 0)),
            pl.BlockSpec((N_META, D_MODEL), lambda i: (0, 0))]


def _meta_rows(meta_ref):
    return jnp.concatenate([meta_ref[...], jnp.zeros((CHUNK - N_META, D_MODEL), F32)], axis=0)


def _norm_kernel(xp_ref, xs_ref, meta_ref, g_ref, h_ref, *, n_p, n_s):
    i = pl.program_id(0)

    @pl.when(i < n_p)
    def _():
        h_ref[...] = _rms(xp_ref[...], g_ref[...]).astype(BF16)

    @pl.when((i >= n_p) & (i < n_p + n_s))
    def _():
        h_ref[...] = _rms(xs_ref[...], g_ref[...]).astype(BF16)

    @pl.when(i == n_p + n_s)
    def _():
        h_ref[0:CHUNK] = _rms(_meta_rows(meta_ref), g_ref[...]).astype(BF16)
        h_ref[CHUNK:] = jnp.zeros((TM - CHUNK, D_MODEL), BF16)


def _first_norm(xp, xs, meta, g, n_p, n_s):
    tiles = n_p + n_s + 1
    return pl.pallas_call(
        functools.partial(_norm_kernel, n_p=n_p, n_s=n_s),
        out_shape=jax.ShapeDtypeStruct((tiles * TM, D_MODEL), BF16),
        grid=(tiles,),
        in_specs=_x_specs(n_p, n_s) + [pl.BlockSpec((1, D_MODEL), lambda i: (0, 0))],
        out_specs=pl.BlockSpec((TM, D_MODEL), lambda i: (i, 0)),
        compiler_params=_cparams("arbitrary"),
        name="first_norm",
    )(xp, xs, meta, g)


def _inproj_kernel(h_ref, w_ref, *rest, epilogue, n_full):
    if epilogue == "rope":
        cos_ref, sin_ref, o_ref, w_scr = rest
    else:
        o_ref, w_scr = rest
    s = pl.program_id(0)
    i = pl.program_id(1)

    @pl.when(i == 0)
    def _():
        w_scr[...] = w_ref[...].astype(BF16)

    def tile(rows):
        z = _dot(h_ref[0:rows], w_scr[...])
        if epilogue == "silu":
            o_ref[0:rows] = z * jax.nn.sigmoid(z)
        elif epilogue == "sigmoid":
            o_ref[0:rows] = jax.nn.sigmoid(z)
        elif epilogue == "rope":
            is_k = s >= D_MODEL // TN_IN
            scale = jnp.where(is_k, HEAD_DIM ** -0.5, 1.0).astype(F32)
            cos = cos_ref[0:rows]
            sin = sin_ref[0:rows]
            for hh in range(TN_IN // HEAD_DIM):
                lo = slice(hh * HEAD_DIM, hh * HEAD_DIM + HALF)
                hi = slice(hh * HEAD_DIM + HALF, (hh + 1) * HEAD_DIM)
                t1 = z[:, lo]
                t2 = z[:, hi]
                o_ref[0:rows, lo] = (t1 * cos - t2 * sin) * scale
                o_ref[0:rows, hi] = (t2 * cos + t1 * sin) * scale
        else:
            o_ref[0:rows] = z

    @pl.when(i < n_full)
    def _():
        tile(TM)

    @pl.when(i == n_full)
    def _():
        tile(CHUNK)
        o_ref[CHUNK:] = jnp.zeros((TM - CHUNK, TN_IN), F32)


def _in_projection(h, w_in, seg_a, seg_b, epilogue, cos, sin):
    mp = h.shape[0]
    tiles = mp // TM
    per_seg = D_MODEL // TN_IN

    def wmap(s, i):
        return (0, (seg_a + (s // per_seg) * (seg_b - seg_a)) * per_seg + s % per_seg)

    in_specs = [pl.BlockSpec((TM, D_MODEL), lambda s, i: (i, 0)),
                pl.BlockSpec((D_MODEL, TN_IN), wmap)]
    args = [h, w_in]
    if epilogue == "rope":
        in_specs += [pl.BlockSpec((TM, HALF), lambda s, i: (i, 0)) for _ in range(2)]
        args += [cos, sin]
    return pl.pallas_call(
        functools.partial(_inproj_kernel, epilogue=epilogue, n_full=tiles - 1),
        out_shape=jax.ShapeDtypeStruct((mp, 2 * D_MODEL), F32),
        grid=(2 * per_seg, tiles),
        in_specs=in_specs,
        out_specs=pl.BlockSpec((TM, TN_IN), lambda s, i: (i, s)),
        scratch_shapes=[pltpu.VMEM((D_MODEL, TN_IN), BF16)],
        compiler_params=_cparams("arbitrary", "arbitrary"),
        name="in_projection_" + epilogue,
    )(*args)


def _window_sum(ext, w, base, rows):
    s = ext
    size = 1
    while size < w:
        s = s[size:] + s[:-size]
        size *= 2
    start = base - (w - 1)
    return s[start:start + rows]


def _pool_group(g, wsum, u, inv_cnt, pw_ref, ps_ref, spg):
    sl = slice(g * POOL_GROUP, (g + 1) * POOL_GROUP)
    pooled = wsum * inv_cnt - u
    mixed = _dot(pooled.astype(BF16), pw_ref[g])
    return (mixed * ps_ref[:, sl] * spg).astype(BF16)


_POOL_W_SPEC = ((N_GROUPS, POOL_GROUP, POOL_GROUP), (0, 0, 0))


def _pool_meta_kernel(u_ref, spg_ref, pw_ref, ps_ref, o_ref):
    avail = lax.broadcasted_iota(jnp.int32, (N_META, 1), 0).astype(F32) + 1.0
    o_ref[N_META:] = jnp.zeros((CHUNK - N_META, D_MODEL), BF16)
    for g, w in enumerate(POOL_WINDOWS):
        sl = slice(g * POOL_GROUP, (g + 1) * POOL_GROUP)
        u = u_ref[:, sl]
        wsum = _window_sum(jnp.concatenate([jnp.zeros_like(u), u], axis=0), w, N_META, N_META)
        inv_cnt = 1.0 / jnp.minimum(float(w), avail)
        o_ref[0:N_META, sl] = _pool_group(g, wsum, u, inv_cnt, pw_ref, ps_ref, spg_ref[:, sl])


def _pool_meta(z_plain, z_silu, pool_w, pool_scale, row_meta):
    mp = z_plain.shape[0]
    return pl.pallas_call(
        _pool_meta_kernel,
        out_shape=jax.ShapeDtypeStruct((mp, D_MODEL), BF16),
        grid=(1,),
        in_specs=[pl.BlockSpec((N_META, D_MODEL), lambda i: (row_meta // N_META, 0)),
                  pl.BlockSpec((N_META, D_MODEL), lambda i: (row_meta // N_META, 0)),
                  pl.BlockSpec(_POOL_W_SPEC[0], lambda i: _POOL_W_SPEC[1]),
                  pl.BlockSpec((1, D_MODEL), lambda i: (0, 0))],
        out_specs=pl.BlockSpec((CHUNK, D_MODEL), lambda i: (row_meta // CHUNK, 0)),
        compiler_params=_cparams("arbitrary"),
        name="pool_meta",
    )(z_plain, z_silu, pool_w, pool_scale)


PT = 256


def _pool_prompt_kernel(prev_ref, meta_ref, u_ref, spg_ref, pw_ref, ps_ref, alias_ref, o_ref, hist_ref):
    del alias_ref
    t = pl.program_id(1)
    for g, w in enumerate(POOL_WINDOWS):
        sl = slice(g * POOL_GROUP, (g + 1) * POOL_GROUP)
        prev = jnp.where(t == 0, meta_ref[:, sl], prev_ref[:, sl])
        u = u_ref[:, sl]
        wsum = _window_sum(jnp.concatenate([prev, u], axis=0), w, N_META, PT)
        o_ref[:, sl] = _pool_group(g, wsum, u, 1.0 / w, pw_ref, ps_ref, spg_ref[:, sl])

    @pl.when(t == pl.num_programs(1) - 1)
    def _():
        hist_ref[0] = u_ref[PT - POOL_HIST:PT, :]


def _pool_prompt(z_plain, z_silu, a_in, pool_w, pool_scale, n_batch, seq, row_meta):
    tiles = seq // PT
    sub = PT // N_META

    def prev_map(b, t):
        return (jnp.maximum(b * (seq // N_META) + t * sub - 1, 0), 0)

    return pl.pallas_call(
        _pool_prompt_kernel,
        out_shape=(jax.ShapeDtypeStruct(a_in.shape, BF16),
                   jax.ShapeDtypeStruct((n_batch, POOL_HIST, D_MODEL), F32)),
        grid=(n_batch, tiles),
        in_specs=[pl.BlockSpec((N_META, D_MODEL), prev_map),
                  pl.BlockSpec((N_META, D_MODEL), lambda b, t: (row_meta // N_META, 0)),
                  pl.BlockSpec((PT, D_MODEL), lambda b, t: (b * tiles + t, 0)),
                  pl.BlockSpec((PT, D_MODEL), lambda b, t: (b * tiles + t, 0)),
                  pl.BlockSpec(_POOL_W_SPEC[0], lambda b, t: _POOL_W_SPEC[1]),
                  pl.BlockSpec((1, D_MODEL), lambda b, t: (0, 0)),
                  pl.BlockSpec(memory_space=pl.ANY)],
        out_specs=(pl.BlockSpec((PT, D_MODEL), lambda b, t: (b * tiles + t, 0)),
                   pl.BlockSpec((1, POOL_HIST, D_MODEL), lambda b, t: (b, 0, 0))),
        input_output_aliases={6: 0},
        compiler_params=_cparams("arbitrary", "arbitrary"),
        name="pool_prompt",
    )(z_plain, z_plain, z_plain, z_silu, pool_w, pool_scale, a_in)


SB = 16


def _pool_sample_kernel(hist_ref, u_ref, spg_ref, pw_ref, ps_ref, *rest, n_new):
    o_ref, nh_ref = rest[-2:]
    per = 1 + POOL_HIST + n_new
    zero_row = jnp.zeros((1, POOL_GROUP), F32)
    for g, w in enumerate(POOL_WINDOWS):
        sl = slice(g * POOL_GROUP, (g + 1) * POOL_GROUP)
        u = u_ref[:, sl]
        pieces = []
        for b in range(SB):
            pieces += [zero_row, hist_ref[b, :, sl], u[b * n_new:(b + 1) * n_new]]
        ext = jnp.concatenate(pieces, axis=0)
        for b in range(SB):
            nh_ref[b, :, sl] = ext[(b + 1) * per - POOL_HIST:(b + 1) * per]
        s = _window_sum(ext, w, 1 + POOL_HIST, SB * per - 1 - POOL_HIST)
        wsum = jnp.concatenate([s[b * per:b * per + n_new] for b in range(SB)], axis=0)
        o_ref[:, sl] = _pool_group(g, wsum, u, 1.0 / w, pw_ref, ps_ref, spg_ref[:, sl])


def _pool_sample(z_plain, z_silu, state_pool, a_in, new_hist, pool_w, pool_scale, layer, n_seq, n_new, row0):
    rows = SB * n_new
    blk0 = row0 // rows
    hist_blk = (None, SB, POOL_HIST, D_MODEL)
    in_specs = [pl.BlockSpec(hist_blk, lambda i: (layer, i, 0, 0)),
                pl.BlockSpec((rows, D_MODEL), lambda i: (blk0 + i, 0)),
                pl.BlockSpec((rows, D_MODEL), lambda i: (blk0 + i, 0)),
                pl.BlockSpec(_POOL_W_SPEC[0], lambda i: _POOL_W_SPEC[1]),
                pl.BlockSpec((1, D_MODEL), lambda i: (0, 0)),
                pl.BlockSpec(memory_space=pl.ANY)]
    args = [state_pool, z_plain, z_silu, pool_w, pool_scale, a_in]
    aliases = {5: 0}
    if new_hist is not None:
        in_specs.append(pl.BlockSpec(memory_space=pl.ANY))
        args.append(new_hist)
        aliases[6] = 1
    return pl.pallas_call(
        functools.partial(_pool_sample_kernel, n_new=n_new),
        out_shape=(jax.ShapeDtypeStruct(a_in.shape, BF16),
                   jax.ShapeDtypeStruct(state_pool.shape, F32)),
        grid=(n_seq // SB,),
        in_specs=in_specs,
        out_specs=(pl.BlockSpec((rows, D_MODEL), lambda i: (blk0 + i, 0)),
                   pl.BlockSpec(hist_blk, lambda i: (layer, i, 0, 0))),
        input_output_aliases=aliases,
        compiler_params=_cparams("arbitrary"),
        name="pool_sample",
    )(*args)


def _head_norm_gate(o, gain, gate):
    mu = jnp.mean(o, axis=-1, keepdims=True)
    oc = o - mu
    on = oc * lax.rsqrt(jnp.mean(oc * oc, axis=-1, keepdims=True) + EPS)
    return ((on * gain) * gate).astype(BF16)


_STATE = (HEADS, HEAD_DIM, HEAD_DIM)


def _ret_chunk_kernel(dec_ref, q_ref, k_ref, v_ref, srg_ref, gn_ref, s0_ref, *rest, n_valid):
    o_ref, sfin_ref, s_scr = rest[-3:]
    c = pl.program_id(1)

    @pl.when(c == 0)
    def _():
        s_scr[...] = s0_ref[...]

    n = CHUNK
    row = lax.broadcasted_iota(jnp.int32, (n, n), 0)
    col = lax.broadcasted_iota(jnp.int32, (n, n), 1)
    causal = row >= col
    diff = jnp.where(causal, row - col, 0).astype(F32)
    ridx = lax.broadcasted_iota(jnp.int32, (n, 1), 0).astype(F32)
    for h in range(HEADS):
        hs = slice(h * HEAD_DIM, (h + 1) * HEAD_DIM)
        lg = dec_ref[0, h]
        q = q_ref[:, hs].astype(BF16)
        k = k_ref[:, hs]
        v = v_ref[:, hs].astype(BF16)
        s_old = s_scr[h]
        decay = jnp.where(causal, jnp.exp(diff * lg), 0.0)
        scores = _dot_nt(q, k.astype(BF16)) * decay
        intra = _dot(scores.astype(BF16), v)
        inter = _dot(q, s_old.astype(BF16)) * jnp.exp((ridx + 1.0) * lg)
        k_dec = (k * jnp.exp((n_valid - 1.0 - ridx) * lg)).astype(BF16)
        s_scr[h] = s_old * dec_ref[1, h] + _dot_tn(k_dec, v)
        o_ref[:, hs] = _head_norm_gate(intra + inter, gn_ref[:, hs], srg_ref[:, hs])

    @pl.when(c == pl.num_programs(1) - 1)
    def _():
        sfin_ref[...] = s_scr[...]


def _ret_chunks(z_rope, z_plain, z_silu, lg, gn, s0, b_in, fin, fin_shape, fin_index, n_batch, n_chunks,
                row0, n_valid):
    blk0 = row0 // CHUNK
    dec = jnp.stack([lg, jnp.exp(n_valid * lg)])

    def zmap(col):
        return lambda b, c: (blk0 + b * n_chunks + c, col)

    in_specs = [pl.BlockSpec(memory_space=pltpu.SMEM),
                pl.BlockSpec((CHUNK, D_MODEL), zmap(0)),
                pl.BlockSpec((CHUNK, D_MODEL), zmap(1)),
                pl.BlockSpec((CHUNK, D_MODEL), zmap(1)),
                pl.BlockSpec((CHUNK, D_MODEL), zmap(1)),
                pl.BlockSpec((1, D_MODEL), lambda b, c: (0, 0)),
                pl.BlockSpec(_STATE, lambda b, c: (0, 0, 0))]
    args = [dec, z_rope, z_rope, z_plain, z_silu, gn, s0]
    aliases = {}
    for out_idx, buf in enumerate((b_in, fin)):
        if buf is not None:
            aliases[len(args)] = out_idx
            in_specs.append(pl.BlockSpec(memory_space=pl.ANY))
            args.append(buf)
    fin_blk = (None,) * (len(fin_shape) - 3) + _STATE
    return pl.pallas_call(
        functools.partial(_ret_chunk_kernel, n_valid=n_valid),
        out_shape=(jax.ShapeDtypeStruct((z_rope.shape[0], D_MODEL), BF16),
                   jax.ShapeDtypeStruct(fin_shape, F32)),
        grid=(n_batch, n_chunks),
        in_specs=in_specs,
        out_specs=(pl.BlockSpec((CHUNK, D_MODEL), zmap(0)),
                   pl.BlockSpec(fin_blk, lambda b, c: fin_index(b) + (0, 0, 0))),
        scratch_shapes=[pltpu.VMEM(_STATE, F32)],
        input_output_aliases=aliases,
        compiler_params=_cparams("arbitrary", "arbitrary"),
        name="retention_chunks",
    )(*args)


RB = 2


def _ret_sample_kernel(gam_ref, q_ref, k_ref, v_ref, srg_ref, gn_ref, dmask_ref, qdec_ref, kdec_ref,
                       s_ref, *rest, n_new):
    o_ref, sout_ref = rest[-2:]
    pairs = [(b, h) for b in range(RB) for h in range(HEADS)]

    def stack(ref):
        return jnp.concatenate(
            [ref[b * n_new:(b + 1) * n_new, h * HEAD_DIM:(h + 1) * HEAD_DIM] for b, h in pairs], axis=0)

    k = stack(k_ref)
    qb = stack(q_ref).astype(BF16)
    v = stack(v_ref)
    vb = v.astype(BF16)
    scores = _dot_nt(qb, k.astype(BF16)) * dmask_ref[...]
    intra = _dot(scores.astype(BF16), vb)
    k_dec = (k * kdec_ref[...]).astype(BF16)
    rows = lax.broadcasted_iota(jnp.int32, v.shape, 0)
    is_even = (rows & n_new) == 0
    v_even = jnp.where(is_even, v, 0.0).astype(BF16)
    v_odd = jnp.where(is_even, 0.0, v).astype(BF16)
    grp = 2 * n_new
    inter_parts = []
    for idx, (b, h) in enumerate(pairs):
        g0 = (idx // 2) * grp
        off = (idx % 2) * n_new
        s_old = s_ref[b, h]
        inter = _dot(qb[g0:g0 + grp], s_old.astype(BF16))
        inter_parts.append(inter[off:off + n_new])
        v_sel = v_even if idx % 2 == 0 else v_odd
        sout_ref[b, h] = s_old * gam_ref[h] + _dot_tn(k_dec[g0:g0 + grp], v_sel[g0:g0 + grp])
    o = intra + jnp.concatenate(inter_parts, axis=0) * qdec_ref[...]
    gain = jnp.concatenate(
        [jnp.broadcast_to(gn_ref[:, h * HEAD_DIM:(h + 1) * HEAD_DIM], (n_new, HEAD_DIM)) for _, h in pairs],
        axis=0)
    out = _head_norm_gate(o, gain, stack(srg_ref)).astype(F32)
    seqs = []
    for b in range(RB):
        seqs.append(jnp.concatenate(
            [out[(b * HEADS + h) * n_new:(b * HEADS + h + 1) * n_new] for h in range(HEADS)], axis=1))
    o_ref[...] = jnp.concatenate(seqs, axis=0).astype(BF16)


def _ret_sample(z_rope, z_plain, z_silu, b_in, new_state, gam, gn, tables, state_ret, layer, n_seq, n_new, row0):
    rows = RB * n_new
    blk0 = row0 // rows
    stack_rows = RB * HEADS * n_new
    dmask, qdec, kdec = tables

    def zmap(col):
        return lambda i: (blk0 + i, col)

    sblk = (None, RB) + _STATE
    in_specs = [pl.BlockSpec(memory_space=pltpu.SMEM),
                pl.BlockSpec((rows, D_MODEL), zmap(0)),
                pl.BlockSpec((rows, D_MODEL), zmap(1)),
                pl.BlockSpec((rows, D_MODEL), zmap(1)),
                pl.BlockSpec((rows, D_MODEL), zmap(1)),
                pl.BlockSpec((1, D_MODEL), lambda i: (0, 0)),
                pl.BlockSpec((stack_rows, stack_rows), lambda i: (0, 0)),
                pl.BlockSpec((stack_rows, HEAD_DIM), lambda i: (0, 0)),
                pl.BlockSpec((stack_rows, HEAD_DIM), lambda i: (0, 0)),
                pl.BlockSpec(sblk, lambda i: (layer, i, 0, 0, 0)),
                pl.BlockSpec(memory_space=pl.ANY)]
    args = [gam, z_rope, z_rope, z_plain, z_silu, gn, dmask, qdec, kdec, state_ret, b_in]
    aliases = {10: 0}
    if new_state is not None:
        in_specs.append(pl.BlockSpec(memory_space=pl.ANY))
        args.append(new_state)
        aliases[11] = 1
    return pl.pallas_call(
        functools.partial(_ret_sample_kernel, n_new=n_new),
        out_shape=(jax.ShapeDtypeStruct(b_in.shape, BF16),
                   jax.ShapeDtypeStruct(state_ret.shape, F32)),
        grid=(n_seq // RB,),
        in_specs=in_specs,
        out_specs=(pl.BlockSpec((rows, D_MODEL), zmap(0)),
                   pl.BlockSpec(sblk, lambda i: (layer, i, 0, 0, 0))),
        input_output_aliases=aliases,
        compiler_params=_cparams("arbitrary"),
        name="retention_sample",
    )(*args)


TN_MERGE = 1024


def _merge_kernel(a_ref, b_ref, gp_ref, gr_ref, wp_ref, wr_ref, o_ref, *, n_full):
    i = pl.program_id(1)

    def tile(rows):
        pool_branch = _dot(a_ref[0:rows], wp_ref[...])
        ret_branch = _dot(b_ref[0:rows], wr_ref[...])
        o_ref[0:rows] = (gp_ref[0:rows] * pool_branch + gr_ref[0:rows] * ret_branch).astype(BF16)

    @pl.when(i < n_full)
    def _():
        tile(TM)

    @pl.when(i == n_full)
    def _():
        tile(CHUNK)
        o_ref[CHUNK:] = jnp.zeros((TM - CHUNK, TN_MERGE), BF16)


def _merge(a_in, b_in, z_sig, proj_pool, proj_ret):
    mp = a_in.shape[0]
    nt = D_MODEL // TN_MERGE
    return pl.pallas_call(
        functools.partial(_merge_kernel, n_full=mp // TM - 1),
        out_shape=jax.ShapeDtypeStruct((mp, D_MODEL), BF16),
        grid=(nt, mp // TM),
        in_specs=[pl.BlockSpec((TM, D_MODEL), lambda j, i: (i, 0)),
                  pl.BlockSpec((TM, D_MODEL), lambda j, i: (i, 0)),
                  pl.BlockSpec((TM, TN_MERGE), lambda j, i: (i, j)),
                  pl.BlockSpec((TM, TN_MERGE), lambda j, i: (i, nt + j)),
                  pl.BlockSpec((D_MODEL, TN_MERGE), lambda j, i: (0, j)),
                  pl.BlockSpec((D_MODEL, TN_MERGE), lambda j, i: (0, j))],
        out_specs=pl.BlockSpec((TM, TN_MERGE), lambda j, i: (i, j)),
        compiler_params=_cparams("arbitrary", "arbitrary"),
        name="merge_branches",
    )(a_in, b_in, z_sig, z_sig, proj_pool, proj_ret)


def _out_kernel(m_ref, w_ref, g_ref, *rest, first, last, n_p, n_s):
    rest = list(rest)
    x_refs = [rest.pop(0) for _ in range(3 if first else 1)]
    i = pl.program_id(0)

    def residual(x, rows):
        return x + _dot(m_ref[0:rows], w_ref[...])

    def emit(group, xn, rows):
        y = _rms(xn, g_ref[...])
        if last:
            rest[group][...] = y
        else:
            xo_ref, ho_ref = rest
            xo_ref[0:rows] = xn
            ho_ref[0:rows] = y.astype(BF16)

    @pl.when(i < n_p)
    def _():
        emit(0, residual(x_refs[0][...], TM), TM)

    @pl.when((i >= n_p) & (i < n_p + n_s))
    def _():
        emit(1, residual(x_refs[1 if first else 0][...], TM), TM)

    if not last:
        @pl.when(i == n_p + n_s)
        def _():
            x = _meta_rows(x_refs[2]) if first else x_refs[0][0:CHUNK]
            emit(2, residual(x, CHUNK), CHUNK)
            xo_ref, ho_ref = rest
            xo_ref[CHUNK:] = jnp.zeros((TM - CHUNK, D_MODEL), F32)
            ho_ref[CHUNK:] = jnp.zeros((TM - CHUNK, D_MODEL), BF16)


def _out_projection(merged, w_out, g_next, x_parts, first, last, n_p, n_s):
    tiles = n_p + n_s + (0 if last else 1)
    row_spec = pl.BlockSpec((TM, D_MODEL), lambda i: (i, 0))
    in_specs = [row_spec,
                pl.BlockSpec((D_MODEL, D_MODEL), lambda i: (0, 0)),
                pl.BlockSpec((1, D_MODEL), lambda i: (0, 0))]
    in_specs += _x_specs(n_p, n_s) if first else [row_spec]
    if last:
        out_shape = (jax.ShapeDtypeStruct((n_p * TM, D_MODEL), F32),
                     jax.ShapeDtypeStruct((n_s * TM, D_MODEL), F32))
        out_specs = tuple(_x_specs(n_p, n_s)[:2])
    else:
        mp = merged.shape[0]
        out_shape = (jax.ShapeDtypeStruct((mp, D_MODEL), F32), jax.ShapeDtypeStruct((mp, D_MODEL), BF16))
        out_specs = (row_spec, row_spec)
    return pl.pallas_call(
        functools.partial(_out_kernel, first=first, last=last, n_p=n_p, n_s=n_s),
        out_shape=out_shape,
        grid=(tiles,),
        in_specs=in_specs,
        out_specs=out_specs,
        compiler_params=_cparams("arbitrary"),
        name="out_projection",
    )(merged, w_out, g_next, *x_parts)


def _log_decay():
    return jnp.log1p(-jnp.exp2(-5.0 - jnp.arange(HEADS, dtype=F32)))


def _rope_tables(pos):
    inv_freq = ROPE_BASE ** (-jnp.arange(HALF, dtype=F32) / HALF)
    ang = pos[:, None] * inv_freq[None, :]
    return jnp.cos(ang), jnp.sin(ang)


def _sample_decay_tables(lg, n_new):
    r = jnp.arange(RB * HEADS * n_new)
    head = (r // n_new) % HEADS
    tok = (r % n_new).astype(F32)
    lg_r = lg[head]
    same = (r[:, None] // n_new) == (r[None, :] // n_new)
    diff = tok[:, None] - tok[None, :]
    keep = same & (diff >= 0)
    dmask = jnp.where(keep, jnp.exp(jnp.where(keep, diff, 0.0) * lg_r[:, None]), 0.0)
    qdec = jnp.broadcast_to(jnp.exp((tok + 1.0) * lg_r)[:, None], (r.shape[0], HEAD_DIM))
    kdec = jnp.broadcast_to(jnp.exp((n_new - 1.0 - tok) * lg_r)[:, None], (r.shape[0], HEAD_DIM))
    return dmask, qdec, kdec


def kernel(x_prompt, x_sample, state_pool, state_ret, meta_tokens, norm_gain, w_in, pool_w, pool_scale,
           ret_gn_gain, proj_pool, proj_ret, w_out, final_norm):
    n_batch, seq = x_prompt.shape[:2]
    n_seq, n_new = x_sample.shape[:2]
    depth = norm_gain.shape[0]
    rows_prompt = n_batch * seq
    rows_sample = n_seq * n_new
    assert seq % PT == 0 and seq % CHUNK == 0 and n_seq % SB == 0 and n_seq % RB == 0
    assert n_new == 8, "sample tokens of one sequence must fill one f32 sublane tile"
    assert rows_prompt % TM == 0 and rows_sample % TM == 0
    n_p = rows_prompt // TM
    n_s = rows_sample // TM
    row_sample = rows_prompt
    row_meta = rows_prompt + rows_sample
    mp = row_meta + TM

    xp = x_prompt.reshape(rows_prompt, D_MODEL).astype(F32)
    xs = x_sample.reshape(rows_sample, D_MODEL).astype(F32)
    meta = meta_tokens.astype(F32)
    state_pool = state_pool.astype(F32)
    state_ret = state_ret.astype(F32)

    pos = jnp.concatenate([
        jnp.tile(N_META + jnp.arange(seq, dtype=F32), n_batch),
        jnp.tile(PAST_LEN + jnp.arange(n_new, dtype=F32), n_seq),
        jnp.arange(N_META, dtype=F32),
        jnp.zeros((mp - row_meta - N_META,), F32)])
    cos, sin = _rope_tables(pos)
    lg = _log_decay()
    gam_new = jnp.exp(n_new * lg)
    tables = _sample_decay_tables(lg, n_new)
    zero_state = jnp.zeros(_STATE, F32)
    row = lambda a: a.reshape(1, D_MODEL).astype(F32)

    h = _first_norm(xp, xs, meta, row(norm_gain[0]), n_p, n_s)
    x_parts = (xp, xs, meta)
    pool_p = []
    ret_p = pool_s = ret_s = None
    for l in range(depth):
        w_l = w_in[l].astype(F32)
        z_plain = _in_projection(h, w_l, SEG_U, SEG_V, "none", cos, sin)
        z_silu = _in_projection(h, w_l, SEG_PG, SEG_RG, "silu", cos, sin)
        z_rope = _in_projection(h, w_l, SEG_Q, SEG_K, "rope", cos, sin)
        z_sig = _in_projection(h, w_l, SEG_GP, SEG_GR, "sigmoid", cos, sin)
        pw = pool_w[l].astype(BF16)
        ps = row(pool_scale[l])
        gn = row(ret_gn_gain[l])

        a_in = _pool_meta(z_plain, z_silu, pw, ps, row_meta)
        a_in, hist_p = _pool_prompt(z_plain, z_silu, a_in, pw, ps, n_batch, seq, row_meta)
        a_in, pool_s = _pool_sample(z_plain, z_silu, state_pool, a_in, pool_s, pw, ps, l, n_seq, n_new,
                                    row_sample)
        pool_p.append(hist_p)

        b_in, s_meta = _ret_chunks(z_rope, z_plain, z_silu, lg, gn, zero_state, None, None, _STATE,
                                   lambda b: (), 1, 1, row_meta, N_META)
        b_in, ret_p = _ret_chunks(z_rope, z_plain, z_silu, lg, gn, s_meta, b_in, ret_p,
                                  (depth, n_batch) + _STATE, lambda b, l=l: (l, b), n_batch, seq // CHUNK,
                                  0, CHUNK)
        b_in, ret_s = _ret_sample(z_rope, z_plain, z_silu, b_in, ret_s, gam_new, gn, tables, state_ret, l,
                                  n_seq, n_new, row_sample)

        merged = _merge(a_in, b_in, z_sig, proj_pool[l].astype(BF16), proj_ret[l].astype(BF16))
        last = l == depth - 1
        g_next = row(final_norm if last else norm_gain[l + 1])
        outs = _out_projection(merged, w_out[l].astype(BF16), g_next, x_parts, l == 0, last, n_p, n_s)
        if last:
            y_prompt, y_sample = outs
        else:
            x_new, h = outs
            x_parts = (x_new,)

    return (y_prompt.reshape(n_batch, seq, D_MODEL), y_sample.reshape(n_seq, n_new, D_MODEL),
            jnp.stack(pool_p), ret_p, pool_s, ret_s)
```

```python
import functools

import jax
import jax.numpy as jnp
from jax import lax
from jax.experimental import pallas as pl
from jax.experimental.pallas import tpu as pltpu

F32 = jnp.float32
BF16 = jnp.bfloat16

D_MODEL = 2048
N_META = 16
POOL_WINDOWS = (2, 4, 8, 16)
N_GROUPS = len(POOL_WINDOWS)
POOL_GROUP = D_MODEL // N_GROUPS
POOL_HIST = max(POOL_WINDOWS) - 1
HEADS = 8
HEAD_DIM = D_MODEL // HEADS
HALF = HEAD_DIM // 2
CHUNK = 128
ROPE_BASE = 10000.0
EPS = 1e-6
PAST_LEN = 16384
SEG_U, SEG_PG, SEG_Q, SEG_K, SEG_V, SEG_RG, SEG_GP, SEG_GR = range(8)

TM = 512
TM_IN = 1024
TN_IN = 1024
VMEM_LIMIT = 56 * 1024 * 1024


def _cparams(*sem):
    return pltpu.CompilerParams(dimension_semantics=sem, vmem_limit_bytes=VMEM_LIMIT)


def _dot(a, b):
    return jnp.dot(a, b, preferred_element_type=F32)


def _dot_nt(a, b):
    return lax.dot_general(a, b, (((1,), (1,)), ((), ())), preferred_element_type=F32)


def _dot_tn(a, b):
    return lax.dot_general(a, b, (((0,), (0,)), ((), ())), preferred_element_type=F32)


def _rms(x, g):
    return x * lax.rsqrt(jnp.mean(x * x, axis=-1, keepdims=True) + EPS) * g


def _x_specs(n_p, n_s):
    return [pl.BlockSpec((TM, D_MODEL), lambda i: (jnp.minimum(i, n_p - 1), 0)),
            pl.BlockSpec((TM, D_MODEL), lambda i: (jnp.clip(i - n_p, 0, n_s - 1), 0))]


def _meta_rows(meta_ref):
    return jnp.concatenate([meta_ref[...], jnp.zeros((CHUNK - N_META, D_MODEL), F32)], axis=0)


def _norm_kernel(xp_ref, xs_ref, meta_ref, g_ref, h_ref, hm_ref, *, n_p, n_s):
    i = pl.program_id(0)

    @pl.when(i < n_p)
    def _():
        h_ref[...] = _rms(xp_ref[...], g_ref[...]).astype(BF16)

    @pl.when((i >= n_p) & (i < n_p + n_s))
    def _():
        h_ref[...] = _rms(xs_ref[...], g_ref[...]).astype(BF16)

    @pl.when(i == n_p + n_s)
    def _():
        hm_ref[...] = _rms(_meta_rows(meta_ref), g_ref[...]).astype(BF16)


def _first_norm(xp, xs, meta, g, n_p, n_s):
    n = n_p + n_s
    return pl.pallas_call(
        functools.partial(_norm_kernel, n_p=n_p, n_s=n_s),
        out_shape=(jax.ShapeDtypeStruct((n * TM, D_MODEL), BF16),
                   jax.ShapeDtypeStruct((CHUNK, D_MODEL), BF16)),
        grid=(n + 1,),
        in_specs=_x_specs(n_p, n_s) + [pl.BlockSpec((N_META, D_MODEL), lambda i: (0, 0)),
                                       pl.BlockSpec((1, D_MODEL), lambda i: (0, 0))],
        out_specs=(pl.BlockSpec((TM, D_MODEL), lambda i: (jnp.minimum(i, n - 1), 0)),
                   pl.BlockSpec((CHUNK, D_MODEL), lambda i: (0, 0))),
        compiler_params=_cparams("arbitrary"),
        name="first_norm",
    )(xp, xs, meta, g)


PLAIN_SEGS = (SEG_U, SEG_PG, SEG_V, SEG_RG, SEG_GP, SEG_GR)
COL_U, COL_PG, COL_V, COL_RG, COL_GP, COL_GR = range(len(PLAIN_SEGS))
ROPE_SEGS = (SEG_Q, SEG_K)
COL_Q, COL_K = range(len(ROPE_SEGS))


def _inproj_kernel(h_ref, hm_ref, w_ref, *rest, rope, n_main):
    if rope:
        cos_ref, sin_ref, cosm_ref, sinm_ref, o_ref, om_ref, w_scr = rest
    else:
        o_ref, om_ref, w_scr = rest
    s = pl.program_id(0)
    i = pl.program_id(1)

    @pl.when(i == 0)
    def _():
        w_scr[...] = w_ref[...].astype(BF16)

    def tile(x_ref, out_ref, trig):
        z = _dot(x_ref[...], w_scr[...])
        if rope:
            is_k = s >= D_MODEL // TN_IN
            scale = jnp.where(is_k, HEAD_DIM ** -0.5, 1.0).astype(F32)
            cos = trig[0][...]
            sin = trig[1][...]
            for hh in range(TN_IN // HEAD_DIM):
                lo = slice(hh * HEAD_DIM, hh * HEAD_DIM + HALF)
                hi = slice(hh * HEAD_DIM + HALF, (hh + 1) * HEAD_DIM)
                t1 = z[:, lo]
                t2 = z[:, hi]
                out_ref[:, lo] = (t1 * cos - t2 * sin) * scale
                out_ref[:, hi] = (t2 * cos + t1 * sin) * scale
        else:
            out_ref[...] = z

    @pl.when(i < n_main)
    def _():
        tile(h_ref, o_ref, (cos_ref, sin_ref) if rope else None)

    @pl.when(i == n_main)
    def _():
        tile(hm_ref, om_ref, (cosm_ref, sinm_ref) if rope else None)


def _in_projection(h, h_meta, w_in, layer, segs, trig):
    m = h.shape[0]
    n_main = m // TM_IN
    per_seg = D_MODEL // TN_IN
    rope = trig is not None
    gap_at = next((j for j in range(1, len(segs)) if segs[j] != segs[j - 1] + 1), len(segs))
    gap = segs[gap_at] - segs[gap_at - 1] - 1 if gap_at < len(segs) else 0
    assert all(segs[j] == segs[0] + j + (gap if j >= gap_at else 0) for j in range(len(segs)))

    def wmap(s, i):
        j = s // per_seg
        seg = segs[0] + j + jnp.where(j >= gap_at, gap, 0)
        return (layer, 0, seg * per_seg + s % per_seg)

    def main_map(s, i):
        return (jnp.minimum(i, n_main - 1), 0)

    in_specs = [pl.BlockSpec((TM_IN, D_MODEL), main_map),
                pl.BlockSpec((CHUNK, D_MODEL), lambda s, i: (0, 0)),
                pl.BlockSpec((None, D_MODEL, TN_IN), wmap)]
    args = [h, h_meta, w_in]
    if rope:
        in_specs += [pl.BlockSpec((TM_IN, HALF), main_map) for _ in range(2)]
        in_specs += [pl.BlockSpec((CHUNK, HALF), lambda s, i: (0, 0)) for _ in range(2)]
        args += list(trig)
    return pl.pallas_call(
        functools.partial(_inproj_kernel, rope=rope, n_main=n_main),
        out_shape=(jax.ShapeDtypeStruct((m, len(segs) * D_MODEL), F32),
                   jax.ShapeDtypeStruct((CHUNK, len(segs) * D_MODEL), F32)),
        grid=(len(segs) * per_seg, n_main + 1),
        in_specs=in_specs,
        out_specs=(pl.BlockSpec((TM_IN, TN_IN), lambda s, i: (jnp.minimum(i, n_main - 1), s)),
                   pl.BlockSpec((CHUNK, TN_IN), lambda s, i: (0, s))),
        scratch_shapes=[pltpu.VMEM((D_MODEL, TN_IN), BF16)],
        compiler_params=_cparams("arbitrary", "arbitrary"),
        name="in_projection_rope" if rope else "in_projection",
    )(*args)


def _window_sum(ext, w, base, rows):
    s = ext
    size = 1
    while size < w:
        s = s[size:] + s[:-size]
        size *= 2
    start = base - (w - 1)
    return s[start:start + rows]


def _silu(x):
    return x * jax.nn.sigmoid(x)


def _pool_group(g, wsum, u, inv_cnt, pw_ref, ps_ref, gate):
    sl = slice(g * POOL_GROUP, (g + 1) * POOL_GROUP)
    pooled = wsum * inv_cnt - u
    mixed = _dot(pooled.astype(BF16), pw_ref[g])
    return (mixed * ps_ref[:, sl] * _silu(gate)).astype(BF16)


_POOL_W_SPEC = ((N_GROUPS, POOL_GROUP, POOL_GROUP), (0, 0, 0))


def _pool_meta_kernel(u_ref, pg_ref, pw_ref, ps_ref, o_ref):
    avail = lax.broadcasted_iota(jnp.int32, (N_META, 1), 0).astype(F32) + 1.0
    o_ref[N_META:] = jnp.zeros((CHUNK - N_META, D_MODEL), BF16)
    for g, w in enumerate(POOL_WINDOWS):
        sl = slice(g * POOL_GROUP, (g + 1) * POOL_GROUP)
        u = u_ref[:, sl]
        wsum = _window_sum(jnp.concatenate([jnp.zeros_like(u), u], axis=0), w, N_META, N_META)
        inv_cnt = 1.0 / jnp.minimum(float(w), avail)
        o_ref[0:N_META, sl] = _pool_group(g, wsum, u, inv_cnt, pw_ref, ps_ref, pg_ref[:, sl])


def _pool_meta(zm, pool_w, pool_scale):
    return pl.pallas_call(
        _pool_meta_kernel,
        out_shape=jax.ShapeDtypeStruct((CHUNK, D_MODEL), BF16),
        grid=(1,),
        in_specs=[pl.BlockSpec((N_META, D_MODEL), lambda i: (0, COL_U)),
                  pl.BlockSpec((N_META, D_MODEL), lambda i: (0, COL_PG)),
                  pl.BlockSpec(_POOL_W_SPEC[0], lambda i: _POOL_W_SPEC[1]),
                  pl.BlockSpec((1, D_MODEL), lambda i: (0, 0))],
        out_specs=pl.BlockSpec((CHUNK, D_MODEL), lambda i: (0, 0)),
        compiler_params=_cparams("arbitrary"),
        name="pool_meta",
    )(zm, zm, pool_w, pool_scale)


PT = 256


def _pool_prompt_kernel(prev_ref, meta_ref, u_ref, pg_ref, pw_ref, ps_ref, o_ref, hist_ref):
    t = pl.program_id(1)
    for g, w in enumerate(POOL_WINDOWS):
        sl = slice(g * POOL_GROUP, (g + 1) * POOL_GROUP)
        prev = jnp.where(t == 0, meta_ref[:, sl], prev_ref[:, sl])
        u = u_ref[:, sl]
        wsum = _window_sum(jnp.concatenate([prev, u], axis=0), w, N_META, PT)
        o_ref[:, sl] = _pool_group(g, wsum, u, 1.0 / w, pw_ref, ps_ref, pg_ref[:, sl])

    @pl.when(t == pl.num_programs(1) - 1)
    def _():
        hist_ref[0] = u_ref[PT - POOL_HIST:PT, :]


def _pool_prompt(z, zm, pool_w, pool_scale, n_batch, seq):
    tiles = seq // PT
    sub = PT // N_META

    def prev_map(b, t):
        return (jnp.maximum(b * (seq // N_META) + t * sub - 1, 0), COL_U)

    return pl.pallas_call(
        _pool_prompt_kernel,
        out_shape=(jax.ShapeDtypeStruct((z.shape[0], D_MODEL), BF16),
                   jax.ShapeDtypeStruct((n_batch, POOL_HIST, D_MODEL), F32)),
        grid=(n_batch, tiles),
        in_specs=[pl.BlockSpec((N_META, D_MODEL), prev_map),
                  pl.BlockSpec((N_META, D_MODEL), lambda b, t: (0, COL_U)),
                  pl.BlockSpec((PT, D_MODEL), lambda b, t: (b * tiles + t, COL_U)),
                  pl.BlockSpec((PT, D_MODEL), lambda b, t: (b * tiles + t, COL_PG)),
                  pl.BlockSpec(_POOL_W_SPEC[0], lambda b, t: _POOL_W_SPEC[1]),
                  pl.BlockSpec((1, D_MODEL), lambda b, t: (0, 0))],
        out_specs=(pl.BlockSpec((PT, D_MODEL), lambda b, t: (b * tiles + t, 0)),
                   pl.BlockSpec((1, POOL_HIST, D_MODEL), lambda b, t: (b, 0, 0))),
        compiler_params=_cparams("arbitrary", "arbitrary"),
        name="pool_prompt",
    )(z, zm, z, z, pool_w, pool_scale)


SB = 16


def _pool_sample_kernel(hist_ref, u_ref, pg_ref, pw_ref, ps_ref, *rest, n_new):
    o_ref, nh_ref = rest[-2:]
    per = 1 + POOL_HIST + n_new
    zero_row = jnp.zeros((1, POOL_GROUP), F32)
    for g, w in enumerate(POOL_WINDOWS):
        sl = slice(g * POOL_GROUP, (g + 1) * POOL_GROUP)
        u = u_ref[:, sl]
        pieces = []
        for b in range(SB):
            pieces += [zero_row, hist_ref[b, :, sl], u[b * n_new:(b + 1) * n_new]]
        ext = jnp.concatenate(pieces, axis=0)
        for b in range(SB):
            nh_ref[b, :, sl] = ext[(b + 1) * per - POOL_HIST:(b + 1) * per]
        s = _window_sum(ext, w, 1 + POOL_HIST, SB * per - 1 - POOL_HIST)
        wsum = jnp.concatenate([s[b * per:b * per + n_new] for b in range(SB)], axis=0)
        o_ref[:, sl] = _pool_group(g, wsum, u, 1.0 / w, pw_ref, ps_ref, pg_ref[:, sl])


def _pool_sample(z, state_pool, a_in, new_hist, pool_w, pool_scale, layer, n_seq, n_new, row0):
    rows = SB * n_new
    blk0 = row0 // rows
    hist_blk = (None, SB, POOL_HIST, D_MODEL)
    in_specs = [pl.BlockSpec(hist_blk, lambda i: (layer, i, 0, 0)),
                pl.BlockSpec((rows, D_MODEL), lambda i: (blk0 + i, COL_U)),
                pl.BlockSpec((rows, D_MODEL), lambda i: (blk0 + i, COL_PG)),
                pl.BlockSpec(_POOL_W_SPEC[0], lambda i: _POOL_W_SPEC[1]),
                pl.BlockSpec((1, D_MODEL), lambda i: (0, 0)),
                pl.BlockSpec(memory_space=pl.ANY)]
    args = [state_pool, z, z, pool_w, pool_scale, a_in]
    aliases = {5: 0}
    if new_hist is not None:
        in_specs.append(pl.BlockSpec(memory_space=pl.ANY))
        args.append(new_hist)
        aliases[6] = 1
    return pl.pallas_call(
        functools.partial(_pool_sample_kernel, n_new=n_new),
        out_shape=(jax.ShapeDtypeStruct(a_in.shape, BF16),
                   jax.ShapeDtypeStruct(state_pool.shape, F32)),
        grid=(n_seq // SB,),
        in_specs=in_specs,
        out_specs=(pl.BlockSpec((rows, D_MODEL), lambda i: (blk0 + i, 0)),
                   pl.BlockSpec(hist_blk, lambda i: (layer, i, 0, 0))),
        input_output_aliases=aliases,
        compiler_params=_cparams("arbitrary"),
        name="pool_sample",
    )(*args)


def _head_norm_gate(o, gain, gate):
    mu = jnp.mean(o, axis=-1, keepdims=True)
    oc = o - mu
    on = oc * lax.rsqrt(jnp.mean(oc * oc, axis=-1, keepdims=True) + EPS)
    return ((on * gain) * _silu(gate)).astype(BF16)


_STATE = (HEADS, HEAD_DIM, HEAD_DIM)


def _ret_chunk_kernel(dec_ref, q_ref, k_ref, v_ref, rg_ref, gn_ref, s0_ref, *rest, n_valid):
    o_ref, sfin_ref, s_scr = rest[-3:]
    c = pl.program_id(1)

    @pl.when(c == 0)
    def _():
        s_scr[...] = s0_ref[...]

    n = CHUNK
    row = lax.broadcasted_iota(jnp.int32, (n, n), 0)
    col = lax.broadcasted_iota(jnp.int32, (n, n), 1)
    causal = row >= col
    diff = jnp.where(causal, row - col, 0).astype(F32)
    ridx = lax.broadcasted_iota(jnp.int32, (n, 1), 0).astype(F32)
    for h in range(HEADS):
        hs = slice(h * HEAD_DIM, (h + 1) * HEAD_DIM)
        lg = dec_ref[0, h]
        q = q_ref[:, hs].astype(BF16)
        k = k_ref[:, hs]
        v = v_ref[:, hs].astype(BF16)
        s_old = s_scr[h]
        decay = jnp.where(causal, jnp.exp(diff * lg), 0.0)
        scores = _dot_nt(q, k.astype(BF16)) * decay
        intra = _dot(scores.astype(BF16), v)
        inter = _dot(q, s_old.astype(BF16)) * jnp.exp((ridx + 1.0) * lg)
        k_dec = (k * jnp.exp((n_valid - 1.0 - ridx) * lg)).astype(BF16)
        s_scr[h] = s_old * dec_ref[1, h] + _dot_tn(k_dec, v)
        o_ref[:, hs] = _head_norm_gate(intra + inter, gn_ref[:, hs], rg_ref[:, hs])

    @pl.when(c == pl.num_programs(1) - 1)
    def _():
        sfin_ref[...] = s_scr[...]


def _ret_chunks(zr, z, lg, gn, s0, fin, fin_shape, fin_index, n_batch, n_chunks, n_valid):
    dec = jnp.stack([lg, jnp.exp(n_valid * lg)])

    def zmap(col):
        return lambda b, c: (b * n_chunks + c, col)

    in_specs = [pl.BlockSpec(memory_space=pltpu.SMEM),
                pl.BlockSpec((CHUNK, D_MODEL), zmap(COL_Q)),
                pl.BlockSpec((CHUNK, D_MODEL), zmap(COL_K)),
                pl.BlockSpec((CHUNK, D_MODEL), zmap(COL_V)),
                pl.BlockSpec((CHUNK, D_MODEL), zmap(COL_RG)),
                pl.BlockSpec((1, D_MODEL), lambda b, c: (0, 0)),
                pl.BlockSpec(_STATE, lambda b, c: (0, 0, 0))]
    args = [dec, zr, zr, z, z, gn, s0]
    aliases = {}
    if fin is not None:
        aliases[len(args)] = 1
        in_specs.append(pl.BlockSpec(memory_space=pl.ANY))
        args.append(fin)
    fin_blk = (None,) * (len(fin_shape) - 3) + _STATE
    return pl.pallas_call(
        functools.partial(_ret_chunk_kernel, n_valid=n_valid),
        out_shape=(jax.ShapeDtypeStruct((zr.shape[0], D_MODEL), BF16),
                   jax.ShapeDtypeStruct(fin_shape, F32)),
        grid=(n_batch, n_chunks),
        in_specs=in_specs,
        out_specs=(pl.BlockSpec((CHUNK, D_MODEL), zmap(0)),
                   pl.BlockSpec(fin_blk, lambda b, c: fin_index(b) + (0, 0, 0))),
        scratch_shapes=[pltpu.VMEM(_STATE, F32)],
        input_output_aliases=aliases,
        compiler_params=_cparams("arbitrary", "arbitrary"),
        name="retention_chunks",
    )(*args)


RB = 2


def _ret_sample_kernel(gam_ref, q_ref, k_ref, v_ref, rg_ref, gn_ref, dmask_ref, qdec_ref, kdec_ref,
                       s_ref, *rest, n_new):
    o_ref, sout_ref = rest[-2:]
    pairs = [(b, h) for b in range(RB) for h in range(HEADS)]

    def stack(ref):
        return jnp.concatenate(
            [ref[b * n_new:(b + 1) * n_new, h * HEAD_DIM:(h + 1) * HEAD_DIM] for b, h in pairs], axis=0)

    k = stack(k_ref)
    qb = stack(q_ref).astype(BF16)
    v = stack(v_ref)
    vb = v.astype(BF16)
    scores = _dot_nt(qb, k.astype(BF16)) * dmask_ref[...]
    intra = _dot(scores.astype(BF16), vb)
    k_dec = (k * kdec_ref[...]).astype(BF16)
    rows = lax.broadcasted_iota(jnp.int32, v.shape, 0)
    is_even = (rows & n_new) == 0
    v_even = jnp.where(is_even, v, 0.0).astype(BF16)
    v_odd = jnp.where(is_even, 0.0, v).astype(BF16)
    grp = 2 * n_new
    inter_parts = []
    for idx, (b, h) in enumerate(pairs):
        g0 = (idx // 2) * grp
        off = (idx % 2) * n_new
        s_old = s_ref[b, h]
        inter = _dot(qb[g0:g0 + grp], s_old.astype(BF16))
        inter_parts.append(inter[off:off + n_new])
        v_sel = v_even if idx % 2 == 0 else v_odd
        sout_ref[b, h] = s_old * gam_ref[h] + _dot_tn(k_dec[g0:g0 + grp], v_sel[g0:g0 + grp])
    o = intra + jnp.concatenate(inter_parts, axis=0) * qdec_ref[...]
    gain = jnp.concatenate(
        [jnp.broadcast_to(gn_ref[:, h * HEAD_DIM:(h + 1) * HEAD_DIM], (n_new, HEAD_DIM)) for _, h in pairs],
        axis=0)
    out = _head_norm_gate(o, gain, stack(rg_ref)).astype(F32)
    seqs = []
    for b in range(RB):
        seqs.append(jnp.concatenate(
            [out[(b * HEADS + h) * n_new:(b * HEADS + h + 1) * n_new] for h in range(HEADS)], axis=1))
    o_ref[...] = jnp.concatenate(seqs, axis=0).astype(BF16)


def _ret_sample(zr, z, b_in, new_state, gam, gn, tables, state_ret, layer, n_seq, n_new, row0):
    rows = RB * n_new
    blk0 = row0 // rows
    stack_rows = RB * HEADS * n_new
    dmask, qdec, kdec = tables

    def zmap(col):
        return lambda i: (blk0 + i, col)

    sblk = (None, RB) + _STATE
    in_specs = [pl.BlockSpec(memory_space=pltpu.SMEM),
                pl.BlockSpec((rows, D_MODEL), zmap(COL_Q)),
                pl.BlockSpec((rows, D_MODEL), zmap(COL_K)),
                pl.BlockSpec((rows, D_MODEL), zmap(COL_V)),
                pl.BlockSpec((rows, D_MODEL), zmap(COL_RG)),
                pl.BlockSpec((1, D_MODEL), lambda i: (0, 0)),
                pl.BlockSpec((stack_rows, stack_rows), lambda i: (0, 0)),
                pl.BlockSpec((stack_rows, HEAD_DIM), lambda i: (0, 0)),
                pl.BlockSpec((stack_rows, HEAD_DIM), lambda i: (0, 0)),
                pl.BlockSpec(sblk, lambda i: (layer, i, 0, 0, 0)),
                pl.BlockSpec(memory_space=pl.ANY)]
    args = [gam, zr, zr, z, z, gn, dmask, qdec, kdec, state_ret, b_in]
    aliases = {10: 0}
    if new_state is not None:
        in_specs.append(pl.BlockSpec(memory_space=pl.ANY))
        args.append(new_state)
        aliases[11] = 1
    return pl.pallas_call(
        functools.partial(_ret_sample_kernel, n_new=n_new),
        out_shape=(jax.ShapeDtypeStruct(b_in.shape, BF16),
                   jax.ShapeDtypeStruct(state_ret.shape, F32)),
        grid=(n_seq // RB,),
        in_specs=in_specs,
        out_specs=(pl.BlockSpec((rows, D_MODEL), zmap(0)),
                   pl.BlockSpec(sblk, lambda i: (layer, i, 0, 0, 0))),
        input_output_aliases=aliases,
        compiler_params=_cparams("arbitrary"),
        name="retention_sample",
    )(*args)


TN_MERGE = 1024


def _merge_kernel(a_ref, b_ref, gp_ref, gr_ref, am_ref, bm_ref, gpm_ref, grm_ref, wp_ref, wr_ref,
                  o_ref, om_ref, *, n_main):
    i = pl.program_id(1)

    def tile(a, b, gp, gr, out):
        pool_branch = _dot(a[...], wp_ref[...])
        ret_branch = _dot(b[...], wr_ref[...])
        out[...] = (jax.nn.sigmoid(gp[...]) * pool_branch + jax.nn.sigmoid(gr[...]) * ret_branch).astype(BF16)

    @pl.when(i < n_main)
    def _():
        tile(a_ref, b_ref, gp_ref, gr_ref, o_ref)

    @pl.when(i == n_main)
    def _():
        tile(am_ref, bm_ref, gpm_ref, grm_ref, om_ref)


def _merge(a_in, b_in, z, a_meta, b_meta, zm, proj_pool, proj_ret):
    m = a_in.shape[0]
    n_main = m // TM
    nt = D_MODEL // TN_MERGE

    def main_map(j, i):
        return (jnp.minimum(i, n_main - 1), 0)

    def gate_map(off):
        return lambda j, i: (jnp.minimum(i, n_main - 1), off + j)

    const = lambda j, i: (0, 0)
    return pl.pallas_call(
        functools.partial(_merge_kernel, n_main=n_main),
        out_shape=(jax.ShapeDtypeStruct((m, D_MODEL), BF16),
                   jax.ShapeDtypeStruct((CHUNK, D_MODEL), BF16)),
        grid=(nt, n_main + 1),
        in_specs=[pl.BlockSpec((TM, D_MODEL), main_map),
                  pl.BlockSpec((TM, D_MODEL), main_map),
                  pl.BlockSpec((TM, TN_MERGE), gate_map(COL_GP * nt)),
                  pl.BlockSpec((TM, TN_MERGE), gate_map(COL_GR * nt)),
                  pl.BlockSpec((CHUNK, D_MODEL), const),
                  pl.BlockSpec((CHUNK, D_MODEL), const),
                  pl.BlockSpec((CHUNK, TN_MERGE), lambda j, i: (0, COL_GP * nt + j)),
                  pl.BlockSpec((CHUNK, TN_MERGE), lambda j, i: (0, COL_GR * nt + j)),
                  pl.BlockSpec((D_MODEL, TN_MERGE), lambda j, i: (0, j)),
                  pl.BlockSpec((D_MODEL, TN_MERGE), lambda j, i: (0, j))],
        out_specs=(pl.BlockSpec((TM, TN_MERGE), gate_map(0)),
                   pl.BlockSpec((CHUNK, TN_MERGE), lambda j, i: (0, j))),
        compiler_params=_cparams("arbitrary", "arbitrary"),
        name="merge_branches",
    )(a_in, b_in, z, z, a_meta, b_meta, zm, zm, proj_pool, proj_ret)


def _out_kernel(m_ref, mm_ref, w_ref, g_ref, *rest, first, last, n_p, n_s):
    rest = list(rest)
    x_refs = [rest.pop(0) for _ in range(3 if first else 2)]
    i = pl.program_id(0)

    def emit(x, merged, outs):
        xn = x + _dot(merged[...], w_ref[...])
        y = _rms(xn, g_ref[...])
        if last:
            outs[0][...] = y
        else:
            outs[0][...] = xn
            outs[1][...] = y.astype(BF16)

    main_outs = [rest[0:1], rest[1:2]] if last else [rest[0:2], rest[0:2]]

    @pl.when(i < n_p)
    def _():
        emit(x_refs[0][...], m_ref, main_outs[0])

    @pl.when((i >= n_p) & (i < n_p + n_s))
    def _():
        emit(x_refs[1 if first else 0][...], m_ref, main_outs[1])

    if not last:
        @pl.when(i == n_p + n_s)
        def _():
            emit(_meta_rows(x_refs[2]) if first else x_refs[1][...], mm_ref, rest[2:4])


def _out_projection(merged, merged_meta, w_out, g_next, x_parts, first, last, n_p, n_s):
    n = n_p + n_s
    main_spec = pl.BlockSpec((TM, D_MODEL), lambda i: (jnp.minimum(i, n - 1), 0))
    meta_spec = pl.BlockSpec((CHUNK, D_MODEL), lambda i: (0, 0))
    in_specs = [main_spec, meta_spec,
                pl.BlockSpec((D_MODEL, D_MODEL), lambda i: (0, 0)),
                pl.BlockSpec((1, D_MODEL), lambda i: (0, 0))]
    if first:
        in_specs += _x_specs(n_p, n_s) + [pl.BlockSpec((N_META, D_MODEL), lambda i: (0, 0))]
    else:
        in_specs += [main_spec, meta_spec]
    if last:
        out_shape = (jax.ShapeDtypeStruct((n_p * TM, D_MODEL), F32),
                     jax.ShapeDtypeStruct((n_s * TM, D_MODEL), F32))
        out_specs = tuple(_x_specs(n_p, n_s))
    else:
        out_shape = (jax.ShapeDtypeStruct((n * TM, D_MODEL), F32), jax.ShapeDtypeStruct((n * TM, D_MODEL), BF16),
                     jax.ShapeDtypeStruct((CHUNK, D_MODEL), F32), jax.ShapeDtypeStruct((CHUNK, D_MODEL), BF16))
        out_specs = (main_spec, main_spec, meta_spec, meta_spec)
    return pl.pallas_call(
        functools.partial(_out_kernel, first=first, last=last, n_p=n_p, n_s=n_s),
        out_shape=out_shape,
        grid=(n if last else n + 1,),
        in_specs=in_specs,
        out_specs=out_specs,
        compiler_params=_cparams("arbitrary"),
        name="out_projection",
    )(merged, merged_meta, w_out, g_next, *x_parts)


def _log_decay():
    return jnp.log1p(-jnp.exp2(-5.0 - jnp.arange(HEADS, dtype=F32)))


def _rope_tables(pos):
    inv_freq = ROPE_BASE ** (-jnp.arange(HALF, dtype=F32) / HALF)
    ang = pos[:, None] * inv_freq[None, :]
    return jnp.cos(ang), jnp.sin(ang)


def _sample_decay_tables(lg, n_new):
    r = jnp.arange(RB * HEADS * n_new)
    head = (r // n_new) % HEADS
    tok = (r % n_new).astype(F32)
    lg_r = lg[head]
    same = (r[:, None] // n_new) == (r[None, :] // n_new)
    diff = tok[:, None] - tok[None, :]
    keep = same & (diff >= 0)
    dmask = jnp.where(keep, jnp.exp(jnp.where(keep, diff, 0.0) * lg_r[:, None]), 0.0)
    qdec = jnp.broadcast_to(jnp.exp((tok + 1.0) * lg_r)[:, None], (r.shape[0], HEAD_DIM))
    kdec = jnp.broadcast_to(jnp.exp((n_new - 1.0 - tok) * lg_r)[:, None], (r.shape[0], HEAD_DIM))
    return dmask, qdec, kdec


def kernel(x_prompt, x_sample, state_pool, state_ret, meta_tokens, norm_gain, w_in, pool_w, pool_scale,
           ret_gn_gain, proj_pool, proj_ret, w_out, final_norm):
    n_batch, seq = x_prompt.shape[:2]
    n_seq, n_new = x_sample.shape[:2]
    depth = norm_gain.shape[0]
    rows_prompt = n_batch * seq
    rows_sample = n_seq * n_new
    assert seq % PT == 0 and seq % CHUNK == 0 and n_seq % SB == 0 and n_seq % RB == 0
    assert n_new == 8, "sample tokens of one sequence must fill one f32 sublane tile"
    assert rows_prompt % TM == 0 and rows_sample % TM == 0 and (rows_prompt + rows_sample) % TM_IN == 0
    n_p = rows_prompt // TM
    n_s = rows_sample // TM

    xp = x_prompt.reshape(rows_prompt, D_MODEL).astype(F32)
    xs = x_sample.reshape(rows_sample, D_MODEL).astype(F32)
    meta = meta_tokens.astype(F32)
    state_pool = state_pool.astype(F32)
    state_ret = state_ret.astype(F32)
    w_in = w_in.astype(F32)

    pos = jnp.concatenate([jnp.tile(N_META + jnp.arange(seq, dtype=F32), n_batch),
                           jnp.tile(PAST_LEN + jnp.arange(n_new, dtype=F32), n_seq)])
    pos_meta = jnp.where(jnp.arange(CHUNK) < N_META, jnp.arange(CHUNK, dtype=F32), 0.0)
    trig = _rope_tables(pos) + _rope_tables(pos_meta)
    lg = _log_decay()
    gam_new = jnp.exp(n_new * lg)
    tables = _sample_decay_tables(lg, n_new)
    zero_state = jnp.zeros(_STATE, F32)
    row = lambda a: a.reshape(1, D_MODEL).astype(F32)

    h, h_meta = _first_norm(xp, xs, meta, row(norm_gain[0]), n_p, n_s)
    x_parts = (xp, xs, meta)
    pool_p = []
    ret_p = pool_s = ret_s = None
    for l in range(depth):
        zr, zrm = _in_projection(h, h_meta, w_in, l, ROPE_SEGS, trig)
        z, zm = _in_projection(h, h_meta, w_in, l, PLAIN_SEGS, None)
        pw = pool_w[l].astype(BF16)
        ps = row(pool_scale[l])
        gn = row(ret_gn_gain[l])

        a_meta = _pool_meta(zm, pw, ps)
        a_in, hist_p = _pool_prompt(z, zm, pw, ps, n_batch, seq)
        a_in, pool_s = _pool_sample(z, state_pool, a_in, pool_s, pw, ps, l, n_seq, n_new, rows_prompt)
        pool_p.append(hist_p)

        b_meta, s_meta = _ret_chunks(zrm, zm, lg, gn, zero_state, None, _STATE, lambda b: (), 1, 1, N_META)
        b_in, ret_p = _ret_chunks(zr, z, lg, gn, s_meta, ret_p, (depth, n_batch) + _STATE,
                                  lambda b, l=l: (l, b), n_batch, seq // CHUNK, CHUNK)
        b_in, ret_s = _ret_sample(zr, z, b_in, ret_s, gam_new, gn, tables, state_ret, l, n_seq, n_new,
                                  rows_prompt)

        merged, merged_meta = _merge(a_in, b_in, z, a_meta, b_meta, zm,
                                     proj_pool[l].astype(BF16), proj_ret[l].astype(BF16))
        last = l == depth - 1
        g_next = row(final_norm if last else norm_gain[l + 1])
        outs = _out_projection(merged, merged_meta, w_out[l].astype(BF16), g_next, x_parts, l == 0, last,
                               n_p, n_s)
        if last:
            y_prompt, y_sample = outs
        else:
            x_new, h, x_meta, h_meta = outs
            x_parts = (x_new, x_meta)

    return (y_prompt.reshape(n_batch, seq, D_MODEL), y_sample.reshape(n_seq, n_new, D_MODEL),
            jnp.stack(pool_p), ret_p, pool_s, ret_s)
```

```python
import functools

import jax
import jax.numpy as jnp
from jax import lax
from jax.experimental import pallas as pl
from jax.experimental.pallas import tpu as pltpu

F32 = jnp.float32
BF16 = jnp.bfloat16

D_MODEL = 2048
N_META = 16
POOL_WINDOWS = (2, 4, 8, 16)
N_GROUPS = len(POOL_WINDOWS)
POOL_GROUP = D_MODEL // N_GROUPS
POOL_HIST = max(POOL_WINDOWS) - 1
HEADS = 8
HEAD_DIM = D_MODEL // HEADS
HALF = HEAD_DIM // 2
CHUNK = 128
ROPE_BASE = 10000.0
EPS = 1e-6
PAST_LEN = 16384
SEG_U, SEG_PG, SEG_Q, SEG_K, SEG_V, SEG_RG, SEG_GP, SEG_GR = range(8)

TM = 512
TM_IN = 1536
TN_IN = 1024
VMEM_LIMIT = 56 * 1024 * 1024
VMEM_LIMIT_IN = 60 * 1024 * 1024


def _cparams(*sem, vmem=VMEM_LIMIT):
    return pltpu.CompilerParams(dimension_semantics=sem, vmem_limit_bytes=vmem)


def _dot(a, b):
    return jnp.dot(a, b, preferred_element_type=F32)


def _dot_nt(a, b):
    return lax.dot_general(a, b, (((1,), (1,)), ((), ())), preferred_element_type=F32)


def _dot_tn(a, b):
    return lax.dot_general(a, b, (((0,), (0,)), ((), ())), preferred_element_type=F32)


def _rms(x, g):
    return x * lax.rsqrt(jnp.mean(x * x, axis=-1, keepdims=True) + EPS) * g


def _x_specs(n_p, n_s):
    return [pl.BlockSpec((TM, D_MODEL), lambda i: (jnp.minimum(i, n_p - 1), 0)),
            pl.BlockSpec((TM, D_MODEL), lambda i: (jnp.clip(i - n_p, 0, n_s - 1), 0))]


def _meta_rows(meta_ref):
    return jnp.concatenate([meta_ref[...], jnp.zeros((CHUNK - N_META, D_MODEL), F32)], axis=0)


def _norm_kernel(xp_ref, xs_ref, meta_ref, g_ref, h_ref, hm_ref, *, n_p, n_s):
    i = pl.program_id(0)

    @pl.when(i < n_p)
    def _():
        h_ref[...] = _rms(xp_ref[...], g_ref[...]).astype(BF16)

    @pl.when((i >= n_p) & (i < n_p + n_s))
    def _():
        h_ref[...] = _rms(xs_ref[...], g_ref[...]).astype(BF16)

    @pl.when(i == n_p + n_s)
    def _():
        hm_ref[...] = _rms(_meta_rows(meta_ref), g_ref[...]).astype(BF16)


def _first_norm(xp, xs, meta, g, n_p, n_s):
    n = n_p + n_s
    return pl.pallas_call(
        functools.partial(_norm_kernel, n_p=n_p, n_s=n_s),
        out_shape=(jax.ShapeDtypeStruct((n * TM, D_MODEL), BF16),
                   jax.ShapeDtypeStruct((CHUNK, D_MODEL), BF16)),
        grid=(n + 1,),
        in_specs=_x_specs(n_p, n_s) + [pl.BlockSpec((N_META, D_MODEL), lambda i: (0, 0)),
                                       pl.BlockSpec((1, D_MODEL), lambda i: (0, 0))],
        out_specs=(pl.BlockSpec((TM, D_MODEL), lambda i: (jnp.minimum(i, n - 1), 0)),
                   pl.BlockSpec((CHUNK, D_MODEL), lambda i: (0, 0))),
        compiler_params=_cparams("arbitrary"),
        name="first_norm",
    )(xp, xs, meta, g)


PLAIN_SEGS = (SEG_U, SEG_PG, SEG_V, SEG_RG, SEG_GP, SEG_GR)
COL_U, COL_PG, COL_V, COL_RG, COL_GP, COL_GR = range(len(PLAIN_SEGS))
ROPE_SEGS = (SEG_Q, SEG_K)
COL_Q, COL_K = range(len(ROPE_SEGS))


def _inproj_kernel(h_ref, hm_ref, w_ref, *rest, rope):
    if rope:
        cos_ref, sin_ref, cosm_ref, sinm_ref, o_ref, om_ref, w_scr = rest
    else:
        o_ref, om_ref, w_scr = rest
    s = pl.program_id(0)
    i = pl.program_id(1)

    @pl.when(i == 0)
    def _():
        w_scr[...] = w_ref[...].astype(BF16)

    def tile(x_ref, out_ref, trig):
        z = _dot(x_ref[...], w_scr[...])
        if rope:
            is_k = s >= D_MODEL // TN_IN
            scale = jnp.where(is_k, HEAD_DIM ** -0.5, 1.0).astype(F32)
            cos = trig[0][...]
            sin = trig[1][...]
            for hh in range(TN_IN // HEAD_DIM):
                lo = slice(hh * HEAD_DIM, hh * HEAD_DIM + HALF)
                hi = slice(hh * HEAD_DIM + HALF, (hh + 1) * HEAD_DIM)
                t1 = z[:, lo]
                t2 = z[:, hi]
                out_ref[:, lo] = (t1 * cos - t2 * sin) * scale
                out_ref[:, hi] = (t2 * cos + t1 * sin) * scale
        else:
            out_ref[...] = z

    @pl.when(i == 0)
    def _():
        tile(hm_ref, om_ref, (cosm_ref, sinm_ref) if rope else None)

    @pl.when(i > 0)
    def _():
        tile(h_ref, o_ref, (cos_ref, sin_ref) if rope else None)


def _in_projection(h, h_meta, w_in, layer, segs, trig):
    m = h.shape[0]
    n_main = m // TM_IN
    per_seg = D_MODEL // TN_IN
    rope = trig is not None
    gap_at = next((j for j in range(1, len(segs)) if segs[j] != segs[j - 1] + 1), len(segs))
    gap = segs[gap_at] - segs[gap_at - 1] - 1 if gap_at < len(segs) else 0
    assert all(segs[j] == segs[0] + j + (gap if j >= gap_at else 0) for j in range(len(segs)))

    def wmap(s, i):
        j = s // per_seg
        seg = segs[0] + j + jnp.where(j >= gap_at, gap, 0)
        return (layer, 0, seg * per_seg + s % per_seg)

    def main_map(s, i):
        return (jnp.maximum(i - 1, 0), 0)

    in_specs = [pl.BlockSpec((TM_IN, D_MODEL), main_map),
                pl.BlockSpec((CHUNK, D_MODEL), lambda s, i: (0, 0)),
                pl.BlockSpec((None, D_MODEL, TN_IN), wmap)]
    args = [h, h_meta, w_in]
    if rope:
        in_specs += [pl.BlockSpec((TM_IN, HALF), main_map) for _ in range(2)]
        in_specs += [pl.BlockSpec((CHUNK, HALF), lambda s, i: (0, 0)) for _ in range(2)]
        args += list(trig)
    return pl.pallas_call(
        functools.partial(_inproj_kernel, rope=rope),
        out_shape=(jax.ShapeDtypeStruct((m, len(segs) * D_MODEL), F32),
                   jax.ShapeDtypeStruct((CHUNK, len(segs) * D_MODEL), F32)),
        grid=(len(segs) * per_seg, n_main + 1),
        in_specs=in_specs,
        out_specs=(pl.BlockSpec((TM_IN, TN_IN), lambda s, i: (jnp.maximum(i - 1, 0), s)),
                   pl.BlockSpec((CHUNK, TN_IN), lambda s, i: (0, s))),
        scratch_shapes=[pltpu.VMEM((D_MODEL, TN_IN), BF16)],
        compiler_params=_cparams("arbitrary", "arbitrary", vmem=VMEM_LIMIT_IN),
        name="in_projection_rope" if rope else "in_projection",
    )(*args)


def _window_sum(ext, w, base, rows):
    s = ext
    size = 1
    while size < w:
        s = s[size:] + s[:-size]
        size *= 2
    start = base - (w - 1)
    return s[start:start + rows]


def _silu(x):
    return x * jax.nn.sigmoid(x)


def _pool_group(g, wsum, u, inv_cnt, pw_ref, ps_ref, gate):
    sl = slice(g * POOL_GROUP, (g + 1) * POOL_GROUP)
    pooled = wsum * inv_cnt - u
    mixed = _dot(pooled.astype(BF16), pw_ref[g])
    return (mixed * ps_ref[:, sl] * _silu(gate)).astype(BF16)


_POOL_W_SPEC = ((N_GROUPS, POOL_GROUP, POOL_GROUP), (0, 0, 0))


def _pool_meta_kernel(u_ref, pg_ref, pw_ref, ps_ref, o_ref):
    avail = lax.broadcasted_iota(jnp.int32, (N_META, 1), 0).astype(F32) + 1.0
    o_ref[N_META:] = jnp.zeros((CHUNK - N_META, D_MODEL), BF16)
    for g, w in enumerate(POOL_WINDOWS):
        sl = slice(g * POOL_GROUP, (g + 1) * POOL_GROUP)
        u = u_ref[:, sl]
        wsum = _window_sum(jnp.concatenate([jnp.zeros_like(u), u], axis=0), w, N_META, N_META)
        inv_cnt = 1.0 / jnp.minimum(float(w), avail)
        o_ref[0:N_META, sl] = _pool_group(g, wsum, u, inv_cnt, pw_ref, ps_ref, pg_ref[:, sl])


def _pool_meta(zm, pool_w, pool_scale):
    return pl.pallas_call(
        _pool_meta_kernel,
        out_shape=jax.ShapeDtypeStruct((CHUNK, D_MODEL), BF16),
        grid=(1,),
        in_specs=[pl.BlockSpec((N_META, D_MODEL), lambda i: (0, COL_U)),
                  pl.BlockSpec((N_META, D_MODEL), lambda i: (0, COL_PG)),
                  pl.BlockSpec(_POOL_W_SPEC[0], lambda i: _POOL_W_SPEC[1]),
                  pl.BlockSpec((1, D_MODEL), lambda i: (0, 0))],
        out_specs=pl.BlockSpec((CHUNK, D_MODEL), lambda i: (0, 0)),
        compiler_params=_cparams("arbitrary"),
        name="pool_meta",
    )(zm, zm, pool_w, pool_scale)


PT = 256


def _pool_prompt_kernel(prev_ref, meta_ref, u_ref, pg_ref, pw_ref, ps_ref, o_ref, hist_ref):
    t = pl.program_id(1)
    for g, w in enumerate(POOL_WINDOWS):
        sl = slice(g * POOL_GROUP, (g + 1) * POOL_GROUP)
        prev = jnp.where(t == 0, meta_ref[:, sl], prev_ref[:, sl])
        u = u_ref[:, sl]
        wsum = _window_sum(jnp.concatenate([prev, u], axis=0), w, N_META, PT)
        o_ref[:, sl] = _pool_group(g, wsum, u, 1.0 / w, pw_ref, ps_ref, pg_ref[:, sl])

    @pl.when(t == pl.num_programs(1) - 1)
    def _():
        hist_ref[0] = u_ref[PT - POOL_HIST:PT, :]


def _pool_prompt(z, zm, pool_w, pool_scale, n_batch, seq):
    tiles = seq // PT
    sub = PT // N_META

    def prev_map(b, t):
        return (jnp.maximum(b * (seq // N_META) + t * sub - 1, 0), COL_U)

    return pl.pallas_call(
        _pool_prompt_kernel,
        out_shape=(jax.ShapeDtypeStruct((z.shape[0], D_MODEL), BF16),
                   jax.ShapeDtypeStruct((n_batch, POOL_HIST, D_MODEL), F32)),
        grid=(n_batch, tiles),
        in_specs=[pl.BlockSpec((N_META, D_MODEL), prev_map),
                  pl.BlockSpec((N_META, D_MODEL), lambda b, t: (0, COL_U)),
                  pl.BlockSpec((PT, D_MODEL), lambda b, t: (b * tiles + t, COL_U)),
                  pl.BlockSpec((PT, D_MODEL), lambda b, t: (b * tiles + t, COL_PG)),
                  pl.BlockSpec(_POOL_W_SPEC[0], lambda b, t: _POOL_W_SPEC[1]),
                  pl.BlockSpec((1, D_MODEL), lambda b, t: (0, 0))],
        out_specs=(pl.BlockSpec((PT, D_MODEL), lambda b, t: (b * tiles + t, 0)),
                   pl.BlockSpec((1, POOL_HIST, D_MODEL), lambda b, t: (b, 0, 0))),
        compiler_params=_cparams("arbitrary", "arbitrary"),
        name="pool_prompt",
    )(z, zm, z, z, pool_w, pool_scale)


SB = 16


def _pool_sample_kernel(hist_ref, u_ref, pg_ref, pw_ref, ps_ref, *rest, n_new):
    o_ref, nh_ref = rest[-2:]
    per = 1 + POOL_HIST + n_new
    zero_row = jnp.zeros((1, POOL_GROUP), F32)
    for g, w in enumerate(POOL_WINDOWS):
        sl = slice(g * POOL_GROUP, (g + 1) * POOL_GROUP)
        u = u_ref[:, sl]
        pieces = []
        for b in range(SB):
            pieces += [zero_row, hist_ref[b, :, sl], u[b * n_new:(b + 1) * n_new]]
        ext = jnp.concatenate(pieces, axis=0)
        for b in range(SB):
            nh_ref[b, :, sl] = ext[(b + 1) * per - POOL_HIST:(b + 1) * per]
        s = _window_sum(ext, w, 1 + POOL_HIST, SB * per - 1 - POOL_HIST)
        wsum = jnp.concatenate([s[b * per:b * per + n_new] for b in range(SB)], axis=0)
        o_ref[:, sl] = _pool_group(g, wsum, u, 1.0 / w, pw_ref, ps_ref, pg_ref[:, sl])


def _pool_sample(z, state_pool, a_in, new_hist, pool_w, pool_scale, layer, n_seq, n_new, row0):
    rows = SB * n_new
    blk0 = row0 // rows
    hist_blk = (None, SB, POOL_HIST, D_MODEL)
    in_specs = [pl.BlockSpec(hist_blk, lambda i: (layer, i, 0, 0)),
                pl.BlockSpec((rows, D_MODEL), lambda i: (blk0 + i, COL_U)),
                pl.BlockSpec((rows, D_MODEL), lambda i: (blk0 + i, COL_PG)),
                pl.BlockSpec(_POOL_W_SPEC[0], lambda i: _POOL_W_SPEC[1]),
                pl.BlockSpec((1, D_MODEL), lambda i: (0, 0)),
                pl.BlockSpec(memory_space=pl.ANY)]
    args = [state_pool, z, z, pool_w, pool_scale, a_in]
    aliases = {5: 0}
    if new_hist is not None:
        in_specs.append(pl.BlockSpec(memory_space=pl.ANY))
        args.append(new_hist)
        aliases[6] = 1
    return pl.pallas_call(
        functools.partial(_pool_sample_kernel, n_new=n_new),
        out_shape=(jax.ShapeDtypeStruct(a_in.shape, BF16),
                   jax.ShapeDtypeStruct(state_pool.shape, F32)),
        grid=(n_seq // SB,),
        in_specs=in_specs,
        out_specs=(pl.BlockSpec((rows, D_MODEL), lambda i: (blk0 + i, 0)),
                   pl.BlockSpec(hist_blk, lambda i: (layer, i, 0, 0))),
        input_output_aliases=aliases,
        compiler_params=_cparams("arbitrary"),
        name="pool_sample",
    )(*args)


def _head_norm_gate(o, gain, gate):
    mu = jnp.mean(o, axis=-1, keepdims=True)
    oc = o - mu
    on = oc * lax.rsqrt(jnp.mean(oc * oc, axis=-1, keepdims=True) + EPS)
    return ((on * gain) * _silu(gate)).astype(BF16)


_STATE = (HEADS, HEAD_DIM, HEAD_DIM)


RB = 2


def _chunk_body(dec_ref, decay_ref, q_ref, k_ref, v_ref, rg_ref, gn_ref, s_scr, o_ref, n_valid):
    ridx = lax.broadcasted_iota(jnp.int32, (CHUNK, 1), 0).astype(F32)
    for h in range(HEADS):
        hs = slice(h * HEAD_DIM, (h + 1) * HEAD_DIM)
        lg = dec_ref[0, h]
        q = q_ref[:, hs].astype(BF16)
        k = k_ref[:, hs]
        v = v_ref[:, hs].astype(BF16)
        s_old = s_scr[h]
        scores = _dot_nt(q, k.astype(BF16)) * decay_ref[h]
        intra = _dot(scores.astype(BF16), v)
        inter = _dot(q, s_old.astype(BF16)) * jnp.exp((ridx + 1.0) * lg)
        k_dec = (k * jnp.exp((n_valid - 1.0 - ridx) * lg)).astype(BF16)
        s_scr[h] = s_old * dec_ref[1, h] + _dot_tn(k_dec, v)
        o_ref[:, hs] = _head_norm_gate(intra + inter, gn_ref[:, hs], rg_ref[:, hs])


def _sample_body(gam_ref, q_ref, k_ref, v_ref, rg_ref, gn_ref, dmask_ref, qdec_ref, kdec_ref, s_ref,
                 o_ref, sout_ref, n_new):
    pairs = [(b, h) for b in range(RB) for h in range(HEADS)]

    def stack(ref):
        return jnp.concatenate(
            [ref[b * n_new:(b + 1) * n_new, h * HEAD_DIM:(h + 1) * HEAD_DIM] for b, h in pairs], axis=0)

    k = stack(k_ref)
    qb = stack(q_ref).astype(BF16)
    v = stack(v_ref)
    vb = v.astype(BF16)
    scores = _dot_nt(qb, k.astype(BF16)) * dmask_ref[...]
    intra = _dot(scores.astype(BF16), vb)
    k_dec = (k * kdec_ref[...]).astype(BF16)
    rows = lax.broadcasted_iota(jnp.int32, v.shape, 0)
    is_even = (rows & n_new) == 0
    v_even = jnp.where(is_even, v, 0.0).astype(BF16)
    v_odd = jnp.where(is_even, 0.0, v).astype(BF16)
    grp = 2 * n_new
    inter_parts = []
    for idx, (b, h) in enumerate(pairs):
        g0 = (idx // 2) * grp
        off = (idx % 2) * n_new
        s_old = s_ref[b, h]
        inter = _dot(qb[g0:g0 + grp], s_old.astype(BF16))
        inter_parts.append(inter[off:off + n_new])
        v_sel = v_even if idx % 2 == 0 else v_odd
        sout_ref[b, h] = s_old * gam_ref[h] + _dot_tn(k_dec[g0:g0 + grp], v_sel[g0:g0 + grp])
    o = intra + jnp.concatenate(inter_parts, axis=0) * qdec_ref[...]
    gain = jnp.concatenate(
        [jnp.broadcast_to(gn_ref[:, h * HEAD_DIM:(h + 1) * HEAD_DIM], (n_new, HEAD_DIM)) for _, h in pairs],
        axis=0)
    out = _head_norm_gate(o, gain, stack(rg_ref)).astype(F32)
    seqs = []
    for b in range(RB):
        seqs.append(jnp.concatenate(
            [out[(b * HEADS + h) * n_new:(b * HEADS + h + 1) * n_new] for h in range(HEADS)], axis=1))
    o_ref[...] = jnp.concatenate(seqs, axis=0).astype(BF16)


N_SAMPLE_IN = 9


def _ret_kernel(dec_ref, decay_ref, q_ref, k_ref, v_ref, rg_ref, gn_ref, s0_ref, *rest, n_valid, n_new):
    if n_new is None:
        o_ref, sfin_ref, s_scr = rest[-3:]
    else:
        o_ref, sfin_ref, so_ref, sout_ref, s_scr = rest[-5:]
    c = pl.program_id(1)

    @pl.when(c == 0)
    def _():
        s_scr[...] = s0_ref[...]

    _chunk_body(dec_ref, decay_ref, q_ref, k_ref, v_ref, rg_ref, gn_ref, s_scr, o_ref, n_valid)
    if n_new is not None:
        gam_ref, sq_ref, sk_ref, sv_ref, srg_ref, dmask_ref, qdec_ref, kdec_ref, s_ref = rest[:N_SAMPLE_IN]
        _sample_body(gam_ref, sq_ref, sk_ref, sv_ref, srg_ref, gn_ref, dmask_ref, qdec_ref, kdec_ref, s_ref,
                     so_ref, sout_ref, n_new)

    @pl.when(c == pl.num_programs(1) - 1)
    def _():
        sfin_ref[...] = s_scr[...]


def _retention(zr, z, lg, gn, decay, s0, fin, fin_shape, fin_index, n_batch, n_chunks, n_valid, sample=None):
    dec = jnp.stack([lg, jnp.exp(n_valid * lg)])

    def zmap(col):
        return lambda b, c: (b * n_chunks + c, col)

    const2 = lambda b, c: (0, 0)
    in_specs = [pl.BlockSpec(memory_space=pltpu.SMEM),
                pl.BlockSpec((HEADS, CHUNK, CHUNK), lambda b, c: (0, 0, 0)),
                pl.BlockSpec((CHUNK, D_MODEL), zmap(COL_Q)),
                pl.BlockSpec((CHUNK, D_MODEL), zmap(COL_K)),
                pl.BlockSpec((CHUNK, D_MODEL), zmap(COL_V)),
                pl.BlockSpec((CHUNK, D_MODEL), zmap(COL_RG)),
                pl.BlockSpec((1, D_MODEL), const2),
                pl.BlockSpec(_STATE, lambda b, c: (0, 0, 0))]
    args = [dec, decay, zr, zr, z, z, gn, s0]
    fin_blk = (None,) * (len(fin_shape) - 3) + _STATE
    out_shape = [jax.ShapeDtypeStruct((n_batch * n_chunks * CHUNK, D_MODEL), BF16),
                 jax.ShapeDtypeStruct(fin_shape, F32)]
    out_specs = [pl.BlockSpec((CHUNK, D_MODEL), zmap(0)),
                 pl.BlockSpec(fin_blk, lambda b, c: fin_index(b) + (0, 0, 0))]
    n_new = None
    new_state = None
    if sample is not None:
        gam, (dmask, qdec, kdec), state_ret, new_state, layer, n_new, row0 = sample
        rows = RB * n_new
        stack_rows = RB * HEADS * n_new
        assert state_ret.shape[1] == RB * n_batch * n_chunks, "one group of sample sequences per grid step"

        def smap(col):
            return lambda b, c: (row0 // rows + b * n_chunks + c, col)

        sblk = (None, RB) + _STATE
        state_map = lambda b, c: (layer, b * n_chunks + c, 0, 0, 0)
        in_specs += [pl.BlockSpec(memory_space=pltpu.SMEM),
                     pl.BlockSpec((rows, D_MODEL), smap(COL_Q)),
                     pl.BlockSpec((rows, D_MODEL), smap(COL_K)),
                     pl.BlockSpec((rows, D_MODEL), smap(COL_V)),
                     pl.BlockSpec((rows, D_MODEL), smap(COL_RG)),
                     pl.BlockSpec((stack_rows, stack_rows), const2),
                     pl.BlockSpec((stack_rows, HEAD_DIM), const2),
                     pl.BlockSpec((stack_rows, HEAD_DIM), const2),
                     pl.BlockSpec(sblk, state_map)]
        args += [gam, zr, zr, z, z, dmask, qdec, kdec, state_ret]
        out_shape += [jax.ShapeDtypeStruct((state_ret.shape[1] * n_new, D_MODEL), BF16),
                      jax.ShapeDtypeStruct(state_ret.shape, F32)]
        out_specs += [pl.BlockSpec((rows, D_MODEL), lambda b, c: (b * n_chunks + c, 0)),
                      pl.BlockSpec(sblk, state_map)]
    aliases = {}
    for out_idx, buf in ((1, fin), (3, new_state)):
        if buf is not None:
            aliases[len(args)] = out_idx
            in_specs.append(pl.BlockSpec(memory_space=pl.ANY))
            args.append(buf)
    return pl.pallas_call(
        functools.partial(_ret_kernel, n_valid=n_valid, n_new=n_new),
        out_shape=tuple(out_shape),
        grid=(n_batch, n_chunks),
        in_specs=in_specs,
        out_specs=tuple(out_specs),
        scratch_shapes=[pltpu.VMEM(_STATE, F32)],
        input_output_aliases=aliases,
        compiler_params=_cparams("arbitrary", "arbitrary"),
        name="retention" if sample is None else "retention_with_sample",
    )(*args)


TN_MERGE = 1024


def _merge_kernel(a_ref, bp_ref, bs_ref, gp_ref, gr_ref, am_ref, bm_ref, gpm_ref, grm_ref, wp_ref, wr_ref,
                  o_ref, om_ref, *, n_p, n_s):
    i = pl.program_id(1)

    def tile(a, b, gp, gr, out):
        pool_branch = _dot(a[...], wp_ref[...])
        ret_branch = _dot(b[...], wr_ref[...])
        out[...] = (jax.nn.sigmoid(gp[...]) * pool_branch + jax.nn.sigmoid(gr[...]) * ret_branch).astype(BF16)

    @pl.when(i < n_p)
    def _():
        tile(a_ref, bp_ref, gp_ref, gr_ref, o_ref)

    @pl.when((i >= n_p) & (i < n_p + n_s))
    def _():
        tile(a_ref, bs_ref, gp_ref, gr_ref, o_ref)

    @pl.when(i == n_p + n_s)
    def _():
        tile(am_ref, bm_ref, gpm_ref, grm_ref, om_ref)


def _merge(a_in, b_prompt, b_sample, z, a_meta, b_meta, zm, proj_pool, proj_ret):
    m = a_in.shape[0]
    n_p = b_prompt.shape[0] // TM
    n_s = b_sample.shape[0] // TM
    n_main = n_p + n_s
    assert n_main * TM == m
    nt = D_MODEL // TN_MERGE

    def main_map(j, i):
        return (jnp.minimum(i, n_main - 1), 0)

    def gate_map(off):
        return lambda j, i: (jnp.minimum(i, n_main - 1), off + j)

    const = lambda j, i: (0, 0)
    return pl.pallas_call(
        functools.partial(_merge_kernel, n_p=n_p, n_s=n_s),
        out_shape=(jax.ShapeDtypeStruct((m, D_MODEL), BF16),
                   jax.ShapeDtypeStruct((CHUNK, D_MODEL), BF16)),
        grid=(nt, n_main + 1),
        in_specs=[pl.BlockSpec((TM, D_MODEL), main_map),
                  pl.BlockSpec((TM, D_MODEL), lambda j, i: (jnp.minimum(i, n_p - 1), 0)),
                  pl.BlockSpec((TM, D_MODEL), lambda j, i: (jnp.clip(i - n_p, 0, n_s - 1), 0)),
                  pl.BlockSpec((TM, TN_MERGE), gate_map(COL_GP * nt)),
                  pl.BlockSpec((TM, TN_MERGE), gate_map(COL_GR * nt)),
                  pl.BlockSpec((CHUNK, D_MODEL), const),
                  pl.BlockSpec((CHUNK, D_MODEL), const),
                  pl.BlockSpec((CHUNK, TN_MERGE), lambda j, i: (0, COL_GP * nt + j)),
                  pl.BlockSpec((CHUNK, TN_MERGE), lambda j, i: (0, COL_GR * nt + j)),
                  pl.BlockSpec((D_MODEL, TN_MERGE), lambda j, i: (0, j)),
                  pl.BlockSpec((D_MODEL, TN_MERGE), lambda j, i: (0, j))],
        out_specs=(pl.BlockSpec((TM, TN_MERGE), gate_map(0)),
                   pl.BlockSpec((CHUNK, TN_MERGE), lambda j, i: (0, j))),
        compiler_params=_cparams("arbitrary", "arbitrary"),
        name="merge_branches",
    )(a_in, b_prompt, b_sample, z, z, a_meta, b_meta, zm, zm, proj_pool, proj_ret)


def _out_kernel(m_ref, mm_ref, w_ref, g_ref, *rest, first, last, n_p, n_s):
    rest = list(rest)
    x_refs = [rest.pop(0) for _ in range(3 if first else 2)]
    i = pl.program_id(0)

    def emit(x, merged, outs):
        xn = x + _dot(merged[...], w_ref[...])
        y = _rms(xn, g_ref[...])
        if last:
            outs[0][...] = y
        else:
            outs[0][...] = xn
            outs[1][...] = y.astype(BF16)

    main_outs = [rest[0:1], rest[1:2]] if last else [rest[0:2], rest[0:2]]

    @pl.when(i < n_p)
    def _():
        emit(x_refs[0][...], m_ref, main_outs[0])

    @pl.when((i >= n_p) & (i < n_p + n_s))
    def _():
        emit(x_refs[1 if first else 0][...], m_ref, main_outs[1])

    if not last:
        @pl.when(i == n_p + n_s)
        def _():
            emit(_meta_rows(x_refs[2]) if first else x_refs[1][...], mm_ref, rest[2:4])


def _out_projection(merged, merged_meta, w_out, g_next, x_parts, first, last, n_p, n_s):
    n = n_p + n_s
    main_spec = pl.BlockSpec((TM, D_MODEL), lambda i: (jnp.minimum(i, n - 1), 0))
    meta_spec = pl.BlockSpec((CHUNK, D_MODEL), lambda i: (0, 0))
    in_specs = [main_spec, meta_spec,
                pl.BlockSpec((D_MODEL, D_MODEL), lambda i: (0, 0)),
                pl.BlockSpec((1, D_MODEL), lambda i: (0, 0))]
    if first:
        in_specs += _x_specs(n_p, n_s) + [pl.BlockSpec((N_META, D_MODEL), lambda i: (0, 0))]
    else:
        in_specs += [main_spec, meta_spec]
    if last:
        out_shape = (jax.ShapeDtypeStruct((n_p * TM, D_MODEL), F32),
                     jax.ShapeDtypeStruct((n_s * TM, D_MODEL), F32))
        out_specs = tuple(_x_specs(n_p, n_s))
    else:
        out_shape = (jax.ShapeDtypeStruct((n * TM, D_MODEL), F32), jax.ShapeDtypeStruct((n * TM, D_MODEL), BF16),
                     jax.ShapeDtypeStruct((CHUNK, D_MODEL), F32), jax.ShapeDtypeStruct((CHUNK, D_MODEL), BF16))
        out_specs = (main_spec, main_spec, meta_spec, meta_spec)
    return pl.pallas_call(
        functools.partial(_out_kernel, first=first, last=last, n_p=n_p, n_s=n_s),
        out_shape=out_shape,
        grid=(n if last else n + 1,),
        in_specs=in_specs,
        out_specs=out_specs,
        compiler_params=_cparams("arbitrary"),
        name="out_projection",
    )(merged, merged_meta, w_out, g_next, *x_parts)


def _log_decay():
    return jnp.log1p(-jnp.exp2(-5.0 - jnp.arange(HEADS, dtype=F32)))


def _rope_tables(pos):
    inv_freq = ROPE_BASE ** (-jnp.arange(HALF, dtype=F32) / HALF)
    ang = pos[:, None] * inv_freq[None, :]
    return jnp.cos(ang), jnp.sin(ang)


def _chunk_decay(lg):
    idx = jnp.arange(CHUNK, dtype=F32)
    diff = idx[:, None] - idx[None, :]
    causal = diff >= 0
    return jnp.where(causal[None], jnp.exp(jnp.where(causal, diff, 0.0)[None] * lg[:, None, None]), 0.0)


def _sample_decay_tables(lg, n_new):
    r = jnp.arange(RB * HEADS * n_new)
    head = (r // n_new) % HEADS
    tok = (r % n_new).astype(F32)
    lg_r = lg[head]
    same = (r[:, None] // n_new) == (r[None, :] // n_new)
    diff = tok[:, None] - tok[None, :]
    keep = same & (diff >= 0)
    dmask = jnp.where(keep, jnp.exp(jnp.where(keep, diff, 0.0) * lg_r[:, None]), 0.0)
    qdec = jnp.broadcast_to(jnp.exp((tok + 1.0) * lg_r)[:, None], (r.shape[0], HEAD_DIM))
    kdec = jnp.broadcast_to(jnp.exp((n_new - 1.0 - tok) * lg_r)[:, None], (r.shape[0], HEAD_DIM))
    return dmask, qdec, kdec


def kernel(x_prompt, x_sample, state_pool, state_ret, meta_tokens, norm_gain, w_in, pool_w, pool_scale,
           ret_gn_gain, proj_pool, proj_ret, w_out, final_norm):
    n_batch, seq = x_prompt.shape[:2]
    n_seq, n_new = x_sample.shape[:2]
    depth = norm_gain.shape[0]
    rows_prompt = n_batch * seq
    rows_sample = n_seq * n_new
    assert seq % PT == 0 and seq % CHUNK == 0 and n_seq % SB == 0 and n_seq % RB == 0
    assert n_new == 8, "sample tokens of one sequence must fill one f32 sublane tile"
    assert rows_prompt % TM == 0 and rows_sample % TM == 0 and (rows_prompt + rows_sample) % TM_IN == 0
    n_p = rows_prompt // TM
    n_s = rows_sample // TM

    xp = x_prompt.reshape(rows_prompt, D_MODEL).astype(F32)
    xs = x_sample.reshape(rows_sample, D_MODEL).astype(F32)
    meta = meta_tokens.astype(F32)
    state_pool = state_pool.astype(F32)
    state_ret = state_ret.astype(F32)
    w_in = w_in.astype(F32)

    trig_p = _rope_tables(N_META + jnp.arange(seq, dtype=F32))
    trig_s = _rope_tables(PAST_LEN + jnp.arange(n_new, dtype=F32))
    pos_meta = jnp.where(jnp.arange(CHUNK) < N_META, jnp.arange(CHUNK, dtype=F32), 0.0)
    trig = tuple(jnp.concatenate([jnp.tile(tp, (n_batch, 1)), jnp.tile(ts, (n_seq, 1))])
                 for tp, ts in zip(trig_p, trig_s)) + _rope_tables(pos_meta)
    lg = _log_decay()
    decay = _chunk_decay(lg)
    gam_new = jnp.exp(n_new * lg)
    tables = _sample_decay_tables(lg, n_new)
    zero_state = jnp.zeros(_STATE, F32)
    row = lambda a: a.reshape(1, D_MODEL).astype(F32)

    h, h_meta = _first_norm(xp, xs, meta, row(norm_gain[0]), n_p, n_s)
    x_parts = (xp, xs, meta)
    pool_p = []
    ret_p = pool_s = ret_s = None
    for l in range(depth):
        zr, zrm = _in_projection(h, h_meta, w_in, l, ROPE_SEGS, trig)
        z, zm = _in_projection(h, h_meta, w_in, l, PLAIN_SEGS, None)
        pw = pool_w[l].astype(BF16)
        ps = row(pool_scale[l])
        gn = row(ret_gn_gain[l])

        a_meta = _pool_meta(zm, pw, ps)
        a_in, hist_p = _pool_prompt(z, zm, pw, ps, n_batch, seq)
        a_in, pool_s = _pool_sample(z, state_pool, a_in, pool_s, pw, ps, l, n_seq, n_new, rows_prompt)
        pool_p.append(hist_p)

        b_meta, s_meta = _retention(zrm, zm, lg, gn, decay, zero_state, None, _STATE, lambda b: (), 1, 1,
                                    N_META)
        b_prompt, ret_p, b_sample, ret_s = _retention(
            zr, z, lg, gn, decay, s_meta, ret_p, (depth, n_batch) + _STATE, lambda b, l=l: (l, b),
            n_batch, seq // CHUNK, CHUNK,
            sample=(gam_new, tables, state_ret, ret_s, l, n_new, rows_prompt))

        merged, merged_meta = _merge(a_in, b_prompt, b_sample, z, a_meta, b_meta, zm,
                                     proj_pool[l].astype(BF16), proj_ret[l].astype(BF16))
        last = l == depth - 1
        g_next = row(final_norm if last else norm_gain[l + 1])
        outs = _out_projection(merged, merged_meta, w_out[l].astype(BF16), g_next, x_parts, l == 0, last,
                               n_p, n_s)
        if last:
            y_prompt, y_sample = outs
        else:
            x_new, h, x_meta, h_meta = outs
            x_parts = (x_new, x_meta)

    return (y_prompt.reshape(n_batch, seq, D_MODEL), y_sample.reshape(n_seq, n_new, D_MODEL),
            jnp.stack(pool_p), ret_p, pool_s, ret_s)
```

```python
import functools

import jax
import jax.numpy as jnp
from jax import lax
from jax.experimental import pallas as pl
from jax.experimental.pallas import tpu as pltpu

F32 = jnp.float32
BF16 = jnp.bfloat16

D_MODEL = 2048
N_META = 16
POOL_WINDOWS = (2, 4, 8, 16)
N_GROUPS = len(POOL_WINDOWS)
POOL_GROUP = D_MODEL // N_GROUPS
POOL_HIST = max(POOL_WINDOWS) - 1
HEADS = 8
HEAD_DIM = D_MODEL // HEADS
HALF = HEAD_DIM // 2
CHUNK = 128
ROPE_BASE = 10000.0
EPS = 1e-6
PAST_LEN = 16384
SEG_U, SEG_PG, SEG_Q, SEG_K, SEG_V, SEG_RG, SEG_GP, SEG_GR = range(8)
N_SEG = 8
ROPE_SEGS = (SEG_Q, SEG_K)
COL_Q, COL_K = range(len(ROPE_SEGS))
PLAIN_SEGS = (SEG_U, SEG_PG, SEG_V, SEG_RG, SEG_GP, SEG_GR)
COL_U, COL_PG, COL_V, COL_RG, COL_GP, COL_GR = range(len(PLAIN_SEGS))

TM = 512
TM_IN = 1024
TN_IN = 1024
PER_SEG = D_MODEL // TN_IN
VMEM_LIMIT = 56 * 1024 * 1024


def _cparams(*sem):
    return pltpu.CompilerParams(dimension_semantics=sem, vmem_limit_bytes=VMEM_LIMIT)


def _dot(a, b):
    return jnp.dot(a, b, preferred_element_type=F32)


def _dot_nt(a, b):
    return lax.dot_general(a, b, (((1,), (1,)), ((), ())), preferred_element_type=F32)


def _dot_tn(a, b):
    return lax.dot_general(a, b, (((0,), (0,)), ((), ())), preferred_element_type=F32)


def _rms(x, g):
    return x * lax.rsqrt(jnp.mean(x * x, axis=-1, keepdims=True) + EPS) * g


def _silu(x):
    return x * jax.nn.sigmoid(x)


def _x_specs(n_p, n_s):
    return [pl.BlockSpec((TM, D_MODEL), lambda i: (jnp.minimum(i, n_p - 1), 0)),
            pl.BlockSpec((TM, D_MODEL), lambda i: (jnp.clip(i - n_p, 0, n_s - 1), 0))]


def _meta_rows(meta_ref):
    return jnp.concatenate([meta_ref[...], jnp.zeros((CHUNK - N_META, D_MODEL), F32)], axis=0)


def _norm_kernel(xp_ref, xs_ref, meta_ref, g_ref, h_ref, hm_ref, *, n_p, n_s):
    i = pl.program_id(0)

    @pl.when(i < n_p)
    def _():
        h_ref[...] = _rms(xp_ref[...], g_ref[...]).astype(BF16)

    @pl.when((i >= n_p) & (i < n_p + n_s))
    def _():
        h_ref[...] = _rms(xs_ref[...], g_ref[...]).astype(BF16)

    @pl.when(i == n_p + n_s)
    def _():
        hm_ref[...] = _rms(_meta_rows(meta_ref), g_ref[...]).astype(BF16)


def _first_norm(xp, xs, meta, g, n_p, n_s):
    n = n_p + n_s
    return pl.pallas_call(
        functools.partial(_norm_kernel, n_p=n_p, n_s=n_s),
        out_shape=(jax.ShapeDtypeStruct((n * TM, D_MODEL), BF16),
                   jax.ShapeDtypeStruct((CHUNK, D_MODEL), BF16)),
        grid=(n + 1,),
        in_specs=_x_specs(n_p, n_s) + [pl.BlockSpec((N_META, D_MODEL), lambda i: (0, 0)),
                                       pl.BlockSpec((1, D_MODEL), lambda i: (0, 0))],
        out_specs=(pl.BlockSpec((TM, D_MODEL), lambda i: (jnp.minimum(i, n - 1), 0)),
                   pl.BlockSpec((CHUNK, D_MODEL), lambda i: (0, 0))),
        compiler_params=_cparams("arbitrary"),
        name="first_norm",
    )(xp, xs, meta, g)


RB = 2
_STATE = (HEADS, HEAD_DIM, HEAD_DIM)


def _head_norm_gate(o, gain, gate):
    mu = jnp.mean(o, axis=-1, keepdims=True)
    oc = o - mu
    on = oc * lax.rsqrt(jnp.mean(oc * oc, axis=-1, keepdims=True) + EPS)
    return ((on * gain) * _silu(gate)).astype(BF16)


def _sample_body(gam_ref, q_ref, k_ref, v_ref, rg_ref, gn_ref, dmask_ref, qdec_ref, kdec_ref, s_ref,
                 o_ref, sout_ref, n_new):
    pairs = [(b, h) for b in range(RB) for h in range(HEADS)]

    def stack(ref):
        return jnp.concatenate(
            [ref[b * n_new:(b + 1) * n_new, h * HEAD_DIM:(h + 1) * HEAD_DIM] for b, h in pairs], axis=0)

    k = stack(k_ref)
    qb = stack(q_ref).astype(BF16)
    v = stack(v_ref)
    vb = v.astype(BF16)
    scores = _dot_nt(qb, k.astype(BF16)) * dmask_ref[...]
    intra = _dot(scores.astype(BF16), vb)
    k_dec = (k * kdec_ref[...]).astype(BF16)
    rows = lax.broadcasted_iota(jnp.int32, v.shape, 0)
    is_even = (rows & n_new) == 0
    v_even = jnp.where(is_even, v, 0.0).astype(BF16)
    v_odd = jnp.where(is_even, 0.0, v).astype(BF16)
    grp = 2 * n_new
    inter_parts = []
    for idx, (b, h) in enumerate(pairs):
        g0 = (idx // 2) * grp
        off = (idx % 2) * n_new
        s_old = s_ref[b, h]
        inter = _dot(qb[g0:g0 + grp], s_old.astype(BF16))
        inter_parts.append(inter[off:off + n_new])
        v_sel = v_even if idx % 2 == 0 else v_odd
        sout_ref[b, h] = s_old * gam_ref[h] + _dot_tn(k_dec[g0:g0 + grp], v_sel[g0:g0 + grp])
    o = intra + jnp.concatenate(inter_parts, axis=0) * qdec_ref[...]
    gain = jnp.concatenate(
        [jnp.broadcast_to(gn_ref[:, h * HEAD_DIM:(h + 1) * HEAD_DIM], (n_new, HEAD_DIM)) for _, h in pairs],
        axis=0)
    out = _head_norm_gate(o, gain, stack(rg_ref)).astype(F32)
    seqs = []
    for b in range(RB):
        seqs.append(jnp.concatenate(
            [out[(b * HEADS + h) * n_new:(b * HEADS + h + 1) * n_new] for h in range(HEADS)], axis=1))
    o_ref[...] = jnp.concatenate(seqs, axis=0).astype(BF16)


def _rope_store(z, o_ref, cos_ref, sin_ref, scale):
    cos = cos_ref[...]
    sin = sin_ref[...]
    for hh in range(TN_IN // HEAD_DIM):
        lo = slice(hh * HEAD_DIM, hh * HEAD_DIM + HALF)
        hi = slice(hh * HEAD_DIM + HALF, (hh + 1) * HEAD_DIM)
        t1 = z[:, lo]
        t2 = z[:, hi]
        o_ref[:, lo] = (t1 * cos - t2 * sin) * scale
        o_ref[:, hi] = (t2 * cos + t1 * sin) * scale


def _k_scale(is_k):
    return jnp.where(is_k, HEAD_DIM ** -0.5, 1.0).astype(F32)


def _inproj_small_kernel(hs_ref, hm_ref, w_ref, cos_ref, sin_ref, cosm_ref, sinm_ref, zs_ref, zm_ref, wb_ref):
    s = pl.program_id(0)
    wb_ref[...] = w_ref[...].astype(BF16)
    zs_ref[...] = _dot(hs_ref[...], wb_ref[...])
    zm_ref[...] = _dot(hm_ref[...], wb_ref[...])

    @pl.when((s >= SEG_Q * PER_SEG) & (s < (SEG_K + 1) * PER_SEG))
    def _():
        scale = _k_scale(s >= SEG_K * PER_SEG)
        _rope_store(zs_ref[...], zs_ref, cos_ref, sin_ref, scale)
        _rope_store(zm_ref[...], zm_ref, cosm_ref, sinm_ref, scale)


def _in_projection_small(h, h_meta, w_in, layer, trig_s, trig_m, row0, rows):
    assert row0 % rows == 0
    const = lambda s: (0, 0)
    col = lambda s: (0, s)
    return pl.pallas_call(
        _inproj_small_kernel,
        out_shape=(jax.ShapeDtypeStruct((rows, N_SEG * D_MODEL), F32),
                   jax.ShapeDtypeStruct((CHUNK, N_SEG * D_MODEL), F32),
                   jax.ShapeDtypeStruct((D_MODEL, N_SEG * D_MODEL), BF16)),
        grid=(N_SEG * PER_SEG,),
        in_specs=[pl.BlockSpec((rows, D_MODEL), lambda s: (row0 // rows, 0)),
                  pl.BlockSpec((CHUNK, D_MODEL), const),
                  pl.BlockSpec((None, D_MODEL, TN_IN), lambda s: (layer, 0, s)),
                  pl.BlockSpec((rows, HALF), const),
                  pl.BlockSpec((rows, HALF), const),
                  pl.BlockSpec((CHUNK, HALF), const),
                  pl.BlockSpec((CHUNK, HALF), const)],
        out_specs=(pl.BlockSpec((rows, TN_IN), col),
                   pl.BlockSpec((CHUNK, TN_IN), col),
                   pl.BlockSpec((D_MODEL, TN_IN), col)),
        compiler_params=_cparams("arbitrary"),
        name="in_projection_small",
    )(h, h_meta, w_in, *trig_s, *trig_m)


N_SAMPLE_IN = 10


def _inproj_kernel(h_ref, w_ref, *rest, rope, n_new, n_groups):
    if rope:
        cos_ref, sin_ref, o_ref = rest
        z = _dot(h_ref[...], w_ref[...])
        _rope_store(z, o_ref, cos_ref, sin_ref, _k_scale(pl.program_id(0) >= PER_SEG))
        return
    if n_new is None:
        rest[-1][...] = _dot(h_ref[...], w_ref[...])
        return
    o_ref, so_ref, sout_ref = rest[-3:]
    step = pl.program_id(0) * pl.num_programs(1) + pl.program_id(1)

    @pl.when(step < n_groups)
    def _():
        o_ref[...] = _dot(h_ref[...], w_ref[...])
        _sample_body(*rest[:N_SAMPLE_IN], so_ref, sout_ref, n_new)

    @pl.when(step >= n_groups)
    def _():
        o_ref[...] = _dot(h_ref[...], w_ref[...])


def _in_projection(h, wb, rows, segs, trig=None, seq=None, sample=None):
    assert rows % TM_IN == 0
    tiles = rows // TM_IN
    rope = trig is not None
    gap_at = next((j for j in range(1, len(segs)) if segs[j] != segs[j - 1] + 1), len(segs))
    gap = segs[gap_at] - segs[gap_at - 1] - 1 if gap_at < len(segs) else 0
    assert all(segs[j] == segs[0] + j + (gap if j >= gap_at else 0) for j in range(len(segs)))

    def wmap(s, i):
        j = s // PER_SEG
        seg = segs[0] + j + jnp.where(j >= gap_at, gap, 0)
        return (0, seg * PER_SEG + s % PER_SEG)

    in_specs = [pl.BlockSpec((TM_IN, D_MODEL), lambda s, i: (i, 0)),
                pl.BlockSpec((D_MODEL, TN_IN), wmap)]
    args = [h, wb]
    out_shape = [jax.ShapeDtypeStruct((rows, len(segs) * D_MODEL), F32)]
    out_specs = [pl.BlockSpec((TM_IN, TN_IN), lambda s, i: (i, s))]
    aliases = {}
    n_new = n_groups = None
    if rope:
        assert seq % TM_IN == 0
        in_specs += [pl.BlockSpec((TM_IN, HALF), lambda s, i: (i % (seq // TM_IN), 0)) for _ in range(2)]
        args += list(trig)
    if sample is not None:
        gam, gn, (dmask, qdec, kdec), state_ret, new_state, layer, n_new, zs = sample
        n_groups = state_ret.shape[1] // RB
        assert n_groups <= len(segs) * PER_SEG * tiles, "one group of sample sequences per grid step"
        rows_g = RB * n_new
        stack_rows = RB * HEADS * n_new

        def group(s, i):
            return jnp.minimum(s * tiles + i, n_groups - 1)

        def zmap(seg):
            return lambda s, i: (group(s, i), seg)

        const = lambda s, i: (0, 0)
        sblk = (None, RB) + _STATE
        state_map = lambda s, i: (layer, group(s, i), 0, 0, 0)
        in_specs += [pl.BlockSpec(memory_space=pltpu.SMEM),
                     pl.BlockSpec((rows_g, D_MODEL), zmap(SEG_Q)),
                     pl.BlockSpec((rows_g, D_MODEL), zmap(SEG_K)),
                     pl.BlockSpec((rows_g, D_MODEL), zmap(SEG_V)),
                     pl.BlockSpec((rows_g, D_MODEL), zmap(SEG_RG)),
                     pl.BlockSpec((1, D_MODEL), const),
                     pl.BlockSpec((stack_rows, stack_rows), const),
                     pl.BlockSpec((stack_rows, HEAD_DIM), const),
                     pl.BlockSpec((stack_rows, HEAD_DIM), const),
                     pl.BlockSpec(sblk, state_map)]
        args += [gam, zs, zs, zs, zs, gn, dmask, qdec, kdec, state_ret]
        out_shape += [jax.ShapeDtypeStruct((rows + n_groups * rows_g, D_MODEL), BF16),
                      jax.ShapeDtypeStruct(state_ret.shape, F32)]
        out_specs += [pl.BlockSpec((rows_g, D_MODEL), lambda s, i: (rows // rows_g + group(s, i), 0)),
                      pl.BlockSpec(sblk, state_map)]
        if new_state is not None:
            aliases[len(args)] = 2
            in_specs.append(pl.BlockSpec(memory_space=pl.ANY))
            args.append(new_state)
    outs = pl.pallas_call(
        functools.partial(_inproj_kernel, rope=rope, n_new=n_new, n_groups=n_groups),
        out_shape=tuple(out_shape),
        grid=(len(segs) * PER_SEG, tiles),
        in_specs=in_specs,
        out_specs=tuple(out_specs),
        input_output_aliases=aliases,
        compiler_params=_cparams("arbitrary", "arbitrary"),
        name="in_projection_rope" if rope else "in_projection",
    )(*args)
    return outs if sample is not None else outs[0]


def _window_sum(ext, w, base, rows):
    s = ext
    size = 1
    while size < w:
        s = s[size:] + s[:-size]
        size *= 2
    start = base - (w - 1)
    return s[start:start + rows]


def _pool_group(g, wsum, u, inv_cnt, pw_ref, ps_ref, gate):
    sl = slice(g * POOL_GROUP, (g + 1) * POOL_GROUP)
    pooled = wsum * inv_cnt - u
    mixed = _dot(pooled.astype(BF16), pw_ref[g])
    return (mixed * ps_ref[:, sl] * _silu(gate)).astype(BF16)


_POOL_W_SPEC = ((N_GROUPS, POOL_GROUP, POOL_GROUP), (0, 0, 0))


def _pool_meta_kernel(u_ref, pg_ref, pw_ref, ps_ref, o_ref):
    avail = lax.broadcasted_iota(jnp.int32, (N_META, 1), 0).astype(F32) + 1.0
    o_ref[N_META:] = jnp.zeros((CHUNK - N_META, D_MODEL), BF16)
    for g, w in enumerate(POOL_WINDOWS):
        sl = slice(g * POOL_GROUP, (g + 1) * POOL_GROUP)
        u = u_ref[:, sl]
        wsum = _window_sum(jnp.concatenate([jnp.zeros_like(u), u], axis=0), w, N_META, N_META)
        inv_cnt = 1.0 / jnp.minimum(float(w), avail)
        o_ref[0:N_META, sl] = _pool_group(g, wsum, u, inv_cnt, pw_ref, ps_ref, pg_ref[:, sl])


def _pool_meta(zm, pool_w, pool_scale):
    return pl.pallas_call(
        _pool_meta_kernel,
        out_shape=jax.ShapeDtypeStruct((CHUNK, D_MODEL), BF16),
        grid=(1,),
        in_specs=[pl.BlockSpec((N_META, D_MODEL), lambda i: (0, SEG_U)),
                  pl.BlockSpec((N_META, D_MODEL), lambda i: (0, SEG_PG)),
                  pl.BlockSpec(_POOL_W_SPEC[0], lambda i: _POOL_W_SPEC[1]),
                  pl.BlockSpec((1, D_MODEL), lambda i: (0, 0))],
        out_specs=pl.BlockSpec((CHUNK, D_MODEL), lambda i: (0, 0)),
        compiler_params=_cparams("arbitrary"),
        name="pool_meta",
    )(zm, zm, pool_w, pool_scale)


PT = 256


def _pool_prompt_kernel(prev_ref, meta_ref, u_ref, pg_ref, pw_ref, ps_ref, o_ref, hist_ref):
    t = pl.program_id(1)
    for g, w in enumerate(POOL_WINDOWS):
        sl = slice(g * POOL_GROUP, (g + 1) * POOL_GROUP)
        prev = jnp.where(t == 0, meta_ref[:, sl], prev_ref[:, sl])
        u = u_ref[:, sl]
        wsum = _window_sum(jnp.concatenate([prev, u], axis=0), w, N_META, PT)
        o_ref[:, sl] = _pool_group(g, wsum, u, 1.0 / w, pw_ref, ps_ref, pg_ref[:, sl])

    @pl.when(t == pl.num_programs(1) - 1)
    def _():
        hist_ref[0] = u_ref[PT - POOL_HIST:PT, :]


def _pool_prompt(z, zm, pool_w, pool_scale, n_batch, seq, total_rows):
    tiles = seq // PT
    sub = PT // N_META

    def prev_map(b, t):
        return (jnp.maximum(b * (seq // N_META) + t * sub - 1, 0), COL_U)

    return pl.pallas_call(
        _pool_prompt_kernel,
        out_shape=(jax.ShapeDtypeStruct((total_rows, D_MODEL), BF16),
                   jax.ShapeDtypeStruct((n_batch, POOL_HIST, D_MODEL), F32)),
        grid=(n_batch, tiles),
        in_specs=[pl.BlockSpec((N_META, D_MODEL), prev_map),
                  pl.BlockSpec((N_META, D_MODEL), lambda b, t: (0, SEG_U)),
                  pl.BlockSpec((PT, D_MODEL), lambda b, t: (b * tiles + t, COL_U)),
                  pl.BlockSpec((PT, D_MODEL), lambda b, t: (b * tiles + t, COL_PG)),
                  pl.BlockSpec(_POOL_W_SPEC[0], lambda b, t: _POOL_W_SPEC[1]),
                  pl.BlockSpec((1, D_MODEL), lambda b, t: (0, 0))],
        out_specs=(pl.BlockSpec((PT, D_MODEL), lambda b, t: (b * tiles + t, 0)),
                   pl.BlockSpec((1, POOL_HIST, D_MODEL), lambda b, t: (b, 0, 0))),
        compiler_params=_cparams("arbitrary", "arbitrary"),
        name="pool_prompt",
    )(z, zm, z, z, pool_w, pool_scale)


SB = 16


def _pool_sample_kernel(hist_ref, u_ref, pg_ref, pw_ref, ps_ref, *rest, n_new):
    o_ref, nh_ref = rest[-2:]
    per = 1 + POOL_HIST + n_new
    zero_row = jnp.zeros((1, POOL_GROUP), F32)
    for g, w in enumerate(POOL_WINDOWS):
        sl = slice(g * POOL_GROUP, (g + 1) * POOL_GROUP)
        u = u_ref[:, sl]
        pieces = []
        for b in range(SB):
            pieces += [zero_row, hist_ref[b, :, sl], u[b * n_new:(b + 1) * n_new]]
        ext = jnp.concatenate(pieces, axis=0)
        for b in range(SB):
            nh_ref[b, :, sl] = ext[(b + 1) * per - POOL_HIST:(b + 1) * per]
        s = _window_sum(ext, w, 1 + POOL_HIST, SB * per - 1 - POOL_HIST)
        wsum = jnp.concatenate([s[b * per:b * per + n_new] for b in range(SB)], axis=0)
        o_ref[:, sl] = _pool_group(g, wsum, u, 1.0 / w, pw_ref, ps_ref, pg_ref[:, sl])


def _pool_sample(zs, state_pool, a_buf, new_hist, pool_w, pool_scale, layer, n_seq, n_new, row0):
    rows = SB * n_new
    hist_blk = (None, SB, POOL_HIST, D_MODEL)
    in_specs = [pl.BlockSpec(hist_blk, lambda i: (layer, i, 0, 0)),
                pl.BlockSpec((rows, D_MODEL), lambda i: (i, SEG_U)),
                pl.BlockSpec((rows, D_MODEL), lambda i: (i, SEG_PG)),
                pl.BlockSpec(_POOL_W_SPEC[0], lambda i: _POOL_W_SPEC[1]),
                pl.BlockSpec((1, D_MODEL), lambda i: (0, 0)),
                pl.BlockSpec(memory_space=pl.ANY)]
    args = [state_pool, zs, zs, pool_w, pool_scale, a_buf]
    aliases = {5: 0}
    if new_hist is not None:
        aliases[len(args)] = 1
        in_specs.append(pl.BlockSpec(memory_space=pl.ANY))
        args.append(new_hist)
    return pl.pallas_call(
        functools.partial(_pool_sample_kernel, n_new=n_new),
        out_shape=(jax.ShapeDtypeStruct(a_buf.shape, BF16),
                   jax.ShapeDtypeStruct(state_pool.shape, F32)),
        grid=(n_seq // SB,),
        in_specs=in_specs,
        out_specs=(pl.BlockSpec((rows, D_MODEL), lambda i: (row0 // rows + i, 0)),
                   pl.BlockSpec(hist_blk, lambda i: (layer, i, 0, 0))),
        input_output_aliases=aliases,
        compiler_params=_cparams("arbitrary"),
        name="pool_sample",
    )(*args)


def _ret_kernel(dec_ref, decay_ref, q_ref, k_ref, v_ref, rg_ref, gn_ref, s0_ref, *rest, n_valid):
    o_ref, sfin_ref, s_scr = rest[-3:]
    c = pl.program_id(1)

    @pl.when(c == 0)
    def _():
        s_scr[...] = s0_ref[...]

    ridx = lax.broadcasted_iota(jnp.int32, (CHUNK, 1), 0).astype(F32)
    for h in range(HEADS):
        hs = slice(h * HEAD_DIM, (h + 1) * HEAD_DIM)
        lg = dec_ref[0, h]
        q = q_ref[:, hs].astype(BF16)
        k = k_ref[:, hs]
        v = v_ref[:, hs].astype(BF16)
        s_old = s_scr[h]
        scores = _dot_nt(q, k.astype(BF16)) * decay_ref[h]
        intra = _dot(scores.astype(BF16), v)
        inter = _dot(q, s_old.astype(BF16)) * jnp.exp((ridx + 1.0) * lg)
        k_dec = (k * jnp.exp((n_valid - 1.0 - ridx) * lg)).astype(BF16)
        s_scr[h] = s_old * dec_ref[1, h] + _dot_tn(k_dec, v)
        o_ref[:, hs] = _head_norm_gate(intra + inter, gn_ref[:, hs], rg_ref[:, hs])

    @pl.when(c == pl.num_programs(1) - 1)
    def _():
        sfin_ref[...] = s_scr[...]


def _retention(srcs, lg, gn, decay, s0, b_buf, fin, fin_shape, fin_index, n_batch, n_chunks, n_valid):
    dec = jnp.stack([lg, jnp.exp(n_valid * lg)])

    def zmap(col):
        return lambda b, c: (b * n_chunks + c, col)

    in_specs = [pl.BlockSpec(memory_space=pltpu.SMEM),
                pl.BlockSpec((HEADS, CHUNK, CHUNK), lambda b, c: (0, 0, 0))]
    in_specs += [pl.BlockSpec((CHUNK, D_MODEL), zmap(col)) for _, col in srcs]
    in_specs += [pl.BlockSpec((1, D_MODEL), lambda b, c: (0, 0)),
                 pl.BlockSpec(_STATE, lambda b, c: (0, 0, 0))]
    args = [dec, decay] + [a for a, _ in srcs] + [gn, s0]
    aliases = {}
    for out_idx, buf in enumerate((b_buf, fin)):
        if buf is not None:
            aliases[len(args)] = out_idx
            in_specs.append(pl.BlockSpec(memory_space=pl.ANY))
            args.append(buf)
    fin_blk = (None,) * (len(fin_shape) - 3) + _STATE
    b_rows = n_batch * n_chunks * CHUNK if b_buf is None else b_buf.shape[0]
    return pl.pallas_call(
        functools.partial(_ret_kernel, n_valid=n_valid),
        out_shape=(jax.ShapeDtypeStruct((b_rows, D_MODEL), BF16),
                   jax.ShapeDtypeStruct(fin_shape, F32)),
        grid=(n_batch, n_chunks),
        in_specs=in_specs,
        out_specs=(pl.BlockSpec((CHUNK, D_MODEL), zmap(0)),
                   pl.BlockSpec(fin_blk, lambda b, c: fin_index(b) + (0, 0, 0))),
        scratch_shapes=[pltpu.VMEM(_STATE, F32)],
        input_output_aliases=aliases,
        compiler_params=_cparams("arbitrary", "arbitrary"),
        name="retention",
    )(*args)


TN_MERGE = 1024


def _merge_kernel(a_ref, b_ref, gpp_ref, grp_ref, gps_ref, grs_ref,
                  am_ref, bm_ref, gpm_ref, grm_ref, wp_ref, wr_ref, o_ref, om_ref, *, n_p, n_s):
    i = pl.program_id(1)

    def tile(a, b, gp, gr, out):
        pool_branch = _dot(a[...], wp_ref[...])
        ret_branch = _dot(b[...], wr_ref[...])
        out[...] = (jax.nn.sigmoid(gp[...]) * pool_branch + jax.nn.sigmoid(gr[...]) * ret_branch).astype(BF16)

    @pl.when(i < n_p)
    def _():
        tile(a_ref, b_ref, gpp_ref, grp_ref, o_ref)

    @pl.when((i >= n_p) & (i < n_p + n_s))
    def _():
        tile(a_ref, b_ref, gps_ref, grs_ref, o_ref)

    @pl.when(i == n_p + n_s)
    def _():
        tile(am_ref, bm_ref, gpm_ref, grm_ref, om_ref)


def _merge(a_in, b_in, z, zs, a_meta, b_meta, zm, proj_pool, proj_ret):
    n_p = z.shape[0] // TM
    n_s = zs.shape[0] // TM
    n_main = n_p + n_s
    assert a_in.shape[0] == n_main * TM and b_in.shape[0] == n_main * TM
    nt = D_MODEL // TN_MERGE

    def p_map(col):
        return lambda j, i: (jnp.minimum(i, n_p - 1), col(j))

    def s_map(col):
        return lambda j, i: (jnp.clip(i - n_p, 0, n_s - 1), col(j))

    row = pl.BlockSpec((TM, D_MODEL), lambda j, i: (jnp.minimum(i, n_main - 1), 0))
    gate = lambda ref_map, seg: pl.BlockSpec((TM, TN_MERGE), ref_map(lambda j: seg * nt + j))
    const = lambda j, i: (0, 0)
    return pl.pallas_call(
        functools.partial(_merge_kernel, n_p=n_p, n_s=n_s),
        out_shape=(jax.ShapeDtypeStruct((n_main * TM, D_MODEL), BF16),
                   jax.ShapeDtypeStruct((CHUNK, D_MODEL), BF16)),
        grid=(nt, n_main + 1),
        in_specs=[row, row,
                  gate(p_map, COL_GP), gate(p_map, COL_GR), gate(s_map, SEG_GP), gate(s_map, SEG_GR),
                  pl.BlockSpec((CHUNK, D_MODEL), const),
                  pl.BlockSpec((CHUNK, D_MODEL), const),
                  pl.BlockSpec((CHUNK, TN_MERGE), lambda j, i: (0, SEG_GP * nt + j)),
                  pl.BlockSpec((CHUNK, TN_MERGE), lambda j, i: (0, SEG_GR * nt + j)),
                  pl.BlockSpec((D_MODEL, TN_MERGE), lambda j, i: (0, j)),
                  pl.BlockSpec((D_MODEL, TN_MERGE), lambda j, i: (0, j))],
        out_specs=(pl.BlockSpec((TM, TN_MERGE), lambda j, i: (jnp.minimum(i, n_main - 1), j)),
                   pl.BlockSpec((CHUNK, TN_MERGE), lambda j, i: (0, j))),
        compiler_params=_cparams("arbitrary", "arbitrary"),
        name="merge_branches",
    )(a_in, b_in, z, z, zs, zs, a_meta, b_meta, zm, zm, proj_pool, proj_ret)


def _out_kernel(m_ref, mm_ref, w_ref, g_ref, *rest, first, last, n_p, n_s):
    rest = list(rest)
    x_refs = [rest.pop(0) for _ in range(3 if first else 2)]
    i = pl.program_id(0)

    def emit(x, merged, outs):
        xn = x + _dot(merged[...], w_ref[...])
        y = _rms(xn, g_ref[...])
        if last:
            outs[0][...] = y
        else:
            outs[0][...] = xn
            outs[1][...] = y.astype(BF16)

    main_outs = [rest[0:1], rest[1:2]] if last else [rest[0:2], rest[0:2]]

    @pl.when(i < n_p)
    def _():
        emit(x_refs[0][...], m_ref, main_outs[0])

    @pl.when((i >= n_p) & (i < n_p + n_s))
    def _():
        emit(x_refs[1 if first else 0][...], m_ref, main_outs[1])

    if not last:
        @pl.when(i == n_p + n_s)
        def _():
            emit(_meta_rows(x_refs[2]) if first else x_refs[1][...], mm_ref, rest[2:4])


def _out_projection(merged, merged_meta, w_out, g_next, x_parts, first, last, n_p, n_s):
    n = n_p + n_s
    main_spec = pl.BlockSpec((TM, D_MODEL), lambda i: (jnp.minimum(i, n - 1), 0))
    meta_spec = pl.BlockSpec((CHUNK, D_MODEL), lambda i: (0, 0))
    in_specs = [main_spec, meta_spec,
                pl.BlockSpec((D_MODEL, D_MODEL), lambda i: (0, 0)),
                pl.BlockSpec((1, D_MODEL), lambda i: (0, 0))]
    if first:
        in_specs += _x_specs(n_p, n_s) + [pl.BlockSpec((N_META, D_MODEL), lambda i: (0, 0))]
    else:
        in_specs += [main_spec, meta_spec]
    if last:
        out_shape = (jax.ShapeDtypeStruct((n_p * TM, D_MODEL), F32),
                     jax.ShapeDtypeStruct((n_s * TM, D_MODEL), F32))
        out_specs = tuple(_x_specs(n_p, n_s))
    else:
        out_shape = (jax.ShapeDtypeStruct((n * TM, D_MODEL), F32), jax.ShapeDtypeStruct((n * TM, D_MODEL), BF16),
                     jax.ShapeDtypeStruct((CHUNK, D_MODEL), F32), jax.ShapeDtypeStruct((CHUNK, D_MODEL), BF16))
        out_specs = (main_spec, main_spec, meta_spec, meta_spec)
    return pl.pallas_call(
        functools.partial(_out_kernel, first=first, last=last, n_p=n_p, n_s=n_s),
        out_shape=out_shape,
        grid=(n if last else n + 1,),
        in_specs=in_specs,
        out_specs=out_specs,
        compiler_params=_cparams("arbitrary"),
        name="out_projection",
    )(merged, merged_meta, w_out, g_next, *x_parts)


def _log_decay():
    return jnp.log1p(-jnp.exp2(-5.0 - jnp.arange(HEADS, dtype=F32)))


def _rope_tables(pos):
    inv_freq = ROPE_BASE ** (-jnp.arange(HALF, dtype=F32) / HALF)
    ang = pos[:, None] * inv_freq[None, :]
    return jnp.cos(ang), jnp.sin(ang)


def _chunk_decay(lg):
    idx = jnp.arange(CHUNK, dtype=F32)
    diff = idx[:, None] - idx[None, :]
    causal = diff >= 0
    return jnp.where(causal[None], jnp.exp(jnp.where(causal, diff, 0.0)[None] * lg[:, None, None]), 0.0)


def _sample_decay_tables(lg, n_new):
    r = jnp.arange(RB * HEADS * n_new)
    head = (r // n_new) % HEADS
    tok = (r % n_new).astype(F32)
    lg_r = lg[head]
    same = (r[:, None] // n_new) == (r[None, :] // n_new)
    diff = tok[:, None] - tok[None, :]
    keep = same & (diff >= 0)
    dmask = jnp.where(keep, jnp.exp(jnp.where(keep, diff, 0.0) * lg_r[:, None]), 0.0)
    qdec = jnp.broadcast_to(jnp.exp((tok + 1.0) * lg_r)[:, None], (r.shape[0], HEAD_DIM))
    kdec = jnp.broadcast_to(jnp.exp((n_new - 1.0 - tok) * lg_r)[:, None], (r.shape[0], HEAD_DIM))
    return dmask, qdec, kdec


def kernel(x_prompt, x_sample, state_pool, state_ret, meta_tokens, norm_gain, w_in, pool_w, pool_scale,
           ret_gn_gain, proj_pool, proj_ret, w_out, final_norm):
    n_batch, seq = x_prompt.shape[:2]
    n_seq, n_new = x_sample.shape[:2]
    depth = norm_gain.shape[0]
    rows_prompt = n_batch * seq
    rows_sample = n_seq * n_new
    assert seq % PT == 0 and seq % CHUNK == 0 and n_seq % SB == 0 and n_seq % RB == 0
    assert n_new == 8, "sample tokens of one sequence must fill one f32 sublane tile"
    assert rows_prompt % TM == 0 and rows_sample % TM == 0
    n_p = rows_prompt // TM
    n_s = rows_sample // TM

    xp = x_prompt.reshape(rows_prompt, D_MODEL).astype(F32)
    xs = x_sample.reshape(rows_sample, D_MODEL).astype(F32)
    meta = meta_tokens.astype(F32)
    state_pool = state_pool.astype(F32)
    state_ret = state_ret.astype(F32)
    w_in = w_in.astype(F32)

    trig_p = _rope_tables(N_META + jnp.arange(seq, dtype=F32))
    trig_s = tuple(jnp.tile(t, (n_seq, 1)) for t in _rope_tables(PAST_LEN + jnp.arange(n_new, dtype=F32)))
    trig_m = _rope_tables(jnp.where(jnp.arange(CHUNK) < N_META, jnp.arange(CHUNK, dtype=F32), 0.0))
    lg = _log_decay()
    decay = _chunk_decay(lg)
    gam_new = jnp.exp(n_new * lg)
    tables = _sample_decay_tables(lg, n_new)
    zero_state = jnp.zeros(_STATE, F32)
    row = lambda a: a.reshape(1, D_MODEL).astype(F32)

    h, h_meta = _first_norm(xp, xs, meta, row(norm_gain[0]), n_p, n_s)
    x_parts = (xp, xs, meta)
    pool_p = []
    ret_p = pool_s = ret_s = None
    for l in range(depth):
        pw = pool_w[l].astype(BF16)
        ps = row(pool_scale[l])
        gn = row(ret_gn_gain[l])

        zs, zm, wb = _in_projection_small(h, h_meta, w_in, l, trig_s, trig_m, rows_prompt, rows_sample)
        zr = _in_projection(h, wb, rows_prompt, ROPE_SEGS, trig=trig_p, seq=seq)
        z, b_in, ret_s = _in_projection(
            h, wb, rows_prompt, PLAIN_SEGS, sample=(gam_new, gn, tables, state_ret, ret_s, l, n_new, zs))

        a_meta = _pool_meta(zm, pw, ps)
        a_in, hist_p = _pool_prompt(z, zm, pw, ps, n_batch, seq, rows_prompt + rows_sample)
        a_in, pool_s = _pool_sample(zs, state_pool, a_in, pool_s, pw, ps, l, n_seq, n_new, rows_prompt)
        pool_p.append(hist_p)

        b_meta, s_meta = _retention(((zm, SEG_Q), (zm, SEG_K), (zm, SEG_V), (zm, SEG_RG)), lg, gn, decay,
                                    zero_state, None, None, _STATE, lambda b: (), 1, 1, N_META)
        b_in, ret_p = _retention(((zr, COL_Q), (zr, COL_K), (z, COL_V), (z, COL_RG)), lg, gn, decay,
                                 s_meta, b_in, ret_p, (depth, n_batch) + _STATE, lambda b, l=l: (l, b),
                                 n_batch, seq // CHUNK, CHUNK)

        merged, merged_meta = _merge(a_in, b_in, z, zs, a_meta, b_meta, zm,
                                     proj_pool[l].astype(BF16), proj_ret[l].astype(BF16))
        last = l == depth - 1
        g_next = row(final_norm if last else norm_gain[l + 1])
        outs = _out_projection(merged, merged_meta, w_out[l].astype(BF16), g_next, x_parts, l == 0, last,
                               n_p, n_s)
        if last:
            y_prompt, y_sample = outs
        else:
            x_new, h, x_meta, h_meta = outs
            x_parts = (x_new, x_meta)

    return (y_prompt.reshape(n_batch, seq, D_MODEL), y_sample.reshape(n_seq, n_new, D_MODEL),
            jnp.stack(pool_p), ret_p, pool_s, ret_s)
```

```python
import functools

import jax
import jax.numpy as jnp
from jax import lax
from jax.experimental import pallas as pl
from jax.experimental.pallas import tpu as pltpu

F32 = jnp.float32
BF16 = jnp.bfloat16

D_MODEL = 2048
N_META = 16
POOL_WINDOWS = (2, 4, 8, 16)
N_GROUPS = len(POOL_WINDOWS)
POOL_GROUP = D_MODEL // N_GROUPS
POOL_HIST = max(POOL_WINDOWS) - 1
HEADS = 8
HEAD_DIM = D_MODEL // HEADS
HALF = HEAD_DIM // 2
CHUNK = 128
ROPE_BASE = 10000.0
EPS = 1e-6
PAST_LEN = 16384
SEG_U, SEG_PG, SEG_Q, SEG_K, SEG_V, SEG_RG, SEG_GP, SEG_GR = range(8)
N_SEG = 8
ROPE_SEGS = (SEG_Q, SEG_K)
COL_Q, COL_K = range(len(ROPE_SEGS))
PLAIN_SEGS = (SEG_U, SEG_PG, SEG_V, SEG_RG, SEG_GP, SEG_GR)
COL_U, COL_PG, COL_V, COL_RG, COL_GP, COL_GR = range(len(PLAIN_SEGS))

TM = 512
TM_IN = 1024
TN_IN = 1024
PER_SEG = D_MODEL // TN_IN
VMEM_LIMIT = 56 * 1024 * 1024


def _cparams(*sem):
    return pltpu.CompilerParams(dimension_semantics=sem, vmem_limit_bytes=VMEM_LIMIT)


def _dot(a, b):
    return jnp.dot(a, b, preferred_element_type=F32)


def _dot_nt(a, b):
    return lax.dot_general(a, b, (((1,), (1,)), ((), ())), preferred_element_type=F32)


def _dot_tn(a, b):
    return lax.dot_general(a, b, (((0,), (0,)), ((), ())), preferred_element_type=F32)


def _rms(x, g):
    return x * lax.rsqrt(jnp.mean(x * x, axis=-1, keepdims=True) + EPS) * g


def _silu(x):
    return x * jax.nn.sigmoid(x)


def _x_specs(n_p, n_s):
    return [pl.BlockSpec((TM, D_MODEL), lambda i: (jnp.minimum(i, n_p - 1), 0)),
            pl.BlockSpec((TM, D_MODEL), lambda i: (jnp.clip(i - n_p, 0, n_s - 1), 0))]


def _meta_rows(meta_ref):
    return jnp.concatenate([meta_ref[...], jnp.zeros((CHUNK - N_META, D_MODEL), F32)], axis=0)


def _norm_kernel(xp_ref, xs_ref, meta_ref, g_ref, h_ref, hm_ref, *, n_p, n_s):
    i = pl.program_id(0)

    @pl.when(i < n_p)
    def _():
        h_ref[...] = _rms(xp_ref[...], g_ref[...]).astype(BF16)

    @pl.when((i >= n_p) & (i < n_p + n_s))
    def _():
        h_ref[...] = _rms(xs_ref[...], g_ref[...]).astype(BF16)

    @pl.when(i == n_p + n_s)
    def _():
        hm_ref[...] = _rms(_meta_rows(meta_ref), g_ref[...]).astype(BF16)


def _first_norm(xp, xs, meta, g, n_p, n_s):
    n = n_p + n_s
    return pl.pallas_call(
        functools.partial(_norm_kernel, n_p=n_p, n_s=n_s),
        out_shape=(jax.ShapeDtypeStruct((n * TM, D_MODEL), BF16),
                   jax.ShapeDtypeStruct((CHUNK, D_MODEL), BF16)),
        grid=(n + 1,),
        in_specs=_x_specs(n_p, n_s) + [pl.BlockSpec((N_META, D_MODEL), lambda i: (0, 0)),
                                       pl.BlockSpec((1, D_MODEL), lambda i: (0, 0))],
        out_specs=(pl.BlockSpec((TM, D_MODEL), lambda i: (jnp.minimum(i, n - 1), 0)),
                   pl.BlockSpec((CHUNK, D_MODEL), lambda i: (0, 0))),
        compiler_params=_cparams("arbitrary"),
        name="first_norm",
    )(xp, xs, meta, g)


RB = 2
_STATE = (HEADS, HEAD_DIM, HEAD_DIM)


def _head_norm_gate(o, gain, gate):
    mu = jnp.mean(o, axis=-1, keepdims=True)
    oc = o - mu
    on = oc * lax.rsqrt(jnp.mean(oc * oc, axis=-1, keepdims=True) + EPS)
    return ((on * gain) * _silu(gate)).astype(BF16)


def _sample_body(gam_ref, q_ref, k_ref, v_ref, rg_ref, gn_ref, dmask_ref, qdec_ref, kdec_ref, s_ref,
                 o_ref, sout_ref, n_new):
    pairs = [(b, h) for b in range(RB) for h in range(HEADS)]

    def stack(ref):
        return jnp.concatenate(
            [ref[b * n_new:(b + 1) * n_new, h * HEAD_DIM:(h + 1) * HEAD_DIM] for b, h in pairs], axis=0)

    k = stack(k_ref)
    qb = stack(q_ref).astype(BF16)
    v = stack(v_ref)
    vb = v.astype(BF16)
    scores = _dot_nt(qb, k.astype(BF16)) * dmask_ref[...]
    intra = _dot(scores.astype(BF16), vb)
    k_dec = (k * kdec_ref[...]).astype(BF16)
    rows = lax.broadcasted_iota(jnp.int32, v.shape, 0)
    is_even = (rows & n_new) == 0
    v_even = jnp.where(is_even, v, 0.0).astype(BF16)
    v_odd = jnp.where(is_even, 0.0, v).astype(BF16)
    grp = 2 * n_new
    inter_parts = []
    for idx, (b, h) in enumerate(pairs):
        g0 = (idx // 2) * grp
        off = (idx % 2) * n_new
        s_old = s_ref[b, h]
        inter = _dot(qb[g0:g0 + grp], s_old.astype(BF16))
        inter_parts.append(inter[off:off + n_new])
        v_sel = v_even if idx % 2 == 0 else v_odd
        sout_ref[b, h] = s_old * gam_ref[h] + _dot_tn(k_dec[g0:g0 + grp], v_sel[g0:g0 + grp])
    o = intra + jnp.concatenate(inter_parts, axis=0) * qdec_ref[...]
    gain = jnp.concatenate(
        [jnp.broadcast_to(gn_ref[:, h * HEAD_DIM:(h + 1) * HEAD_DIM], (n_new, HEAD_DIM)) for _, h in pairs],
        axis=0)
    out = _head_norm_gate(o, gain, stack(rg_ref)).astype(F32)
    seqs = []
    for b in range(RB):
        seqs.append(jnp.concatenate(
            [out[(b * HEADS + h) * n_new:(b * HEADS + h + 1) * n_new] for h in range(HEADS)], axis=1))
    o_ref[...] = jnp.concatenate(seqs, axis=0).astype(BF16)


def _rope_store(z, o_ref, cos_ref, sin_ref, scale):
    cos = cos_ref[...]
    sin = sin_ref[...]
    for hh in range(TN_IN // HEAD_DIM):
        lo = slice(hh * HEAD_DIM, hh * HEAD_DIM + HALF)
        hi = slice(hh * HEAD_DIM + HALF, (hh + 1) * HEAD_DIM)
        t1 = z[:, lo]
        t2 = z[:, hi]
        o_ref[:, lo] = (t1 * cos - t2 * sin) * scale
        o_ref[:, hi] = (t2 * cos + t1 * sin) * scale


def _k_scale(is_k):
    return jnp.where(is_k, HEAD_DIM ** -0.5, 1.0).astype(F32)


def _inproj_small_kernel(hs_ref, hm_ref, w_ref, cos_ref, sin_ref, cosm_ref, sinm_ref, zs_ref, zm_ref, wb_ref):
    s = pl.program_id(0)
    wb_ref[...] = w_ref[...].astype(BF16)
    zs_ref[...] = _dot(hs_ref[...], wb_ref[...])
    zm_ref[...] = _dot(hm_ref[...], wb_ref[...])

    @pl.when((s >= SEG_Q * PER_SEG) & (s < (SEG_K + 1) * PER_SEG))
    def _():
        scale = _k_scale(s >= SEG_K * PER_SEG)
        _rope_store(zs_ref[...], zs_ref, cos_ref, sin_ref, scale)
        _rope_store(zm_ref[...], zm_ref, cosm_ref, sinm_ref, scale)


def _in_projection_small(h, h_meta, w_in, layer, trig_s, trig_m, row0, rows):
    assert row0 % rows == 0
    const = lambda s: (0, 0)
    col = lambda s: (0, s)
    return pl.pallas_call(
        _inproj_small_kernel,
        out_shape=(jax.ShapeDtypeStruct((rows, N_SEG * D_MODEL), F32),
                   jax.ShapeDtypeStruct((CHUNK, N_SEG * D_MODEL), F32),
                   jax.ShapeDtypeStruct((D_MODEL, N_SEG * D_MODEL), BF16)),
        grid=(N_SEG * PER_SEG,),
        in_specs=[pl.BlockSpec((rows, D_MODEL), lambda s: (row0 // rows, 0)),
                  pl.BlockSpec((CHUNK, D_MODEL), const),
                  pl.BlockSpec((None, D_MODEL, TN_IN), lambda s: (layer, 0, s)),
                  pl.BlockSpec((rows, HALF), const),
                  pl.BlockSpec((rows, HALF), const),
                  pl.BlockSpec((CHUNK, HALF), const),
                  pl.BlockSpec((CHUNK, HALF), const)],
        out_specs=(pl.BlockSpec((rows, TN_IN), col),
                   pl.BlockSpec((CHUNK, TN_IN), col),
                   pl.BlockSpec((D_MODEL, TN_IN), col)),
        compiler_params=_cparams("arbitrary"),
        name="in_projection_small",
    )(h, h_meta, w_in, *trig_s, *trig_m)


N_SAMPLE_IN = 10


def _inproj_kernel(h_ref, w_ref, *rest, rope, n_new, n_groups):
    if rope:
        cos_ref, sin_ref, o_ref = rest
        z = _dot(h_ref[...], w_ref[...])
        _rope_store(z, o_ref, cos_ref, sin_ref, _k_scale(pl.program_id(0) >= PER_SEG))
        return
    if n_new is None:
        rest[-1][...] = _dot(h_ref[...], w_ref[...])
        return
    o_ref, so_ref, sout_ref = rest[-3:]
    step = pl.program_id(0) * pl.num_programs(1) + pl.program_id(1)

    @pl.when(step < n_groups)
    def _():
        o_ref[...] = _dot(h_ref[...], w_ref[...])
        _sample_body(*rest[:N_SAMPLE_IN], so_ref, sout_ref, n_new)

    @pl.when(step >= n_groups)
    def _():
        o_ref[...] = _dot(h_ref[...], w_ref[...])


def _in_projection(h, wb, rows, segs, trig=None, seq=None, sample=None):
    assert rows % TM_IN == 0
    tiles = rows // TM_IN
    rope = trig is not None
    gap_at = next((j for j in range(1, len(segs)) if segs[j] != segs[j - 1] + 1), len(segs))
    gap = segs[gap_at] - segs[gap_at - 1] - 1 if gap_at < len(segs) else 0
    assert all(segs[j] == segs[0] + j + (gap if j >= gap_at else 0) for j in range(len(segs)))

    def wmap(s, i):
        j = s // PER_SEG
        seg = segs[0] + j + jnp.where(j >= gap_at, gap, 0)
        return (0, seg * PER_SEG + s % PER_SEG)

    in_specs = [pl.BlockSpec((TM_IN, D_MODEL), lambda s, i: (i, 0)),
                pl.BlockSpec((D_MODEL, TN_IN), wmap)]
    args = [h, wb]
    out_shape = [jax.ShapeDtypeStruct((rows, len(segs) * D_MODEL), F32)]
    out_specs = [pl.BlockSpec((TM_IN, TN_IN), lambda s, i: (i, s))]
    aliases = {}
    n_new = n_groups = None
    if rope:
        assert seq % TM_IN == 0
        in_specs += [pl.BlockSpec((TM_IN, HALF), lambda s, i: (i % (seq // TM_IN), 0)) for _ in range(2)]
        args += list(trig)
    if sample is not None:
        gam, gn, (dmask, qdec, kdec), state_ret, new_state, layer, n_new, zs = sample
        n_groups = state_ret.shape[1] // RB
        assert n_groups <= len(segs) * PER_SEG * tiles, "one group of sample sequences per grid step"
        rows_g = RB * n_new
        stack_rows = RB * HEADS * n_new

        def group(s, i):
            return jnp.minimum(s * tiles + i, n_groups - 1)

        def zmap(seg):
            return lambda s, i: (group(s, i), seg)

        const = lambda s, i: (0, 0)
        sblk = (None, RB) + _STATE
        state_map = lambda s, i: (layer, group(s, i), 0, 0, 0)
        in_specs += [pl.BlockSpec(memory_space=pltpu.SMEM),
                     pl.BlockSpec((rows_g, D_MODEL), zmap(SEG_Q)),
                     pl.BlockSpec((rows_g, D_MODEL), zmap(SEG_K)),
                     pl.BlockSpec((rows_g, D_MODEL), zmap(SEG_V)),
                     pl.BlockSpec((rows_g, D_MODEL), zmap(SEG_RG)),
                     pl.BlockSpec((1, D_MODEL), const),
                     pl.BlockSpec((stack_rows, stack_rows), const),
                     pl.BlockSpec((stack_rows, HEAD_DIM), const),
                     pl.BlockSpec((stack_rows, HEAD_DIM), const),
                     pl.BlockSpec(sblk, state_map)]
        args += [gam, zs, zs, zs, zs, gn, dmask, qdec, kdec, state_ret]
        out_shape += [jax.ShapeDtypeStruct((rows + n_groups * rows_g, D_MODEL), BF16),
                      jax.ShapeDtypeStruct(state_ret.shape, F32)]
        out_specs += [pl.BlockSpec((rows_g, D_MODEL), lambda s, i: (rows // rows_g + group(s, i), 0)),
                      pl.BlockSpec(sblk, state_map)]
        if new_state is not None:
            aliases[len(args)] = 2
            in_specs.append(pl.BlockSpec(memory_space=pl.ANY))
            args.append(new_state)
    outs = pl.pallas_call(
        functools.partial(_inproj_kernel, rope=rope, n_new=n_new, n_groups=n_groups),
        out_shape=tuple(out_shape),
        grid=(len(segs) * PER_SEG, tiles),
        in_specs=in_specs,
        out_specs=tuple(out_specs),
        input_output_aliases=aliases,
        compiler_params=_cparams("arbitrary", "arbitrary"),
        name="in_projection_rope" if rope else "in_projection",
    )(*args)
    return outs if sample is not None else outs[0]


def _window_sum(ext, w, base, rows):
    s = ext
    size = 1
    while size < w:
        s = s[size:] + s[:-size]
        size *= 2
    start = base - (w - 1)
    return s[start:start + rows]


def _pool_group(g, wsum, u, inv_cnt, pw_ref, ps_ref, gate):
    sl = slice(g * POOL_GROUP, (g + 1) * POOL_GROUP)
    pooled = wsum * inv_cnt - u
    mixed = _dot(pooled.astype(BF16), pw_ref[g])
    return (mixed * ps_ref[:, sl] * _silu(gate)).astype(BF16)


def _pool_w_spec(layer):
    return pl.BlockSpec((None, N_GROUPS, POOL_GROUP, POOL_GROUP), lambda *_: (layer, 0, 0, 0))


def _pool_meta_kernel(u_ref, pg_ref, pw_ref, ps_ref, o_ref):
    avail = lax.broadcasted_iota(jnp.int32, (N_META, 1), 0).astype(F32) + 1.0
    o_ref[N_META:] = jnp.zeros((CHUNK - N_META, D_MODEL), BF16)
    for g, w in enumerate(POOL_WINDOWS):
        sl = slice(g * POOL_GROUP, (g + 1) * POOL_GROUP)
        u = u_ref[:, sl]
        wsum = _window_sum(jnp.concatenate([jnp.zeros_like(u), u], axis=0), w, N_META, N_META)
        inv_cnt = 1.0 / jnp.minimum(float(w), avail)
        o_ref[0:N_META, sl] = _pool_group(g, wsum, u, inv_cnt, pw_ref, ps_ref, pg_ref[:, sl])


def _pool_meta(zm, pool_w, pool_scale, layer):
    return pl.pallas_call(
        _pool_meta_kernel,
        out_shape=jax.ShapeDtypeStruct((CHUNK, D_MODEL), BF16),
        grid=(1,),
        in_specs=[pl.BlockSpec((N_META, D_MODEL), lambda i: (0, SEG_U)),
                  pl.BlockSpec((N_META, D_MODEL), lambda i: (0, SEG_PG)),
                  _pool_w_spec(layer),
                  pl.BlockSpec((1, D_MODEL), lambda i: (0, 0))],
        out_specs=pl.BlockSpec((CHUNK, D_MODEL), lambda i: (0, 0)),
        compiler_params=_cparams("arbitrary"),
        name="pool_meta",
    )(zm, zm, pool_w, pool_scale)


PT = 256


def _pool_prompt_kernel(prev_ref, meta_ref, u_ref, pg_ref, pw_ref, ps_ref, o_ref, hist_ref):
    t = pl.program_id(1)
    for g, w in enumerate(POOL_WINDOWS):
        sl = slice(g * POOL_GROUP, (g + 1) * POOL_GROUP)
        prev = jnp.where(t == 0, meta_ref[:, sl], prev_ref[:, sl])
        u = u_ref[:, sl]
        wsum = _window_sum(jnp.concatenate([prev, u], axis=0), w, N_META, PT)
        o_ref[:, sl] = _pool_group(g, wsum, u, 1.0 / w, pw_ref, ps_ref, pg_ref[:, sl])

    @pl.when(t == pl.num_programs(1) - 1)
    def _():
        hist_ref[0] = u_ref[PT - POOL_HIST:PT, :]


def _pool_prompt(z, zm, pool_w, pool_scale, layer, n_batch, seq, total_rows):
    tiles = seq // PT
    sub = PT // N_META

    def prev_map(b, t):
        return (jnp.maximum(b * (seq // N_META) + t * sub - 1, 0), COL_U)

    return pl.pallas_call(
        _pool_prompt_kernel,
        out_shape=(jax.ShapeDtypeStruct((total_rows, D_MODEL), BF16),
                   jax.ShapeDtypeStruct((n_batch, POOL_HIST, D_MODEL), F32)),
        grid=(n_batch, tiles),
        in_specs=[pl.BlockSpec((N_META, D_MODEL), prev_map),
                  pl.BlockSpec((N_META, D_MODEL), lambda b, t: (0, SEG_U)),
                  pl.BlockSpec((PT, D_MODEL), lambda b, t: (b * tiles + t, COL_U)),
                  pl.BlockSpec((PT, D_MODEL), lambda b, t: (b * tiles + t, COL_PG)),
                  _pool_w_spec(layer),
                  pl.BlockSpec((1, D_MODEL), lambda b, t: (0, 0))],
        out_specs=(pl.BlockSpec((PT, D_MODEL), lambda b, t: (b * tiles + t, 0)),
                   pl.BlockSpec((1, POOL_HIST, D_MODEL), lambda b, t: (b, 0, 0))),
        compiler_params=_cparams("arbitrary", "arbitrary"),
        name="pool_prompt",
    )(z, zm, z, z, pool_w, pool_scale)


LANE = 128
LANE_TILES = POOL_GROUP // LANE


def _pool_sample_kernel(hist_ref, *rest, n_seq, n_new):
    u_refs = rest[:LANE_TILES]
    pg_ref, pw_ref, ps_ref = rest[LANE_TILES:LANE_TILES + 3]
    o_ref, nh_ref, pooled_scr = rest[-3 - LANE_TILES:-LANE_TILES]
    mix_scrs = rest[-LANE_TILES:]
    g = pl.program_id(0)

    def token_rows(t):
        return pl.ds(t, n_seq, stride=n_new)

    u_t = [jnp.concatenate([r[token_rows(t), :] for r in u_refs], axis=1) for t in range(n_new)]
    ext = [hist_ref[j] for j in range(POOL_HIST)] + u_t
    for j in range(POOL_HIST):
        nh_ref[j] = ext[n_new + j]
    for k, w in enumerate(POOL_WINDOWS):
        @pl.when(g == k)
        def _(w=w):
            s = ext
            size = 1
            while size < w:
                s = [s[i + size] + s[i] for i in range(len(s) - size)]
                size *= 2
            for t in range(n_new):
                pooled_scr[t * n_seq:(t + 1) * n_seq] = s[POOL_HIST + 1 + t - w] * (1.0 / w) - u_t[t]
    mixed = _dot(pooled_scr[...].astype(BF16), pw_ref[...])
    for t in range(n_new):
        for c, scr in enumerate(mix_scrs):
            scr[token_rows(t), :] = mixed[t * n_seq:(t + 1) * n_seq, c * LANE:(c + 1) * LANE]
    mixed = jnp.concatenate([scr[...] for scr in mix_scrs], axis=1)
    o_ref[...] = (mixed * ps_ref[...] * _silu(pg_ref[...])).astype(BF16)


def _pool_sample(zs, state_pool_t, a_buf, new_hist, pool_w, pool_scale, layer, n_seq, n_new, row0):
    rows = n_seq * n_new
    assert row0 % rows == 0
    hist_blk = (None, POOL_HIST, n_seq, POOL_GROUP)
    hist_map = lambda g: (layer, 0, 0, g)
    per_seg = D_MODEL // POOL_GROUP

    def u_tile(c):
        return pl.BlockSpec((rows, LANE), lambda g: (0, (SEG_U * per_seg + g) * LANE_TILES + c))

    in_specs = [pl.BlockSpec(hist_blk, hist_map)] + [u_tile(c) for c in range(LANE_TILES)]
    in_specs += [pl.BlockSpec((rows, POOL_GROUP), lambda g: (0, SEG_PG * per_seg + g)),
                 pl.BlockSpec((None, None, POOL_GROUP, POOL_GROUP), lambda g: (layer, g, 0, 0)),
                 pl.BlockSpec((1, POOL_GROUP), lambda g: (0, g)),
                 pl.BlockSpec(memory_space=pl.ANY)]
    args = [state_pool_t] + [zs] * LANE_TILES + [zs, pool_w, pool_scale, a_buf]
    aliases = {len(args) - 1: 0}
    if new_hist is not None:
        aliases[len(args)] = 1
        in_specs.append(pl.BlockSpec(memory_space=pl.ANY))
        args.append(new_hist)
    return pl.pallas_call(
        functools.partial(_pool_sample_kernel, n_seq=n_seq, n_new=n_new),
        out_shape=(jax.ShapeDtypeStruct(a_buf.shape, BF16),
                   jax.ShapeDtypeStruct(state_pool_t.shape, F32)),
        grid=(N_GROUPS,),
        in_specs=in_specs,
        out_specs=(pl.BlockSpec((rows, POOL_GROUP), lambda g: (row0 // rows, g)),
                   pl.BlockSpec(hist_blk, hist_map)),
        scratch_shapes=[pltpu.VMEM((rows, POOL_GROUP), F32)] + [pltpu.VMEM((rows, LANE), F32)] * LANE_TILES,
        input_output_aliases=aliases,
        compiler_params=_cparams("arbitrary"),
        name="pool_sample",
    )(*args)


def _ret_kernel(dec_ref, decay_ref, q_ref, k_ref, v_ref, rg_ref, gn_ref, s0_ref, *rest, n_valid):
    o_ref, sfin_ref, s_scr = rest[-3:]
    c = pl.program_id(1)

    @pl.when(c == 0)
    def _():
        s_scr[...] = s0_ref[...]

    ridx = lax.broadcasted_iota(jnp.int32, (CHUNK, 1), 0).astype(F32)
    for h in range(HEADS):
        hs = slice(h * HEAD_DIM, (h + 1) * HEAD_DIM)
        lg = dec_ref[0, h]
        q = q_ref[:, hs].astype(BF16)
        k = k_ref[:, hs]
        v = v_ref[:, hs].astype(BF16)
        s_old = s_scr[h]
        scores = _dot_nt(q, k.astype(BF16)) * decay_ref[h]
        intra = _dot(scores.astype(BF16), v)
        inter = _dot(q, s_old.astype(BF16)) * jnp.exp((ridx + 1.0) * lg)
        k_dec = (k * jnp.exp((n_valid - 1.0 - ridx) * lg)).astype(BF16)
        s_scr[h] = s_old * dec_ref[1, h] + _dot_tn(k_dec, v)
        o_ref[:, hs] = _head_norm_gate(intra + inter, gn_ref[:, hs], rg_ref[:, hs])

    @pl.when(c == pl.num_programs(1) - 1)
    def _():
        sfin_ref[...] = s_scr[...]


def _retention(srcs, lg, gn, decay, s0, b_buf, fin, fin_shape, fin_index, n_batch, n_chunks, n_valid):
    dec = jnp.stack([lg, jnp.exp(n_valid * lg)])

    def zmap(col):
        return lambda b, c: (b * n_chunks + c, col)

    in_specs = [pl.BlockSpec(memory_space=pltpu.SMEM),
                pl.BlockSpec((HEADS, CHUNK, CHUNK), lambda b, c: (0, 0, 0))]
    in_specs += [pl.BlockSpec((CHUNK, D_MODEL), zmap(col)) for _, col in srcs]
    in_specs += [pl.BlockSpec((1, D_MODEL), lambda b, c: (0, 0)),
                 pl.BlockSpec(_STATE, lambda b, c: (0, 0, 0))]
    args = [dec, decay] + [a for a, _ in srcs] + [gn, s0]
    aliases = {}
    for out_idx, buf in enumerate((b_buf, fin)):
        if buf is not None:
            aliases[len(args)] = out_idx
            in_specs.append(pl.BlockSpec(memory_space=pl.ANY))
            args.append(buf)
    fin_blk = (None,) * (len(fin_shape) - 3) + _STATE
    b_rows = n_batch * n_chunks * CHUNK if b_buf is None else b_buf.shape[0]
    return pl.pallas_call(
        functools.partial(_ret_kernel, n_valid=n_valid),
        out_shape=(jax.ShapeDtypeStruct((b_rows, D_MODEL), BF16),
                   jax.ShapeDtypeStruct(fin_shape, F32)),
        grid=(n_batch, n_chunks),
        in_specs=in_specs,
        out_specs=(pl.BlockSpec((CHUNK, D_MODEL), zmap(0)),
                   pl.BlockSpec(fin_blk, lambda b, c: fin_index(b) + (0, 0, 0))),
        scratch_shapes=[pltpu.VMEM(_STATE, F32)],
        input_output_aliases=aliases,
        compiler_params=_cparams("arbitrary", "arbitrary"),
        name="retention",
    )(*args)


TN_MERGE = 1024


def _merge_kernel(a_ref, b_ref, gpp_ref, grp_ref, gps_ref, grs_ref,
                  am_ref, bm_ref, gpm_ref, grm_ref, wp_ref, wr_ref, o_ref, om_ref, *, n_p, n_s):
    i = pl.program_id(1)

    def tile(a, b, gp, gr, out):
        pool_branch = _dot(a[...], wp_ref[...])
        ret_branch = _dot(b[...], wr_ref[...])
        out[...] = (jax.nn.sigmoid(gp[...]) * pool_branch + jax.nn.sigmoid(gr[...]) * ret_branch).astype(BF16)

    @pl.when(i < n_p)
    def _():
        tile(a_ref, b_ref, gpp_ref, grp_ref, o_ref)

    @pl.when((i >= n_p) & (i < n_p + n_s))
    def _():
        tile(a_ref, b_ref, gps_ref, grs_ref, o_ref)

    @pl.when(i == n_p + n_s)
    def _():
        tile(am_ref, bm_ref, gpm_ref, grm_ref, om_ref)


def _merge(a_in, b_in, z, zs, a_meta, b_meta, zm, proj_pool, proj_ret, layer):
    n_p = z.shape[0] // TM
    n_s = zs.shape[0] // TM
    n_main = n_p + n_s
    assert a_in.shape[0] == n_main * TM and b_in.shape[0] == n_main * TM
    nt = D_MODEL // TN_MERGE

    def p_map(col):
        return lambda j, i: (jnp.minimum(i, n_p - 1), col(j))

    def s_map(col):
        return lambda j, i: (jnp.clip(i - n_p, 0, n_s - 1), col(j))

    row = pl.BlockSpec((TM, D_MODEL), lambda j, i: (jnp.minimum(i, n_main - 1), 0))
    gate = lambda ref_map, seg: pl.BlockSpec((TM, TN_MERGE), ref_map(lambda j: seg * nt + j))
    const = lambda j, i: (0, 0)
    return pl.pallas_call(
        functools.partial(_merge_kernel, n_p=n_p, n_s=n_s),
        out_shape=(jax.ShapeDtypeStruct((n_main * TM, D_MODEL), BF16),
                   jax.ShapeDtypeStruct((CHUNK, D_MODEL), BF16)),
        grid=(nt, n_main + 1),
        in_specs=[row, row,
                  gate(p_map, COL_GP), gate(p_map, COL_GR), gate(s_map, SEG_GP), gate(s_map, SEG_GR),
                  pl.BlockSpec((CHUNK, D_MODEL), const),
                  pl.BlockSpec((CHUNK, D_MODEL), const),
                  pl.BlockSpec((CHUNK, TN_MERGE), lambda j, i: (0, SEG_GP * nt + j)),
                  pl.BlockSpec((CHUNK, TN_MERGE), lambda j, i: (0, SEG_GR * nt + j)),
                  pl.BlockSpec((None, D_MODEL, TN_MERGE), lambda j, i: (layer, 0, j)),
                  pl.BlockSpec((None, D_MODEL, TN_MERGE), lambda j, i: (layer, 0, j))],
        out_specs=(pl.BlockSpec((TM, TN_MERGE), lambda j, i: (jnp.minimum(i, n_main - 1), j)),
                   pl.BlockSpec((CHUNK, TN_MERGE), lambda j, i: (0, j))),
        compiler_params=_cparams("arbitrary", "arbitrary"),
        name="merge_branches",
    )(a_in, b_in, z, z, zs, zs, a_meta, b_meta, zm, zm, proj_pool, proj_ret)


def _out_kernel(m_ref, mm_ref, w_ref, g_ref, *rest, first, last, n_p, n_s):
    rest = list(rest)
    x_refs = [rest.pop(0) for _ in range(3 if first else 2)]
    i = pl.program_id(0)

    def emit(x, merged, outs):
        xn = x + _dot(merged[...], w_ref[...])
        y = _rms(xn, g_ref[...])
        if last:
            outs[0][...] = y
        else:
            outs[0][...] = xn
            outs[1][...] = y.astype(BF16)

    main_outs = [rest[0:1], rest[1:2]] if last else [rest[0:2], rest[0:2]]

    @pl.when(i < n_p)
    def _():
        emit(x_refs[0][...], m_ref, main_outs[0])

    @pl.when((i >= n_p) & (i < n_p + n_s))
    def _():
        emit(x_refs[1 if first else 0][...], m_ref, main_outs[1])

    if not last:
        @pl.when(i == n_p + n_s)
        def _():
            emit(_meta_rows(x_refs[2]) if first else x_refs[1][...], mm_ref, rest[2:4])


def _out_projection(merged, merged_meta, w_out, layer, g_next, x_parts, first, last, n_p, n_s):
    n = n_p + n_s
    main_spec = pl.BlockSpec((TM, D_MODEL), lambda i: (jnp.minimum(i, n - 1), 0))
    meta_spec = pl.BlockSpec((CHUNK, D_MODEL), lambda i: (0, 0))
    in_specs = [main_spec, meta_spec,
                pl.BlockSpec((None, D_MODEL, D_MODEL), lambda i: (layer, 0, 0)),
                pl.BlockSpec((1, D_MODEL), lambda i: (0, 0))]
    if first:
        in_specs += _x_specs(n_p, n_s) + [pl.BlockSpec((N_META, D_MODEL), lambda i: (0, 0))]
    else:
        in_specs += [main_spec, meta_spec]
    if last:
        out_shape = (jax.ShapeDtypeStruct((n_p * TM, D_MODEL), F32),
                     jax.ShapeDtypeStruct((n_s * TM, D_MODEL), F32))
        out_specs = tuple(_x_specs(n_p, n_s))
    else:
        out_shape = (jax.ShapeDtypeStruct((n * TM, D_MODEL), F32), jax.ShapeDtypeStruct((n * TM, D_MODEL), BF16),
                     jax.ShapeDtypeStruct((CHUNK, D_MODEL), F32), jax.ShapeDtypeStruct((CHUNK, D_MODEL), BF16))
        out_specs = (main_spec, main_spec, meta_spec, meta_spec)
    return pl.pallas_call(
        functools.partial(_out_kernel, first=first, last=last, n_p=n_p, n_s=n_s),
        out_shape=out_shape,
        grid=(n if last else n + 1,),
        in_specs=in_specs,
        out_specs=out_specs,
        compiler_params=_cparams("arbitrary"),
        name="out_projection",
    )(merged, merged_meta, w_out, g_next, *x_parts)


def _log_decay():
    return jnp.log1p(-jnp.exp2(-5.0 - jnp.arange(HEADS, dtype=F32)))


def _rope_tables(pos):
    inv_freq = ROPE_BASE ** (-jnp.arange(HALF, dtype=F32) / HALF)
    ang = pos[:, None] * inv_freq[None, :]
    return jnp.cos(ang), jnp.sin(ang)


def _chunk_decay(lg):
    idx = jnp.arange(CHUNK, dtype=F32)
    diff = idx[:, None] - idx[None, :]
    causal = diff >= 0
    return jnp.where(causal[None], jnp.exp(jnp.where(causal, diff, 0.0)[None] * lg[:, None, None]), 0.0)


def _sample_decay_tables(lg, n_new):
    r = jnp.arange(RB * HEADS * n_new)
    head = (r // n_new) % HEADS
    tok = (r % n_new).astype(F32)
    lg_r = lg[head]
    same = (r[:, None] // n_new) == (r[None, :] // n_new)
    diff = tok[:, None] - tok[None, :]
    keep = same & (diff >= 0)
    dmask = jnp.where(keep, jnp.exp(jnp.where(keep, diff, 0.0) * lg_r[:, None]), 0.0)
    qdec = jnp.broadcast_to(jnp.exp((tok + 1.0) * lg_r)[:, None], (r.shape[0], HEAD_DIM))
    kdec = jnp.broadcast_to(jnp.exp((n_new - 1.0 - tok) * lg_r)[:, None], (r.shape[0], HEAD_DIM))
    return dmask, qdec, kdec


def kernel(x_prompt, x_sample, state_pool, state_ret, meta_tokens, norm_gain, w_in, pool_w, pool_scale,
           ret_gn_gain, proj_pool, proj_ret, w_out, final_norm):
    n_batch, seq = x_prompt.shape[:2]
    n_seq, n_new = x_sample.shape[:2]
    depth = norm_gain.shape[0]
    rows_prompt = n_batch * seq
    rows_sample = n_seq * n_new
    assert seq % PT == 0 and seq % CHUNK == 0 and n_seq % RB == 0
    assert n_new == 8, "sample tokens of one sequence must fill one f32 sublane tile"
    assert rows_prompt % TM == 0 and rows_sample % TM == 0
    n_p = rows_prompt // TM
    n_s = rows_sample // TM

    xp = x_prompt.reshape(rows_prompt, D_MODEL).astype(F32)
    xs = x_sample.reshape(rows_sample, D_MODEL).astype(F32)
    meta = meta_tokens.astype(F32)
    state_pool = state_pool.astype(F32)
    state_ret = state_ret.astype(F32)
    w_in = w_in.astype(F32)

    trig_p = _rope_tables(N_META + jnp.arange(seq, dtype=F32))
    trig_s = tuple(jnp.tile(t, (n_seq, 1)) for t in _rope_tables(PAST_LEN + jnp.arange(n_new, dtype=F32)))
    trig_m = _rope_tables(jnp.where(jnp.arange(CHUNK) < N_META, jnp.arange(CHUNK, dtype=F32), 0.0))
    lg = _log_decay()
    decay = _chunk_decay(lg)
    gam_new = jnp.exp(n_new * lg)
    tables = _sample_decay_tables(lg, n_new)
    zero_state = jnp.zeros(_STATE, F32)
    row = lambda a: a.reshape(1, D_MODEL).astype(F32)

    h, h_meta = _first_norm(xp, xs, meta, row(norm_gain[0]), n_p, n_s)
    x_parts = (xp, xs, meta)
    pool_p = []
    ret_p = pool_s = ret_s = None
    pw = pool_w.astype(BF16)
    proj_pool = proj_pool.astype(BF16)
    proj_ret = proj_ret.astype(BF16)
    w_out = w_out.astype(BF16)
    state_pool_t = jnp.transpose(state_pool, (0, 2, 1, 3))
    for l in range(depth):
        ps = row(pool_scale[l])
        gn = row(ret_gn_gain[l])

        zs, zm, wb = _in_projection_small(h, h_meta, w_in, l, trig_s, trig_m, rows_prompt, rows_sample)
        zr = _in_projection(h, wb, rows_prompt, ROPE_SEGS, trig=trig_p, seq=seq)
        z, b_in, ret_s = _in_projection(
            h, wb, rows_prompt, PLAIN_SEGS, sample=(gam_new, gn, tables, state_ret, ret_s, l, n_new, zs))

        a_meta = _pool_meta(zm, pw, ps, l)
        a_in, hist_p = _pool_prompt(z, zm, pw, ps, l, n_batch, seq, rows_prompt + rows_sample)
        a_in, pool_s = _pool_sample(zs, state_pool_t, a_in, pool_s, pw, ps, l, n_seq, n_new, rows_prompt)
        pool_p.append(hist_p)

        b_meta, s_meta = _retention(((zm, SEG_Q), (zm, SEG_K), (zm, SEG_V), (zm, SEG_RG)), lg, gn, decay,
                                    zero_state, None, None, _STATE, lambda b: (), 1, 1, N_META)
        b_in, ret_p = _retention(((zr, COL_Q), (zr, COL_K), (z, COL_V), (z, COL_RG)), lg, gn, decay,
                                 s_meta, b_in, ret_p, (depth, n_batch) + _STATE, lambda b, l=l: (l, b),
                                 n_batch, seq // CHUNK, CHUNK)

        merged, merged_meta = _merge(a_in, b_in, z, zs, a_meta, b_meta, zm, proj_pool, proj_ret, l)
        last = l == depth - 1
        g_next = row(final_norm if last else norm_gain[l + 1])
        outs = _out_projection(merged, merged_meta, w_out, l, g_next, x_parts, l == 0, last, n_p, n_s)
        if last:
            y_prompt, y_sample = outs
        else:
            x_new, h, x_meta, h_meta = outs
            x_parts = (x_new, x_meta)

    return (y_prompt.reshape(n_batch, seq, D_MODEL), y_sample.reshape(n_seq, n_new, D_MODEL),
            jnp.stack(pool_p), ret_p, jnp.transpose(pool_s, (0, 2, 1, 3)), ret_s)
```

```python
import functools

import jax
import jax.numpy as jnp
from jax import lax
from jax.experimental import pallas as pl
from jax.experimental.pallas import tpu as pltpu

F32 = jnp.float32
BF16 = jnp.bfloat16

D_MODEL = 2048
N_META = 16
POOL_WINDOWS = (2, 4, 8, 16)
N_GROUPS = len(POOL_WINDOWS)
POOL_GROUP = D_MODEL // N_GROUPS
POOL_HIST = max(POOL_WINDOWS) - 1
HEADS = 8
HEAD_DIM = D_MODEL // HEADS
HALF = HEAD_DIM // 2
CHUNK = 128
ROPE_BASE = 10000.0
EPS = 1e-6
PAST_LEN = 16384
SEG_U, SEG_PG, SEG_Q, SEG_K, SEG_V, SEG_RG, SEG_GP, SEG_GR = range(8)
N_SEG = 8
ROPE_SEGS = (SEG_Q, SEG_K)
COL_Q, COL_K = range(len(ROPE_SEGS))
PLAIN_SEGS = (SEG_V, SEG_RG, SEG_GP, SEG_GR)
COL_V, COL_RG, COL_GP, COL_GR = range(len(PLAIN_SEGS))

TM = 512
TM_IN = 1024
TN_IN = 1024
PER_SEG = D_MODEL // TN_IN
VMEM_LIMIT = 56 * 1024 * 1024


def _cparams(*sem):
    return pltpu.CompilerParams(dimension_semantics=sem, vmem_limit_bytes=VMEM_LIMIT)


def _dot(a, b):
    return jnp.dot(a, b, preferred_element_type=F32)


def _dot_nt(a, b):
    return lax.dot_general(a, b, (((1,), (1,)), ((), ())), preferred_element_type=F32)


def _dot_tn(a, b):
    return lax.dot_general(a, b, (((0,), (0,)), ((), ())), preferred_element_type=F32)


def _rms(x, g):
    return x * lax.rsqrt(jnp.mean(x * x, axis=-1, keepdims=True) + EPS) * g


def _silu(x):
    return x * jax.nn.sigmoid(x)


def _x_specs(n_p, n_s):
    return [pl.BlockSpec((TM, D_MODEL), lambda i: (jnp.minimum(i, n_p - 1), 0)),
            pl.BlockSpec((TM, D_MODEL), lambda i: (jnp.clip(i - n_p, 0, n_s - 1), 0))]


def _meta_rows(meta_ref):
    return jnp.concatenate([meta_ref[...], jnp.zeros((CHUNK - N_META, D_MODEL), F32)], axis=0)


def _norm_kernel(xp_ref, xs_ref, meta_ref, g_ref, h_ref, hm_ref, *, n_p, n_s):
    i = pl.program_id(0)

    @pl.when(i < n_p)
    def _():
        h_ref[...] = _rms(xp_ref[...], g_ref[...]).astype(BF16)

    @pl.when((i >= n_p) & (i < n_p + n_s))
    def _():
        h_ref[...] = _rms(xs_ref[...], g_ref[...]).astype(BF16)

    @pl.when(i == n_p + n_s)
    def _():
        hm_ref[...] = _rms(_meta_rows(meta_ref), g_ref[...]).astype(BF16)


def _first_norm(xp, xs, meta, g, n_p, n_s):
    n = n_p + n_s
    return pl.pallas_call(
        functools.partial(_norm_kernel, n_p=n_p, n_s=n_s),
        out_shape=(jax.ShapeDtypeStruct((n * TM, D_MODEL), BF16),
                   jax.ShapeDtypeStruct((CHUNK, D_MODEL), BF16)),
        grid=(n + 1,),
        in_specs=_x_specs(n_p, n_s) + [pl.BlockSpec((N_META, D_MODEL), lambda i: (0, 0)),
                                       pl.BlockSpec((1, D_MODEL), lambda i: (0, 0))],
        out_specs=(pl.BlockSpec((TM, D_MODEL), lambda i: (jnp.minimum(i, n - 1), 0)),
                   pl.BlockSpec((CHUNK, D_MODEL), lambda i: (0, 0))),
        compiler_params=_cparams("arbitrary"),
        name="first_norm",
    )(xp, xs, meta, g)


RB = 2
_STATE = (HEADS, HEAD_DIM, HEAD_DIM)


def _head_norm_gate(o, gain, gate):
    mu = jnp.mean(o, axis=-1, keepdims=True)
    oc = o - mu
    on = oc * lax.rsqrt(jnp.mean(oc * oc, axis=-1, keepdims=True) + EPS)
    return ((on * gain) * _silu(gate)).astype(BF16)


def _sample_body(gam_ref, q_ref, k_ref, v_ref, rg_ref, gn_ref, dmask_ref, qdec_ref, kdec_ref, s_ref,
                 o_ref, sout_ref, n_new):
    pairs = [(b, h) for b in range(RB) for h in range(HEADS)]

    def stack(ref):
        return jnp.concatenate(
            [ref[b * n_new:(b + 1) * n_new, h * HEAD_DIM:(h + 1) * HEAD_DIM] for b, h in pairs], axis=0)

    k = stack(k_ref)
    qb = stack(q_ref).astype(BF16)
    v = stack(v_ref)
    vb = v.astype(BF16)
    scores = _dot_nt(qb, k.astype(BF16)) * dmask_ref[...]
    intra = _dot(scores.astype(BF16), vb)
    k_dec = (k * kdec_ref[...]).astype(BF16)
    rows = lax.broadcasted_iota(jnp.int32, v.shape, 0)
    is_even = (rows & n_new) == 0
    v_even = jnp.where(is_even, v, 0.0).astype(BF16)
    v_odd = jnp.where(is_even, 0.0, v).astype(BF16)
    grp = 2 * n_new
    inter_parts = []
    for idx, (b, h) in enumerate(pairs):
        g0 = (idx // 2) * grp
        off = (idx % 2) * n_new
        s_old = s_ref[b, h]
        inter = _dot(qb[g0:g0 + grp], s_old.astype(BF16))
        inter_parts.append(inter[off:off + n_new])
        v_sel = v_even if idx % 2 == 0 else v_odd
        sout_ref[b, h] = s_old * gam_ref[h] + _dot_tn(k_dec[g0:g0 + grp], v_sel[g0:g0 + grp])
    o = intra + jnp.concatenate(inter_parts, axis=0) * qdec_ref[...]
    gain = jnp.concatenate(
        [jnp.broadcast_to(gn_ref[:, h * HEAD_DIM:(h + 1) * HEAD_DIM], (n_new, HEAD_DIM)) for _, h in pairs],
        axis=0)
    out = _head_norm_gate(o, gain, stack(rg_ref)).astype(F32)
    seqs = []
    for b in range(RB):
        seqs.append(jnp.concatenate(
            [out[(b * HEADS + h) * n_new:(b * HEADS + h + 1) * n_new] for h in range(HEADS)], axis=1))
    o_ref[...] = jnp.concatenate(seqs, axis=0).astype(BF16)


def _rope_store(z, o_ref, cos_ref, sin_ref, scale):
    cos = cos_ref[...]
    sin = sin_ref[...]
    for hh in range(TN_IN // HEAD_DIM):
        lo = slice(hh * HEAD_DIM, hh * HEAD_DIM + HALF)
        hi = slice(hh * HEAD_DIM + HALF, (hh + 1) * HEAD_DIM)
        t1 = z[:, lo]
        t2 = z[:, hi]
        o_ref[:, lo] = (t1 * cos - t2 * sin) * scale
        o_ref[:, hi] = (t2 * cos + t1 * sin) * scale


def _k_scale(is_k):
    return jnp.where(is_k, HEAD_DIM ** -0.5, 1.0).astype(F32)


def _inproj_small_kernel(hs_ref, hm_ref, w_ref, cos_ref, sin_ref, cosm_ref, sinm_ref, zs_ref, zm_ref, wb_ref):
    s = pl.program_id(0)
    wb_ref[...] = w_ref[...].astype(BF16)
    zs_ref[...] = _dot(hs_ref[...], wb_ref[...])
    zm_ref[...] = _dot(hm_ref[...], wb_ref[...])

    @pl.when((s >= SEG_Q * PER_SEG) & (s < (SEG_K + 1) * PER_SEG))
    def _():
        scale = _k_scale(s >= SEG_K * PER_SEG)
        _rope_store(zs_ref[...], zs_ref, cos_ref, sin_ref, scale)
        _rope_store(zm_ref[...], zm_ref, cosm_ref, sinm_ref, scale)


def _in_projection_small(h, h_meta, w_in, layer, trig_s, trig_m, row0, rows):
    assert row0 % rows == 0
    const = lambda s: (0, 0)
    col = lambda s: (0, s)
    return pl.pallas_call(
        _inproj_small_kernel,
        out_shape=(jax.ShapeDtypeStruct((rows, N_SEG * D_MODEL), F32),
                   jax.ShapeDtypeStruct((CHUNK, N_SEG * D_MODEL), F32),
                   jax.ShapeDtypeStruct((D_MODEL, N_SEG * D_MODEL), BF16)),
        grid=(N_SEG * PER_SEG,),
        in_specs=[pl.BlockSpec((rows, D_MODEL), lambda s: (row0 // rows, 0)),
                  pl.BlockSpec((CHUNK, D_MODEL), const),
                  pl.BlockSpec((None, D_MODEL, TN_IN), lambda s: (layer, 0, s)),
                  pl.BlockSpec((rows, HALF), const),
                  pl.BlockSpec((rows, HALF), const),
                  pl.BlockSpec((CHUNK, HALF), const),
                  pl.BlockSpec((CHUNK, HALF), const)],
        out_specs=(pl.BlockSpec((rows, TN_IN), col),
                   pl.BlockSpec((CHUNK, TN_IN), col),
                   pl.BlockSpec((D_MODEL, TN_IN), col)),
        compiler_params=_cparams("arbitrary"),
        name="in_projection_small",
    )(h, h_meta, w_in, *trig_s, *trig_m)


N_SAMPLE_IN = 10


def _inproj_kernel(h_ref, w_ref, *rest, rope, n_new, n_groups):
    if rope:
        cos_ref, sin_ref, o_ref = rest
        z = _dot(h_ref[...], w_ref[...])
        _rope_store(z, o_ref, cos_ref, sin_ref, _k_scale(pl.program_id(0) >= PER_SEG))
        return
    if n_new is None:
        rest[-1][...] = _dot(h_ref[...], w_ref[...])
        return
    o_ref, so_ref, sout_ref = rest[-3:]
    step = pl.program_id(0) * pl.num_programs(1) + pl.program_id(1)

    @pl.when(step < n_groups)
    def _():
        o_ref[...] = _dot(h_ref[...], w_ref[...])
        _sample_body(*rest[:N_SAMPLE_IN], so_ref, sout_ref, n_new)

    @pl.when(step >= n_groups)
    def _():
        o_ref[...] = _dot(h_ref[...], w_ref[...])


def _in_projection(h, wb, rows, segs, trig=None, seq=None, sample=None):
    assert rows % TM_IN == 0
    tiles = rows // TM_IN
    rope = trig is not None
    gap_at = next((j for j in range(1, len(segs)) if segs[j] != segs[j - 1] + 1), len(segs))
    gap = segs[gap_at] - segs[gap_at - 1] - 1 if gap_at < len(segs) else 0
    assert all(segs[j] == segs[0] + j + (gap if j >= gap_at else 0) for j in range(len(segs)))

    def wmap(s, i):
        j = s // PER_SEG
        seg = segs[0] + j + jnp.where(j >= gap_at, gap, 0)
        return (0, seg * PER_SEG + s % PER_SEG)

    in_specs = [pl.BlockSpec((TM_IN, D_MODEL), lambda s, i: (i, 0)),
                pl.BlockSpec((D_MODEL, TN_IN), wmap)]
    args = [h, wb]
    out_shape = [jax.ShapeDtypeStruct((rows, len(segs) * D_MODEL), F32)]
    out_specs = [pl.BlockSpec((TM_IN, TN_IN), lambda s, i: (i, s))]
    aliases = {}
    n_new = n_groups = None
    if rope:
        assert seq % TM_IN == 0
        in_specs += [pl.BlockSpec((TM_IN, HALF), lambda s, i: (i % (seq // TM_IN), 0)) for _ in range(2)]
        args += list(trig)
    if sample is not None:
        gam, gn, (dmask, qdec, kdec), state_ret, new_state, layer, n_new, zs = sample
        n_groups = state_ret.shape[1] // RB
        assert n_groups <= len(segs) * PER_SEG * tiles, "one group of sample sequences per grid step"
        rows_g = RB * n_new
        stack_rows = RB * HEADS * n_new

        def group(s, i):
            return jnp.minimum(s * tiles + i, n_groups - 1)

        def zmap(seg):
            return lambda s, i: (group(s, i), seg)

        const = lambda s, i: (0, 0)
        sblk = (None, RB) + _STATE
        state_map = lambda s, i: (layer, group(s, i), 0, 0, 0)
        in_specs += [pl.BlockSpec(memory_space=pltpu.SMEM),
                     pl.BlockSpec((rows_g, D_MODEL), zmap(SEG_Q)),
                     pl.BlockSpec((rows_g, D_MODEL), zmap(SEG_K)),
                     pl.BlockSpec((rows_g, D_MODEL), zmap(SEG_V)),
                     pl.BlockSpec((rows_g, D_MODEL), zmap(SEG_RG)),
                     pl.BlockSpec((1, D_MODEL), const),
                     pl.BlockSpec((stack_rows, stack_rows), const),
                     pl.BlockSpec((stack_rows, HEAD_DIM), const),
                     pl.BlockSpec((stack_rows, HEAD_DIM), const),
                     pl.BlockSpec(sblk, state_map)]
        args += [gam, zs, zs, zs, zs, gn, dmask, qdec, kdec, state_ret]
        out_shape += [jax.ShapeDtypeStruct((rows + n_groups * rows_g, D_MODEL), BF16),
                      jax.ShapeDtypeStruct(state_ret.shape, F32)]
        out_specs += [pl.BlockSpec((rows_g, D_MODEL), lambda s, i: (rows // rows_g + group(s, i), 0)),
                      pl.BlockSpec(sblk, state_map)]
        if new_state is not None:
            aliases[len(args)] = 2
            in_specs.append(pl.BlockSpec(memory_space=pl.ANY))
            args.append(new_state)
    outs = pl.pallas_call(
        functools.partial(_inproj_kernel, rope=rope, n_new=n_new, n_groups=n_groups),
        out_shape=tuple(out_shape),
        grid=(len(segs) * PER_SEG, tiles),
        in_specs=in_specs,
        out_specs=tuple(out_specs),
        input_output_aliases=aliases,
        compiler_params=_cparams("arbitrary", "arbitrary"),
        name="in_projection_rope" if rope else "in_projection",
    )(*args)
    return outs if sample is not None else outs[0]


def _window_sum(ext, w, base, rows):
    s = ext
    size = 1
    while size < w:
        s = s[size:] + s[:-size]
        size *= 2
    start = base - (w - 1)
    return s[start:start + rows]


def _pool_group(g, wsum, u, inv_cnt, pw_ref, ps_ref, gate):
    sl = slice(g * POOL_GROUP, (g + 1) * POOL_GROUP)
    pooled = wsum * inv_cnt - u
    mixed = _dot(pooled.astype(BF16), pw_ref[g])
    return (mixed * ps_ref[:, sl] * _silu(gate)).astype(BF16)


def _pool_w_spec(layer):
    return pl.BlockSpec((None, N_GROUPS, POOL_GROUP, POOL_GROUP), lambda *_: (layer, 0, 0, 0))


def _pool_meta_kernel(u_ref, pg_ref, pw_ref, ps_ref, o_ref):
    avail = lax.broadcasted_iota(jnp.int32, (N_META, 1), 0).astype(F32) + 1.0
    o_ref[N_META:] = jnp.zeros((CHUNK - N_META, D_MODEL), BF16)
    for g, w in enumerate(POOL_WINDOWS):
        sl = slice(g * POOL_GROUP, (g + 1) * POOL_GROUP)
        u = u_ref[:, sl]
        wsum = _window_sum(jnp.concatenate([jnp.zeros_like(u), u], axis=0), w, N_META, N_META)
        inv_cnt = 1.0 / jnp.minimum(float(w), avail)
        o_ref[0:N_META, sl] = _pool_group(g, wsum, u, inv_cnt, pw_ref, ps_ref, pg_ref[:, sl])


def _pool_meta(zm, pool_w, pool_scale, layer):
    return pl.pallas_call(
        _pool_meta_kernel,
        out_shape=jax.ShapeDtypeStruct((CHUNK, D_MODEL), BF16),
        grid=(1,),
        in_specs=[pl.BlockSpec((N_META, D_MODEL), lambda i: (0, SEG_U)),
                  pl.BlockSpec((N_META, D_MODEL), lambda i: (0, SEG_PG)),
                  _pool_w_spec(layer),
                  pl.BlockSpec((1, D_MODEL), lambda i: (0, 0))],
        out_specs=pl.BlockSpec((CHUNK, D_MODEL), lambda i: (0, 0)),
        compiler_params=_cparams("arbitrary"),
        name="pool_meta",
    )(zm, zm, pool_w, pool_scale)


GROUPS_PER_BLOCK = TN_IN // POOL_GROUP


def _proj_pool_kernel(h_ref, wu_ref, wpg_ref, meta_ref, pw_ref, ps_ref, o_ref, hist_ref, carry, *,
                      tiles_per_seq):
    c = pl.program_id(0)
    i = pl.program_id(1)
    first = (i % tiles_per_seq) == 0
    last = (i % tiles_per_seq) == tiles_per_seq - 1
    for cc in range(D_MODEL // TN_IN):
        @pl.when(c == cc)
        def _(cc=cc):
            u = _dot(h_ref[...], wu_ref[...])
            pg = _dot(h_ref[...], wpg_ref[...])
            prev = jnp.where(first, meta_ref[...], carry[...])
            for gg in range(GROUPS_PER_BLOCK):
                w = POOL_WINDOWS[cc * GROUPS_PER_BLOCK + gg]
                sl = slice(gg * POOL_GROUP, (gg + 1) * POOL_GROUP)
                wsum = _window_sum(jnp.concatenate([prev[:, sl], u[:, sl]], axis=0), w, N_META, TM_IN)
                pooled = wsum * (1.0 / w) - u[:, sl]
                mixed = _dot(pooled.astype(BF16), pw_ref[gg])
                o_ref[:, sl] = (mixed * ps_ref[:, sl] * _silu(pg[:, sl])).astype(BF16)
            carry[...] = u[TM_IN - N_META:]

            @pl.when(last)
            def _():
                hist_ref[0] = u[TM_IN - POOL_HIST:]


def _proj_pool(h, wb, zm, pool_w, pool_scale, layer, n_batch, seq, total_rows):
    assert seq % TM_IN == 0 and TN_IN % POOL_GROUP == 0
    tiles_per_seq = seq // TM_IN
    return pl.pallas_call(
        functools.partial(_proj_pool_kernel, tiles_per_seq=tiles_per_seq),
        out_shape=(jax.ShapeDtypeStruct((total_rows, D_MODEL), BF16),
                   jax.ShapeDtypeStruct((n_batch, POOL_HIST, D_MODEL), F32)),
        grid=(D_MODEL // TN_IN, n_batch * tiles_per_seq),
        in_specs=[pl.BlockSpec((TM_IN, D_MODEL), lambda c, i: (i, 0)),
                  pl.BlockSpec((D_MODEL, TN_IN), lambda c, i: (0, SEG_U * PER_SEG + c)),
                  pl.BlockSpec((D_MODEL, TN_IN), lambda c, i: (0, SEG_PG * PER_SEG + c)),
                  pl.BlockSpec((N_META, TN_IN), lambda c, i: (0, SEG_U * PER_SEG + c)),
                  pl.BlockSpec((None, GROUPS_PER_BLOCK, POOL_GROUP, POOL_GROUP), lambda c, i: (layer, c, 0, 0)),
                  pl.BlockSpec((1, TN_IN), lambda c, i: (0, c))],
        out_specs=(pl.BlockSpec((TM_IN, TN_IN), lambda c, i: (i, c)),
                   pl.BlockSpec((1, POOL_HIST, TN_IN), lambda c, i: (i // tiles_per_seq, 0, c))),
        scratch_shapes=[pltpu.VMEM((N_META, TN_IN), F32)],
        compiler_params=_cparams("arbitrary", "arbitrary"),
        name="projection_pool",
    )(h, wb, wb, zm, pool_w, pool_scale)


LANE = 128
LANE_TILES = POOL_GROUP // LANE


def _pool_sample_kernel(hist_ref, *rest, n_seq, n_new):
    u_refs = rest[:LANE_TILES]
    pg_ref, pw_ref, ps_ref = rest[LANE_TILES:LANE_TILES + 3]
    o_ref, nh_ref, pooled_scr = rest[-3 - LANE_TILES:-LANE_TILES]
    mix_scrs = rest[-LANE_TILES:]
    g = pl.program_id(0)

    def token_rows(t):
        return pl.ds(t, n_seq, stride=n_new)

    u_t = [jnp.concatenate([r[token_rows(t), :] for r in u_refs], axis=1) for t in range(n_new)]
    ext = [hist_ref[j] for j in range(POOL_HIST)] + u_t
    for j in range(POOL_HIST):
        nh_ref[j] = ext[n_new + j]
    for k, w in enumerate(POOL_WINDOWS):
        @pl.when(g == k)
        def _(w=w):
            s = ext
            size = 1
            while size < w:
                s = [s[i + size] + s[i] for i in range(len(s) - size)]
                size *= 2
            for t in range(n_new):
                pooled_scr[t * n_seq:(t + 1) * n_seq] = s[POOL_HIST + 1 + t - w] * (1.0 / w) - u_t[t]
    mixed = _dot(pooled_scr[...].astype(BF16), pw_ref[...])
    for t in range(n_new):
        for c, scr in enumerate(mix_scrs):
            scr[token_rows(t), :] = mixed[t * n_seq:(t + 1) * n_seq, c * LANE:(c + 1) * LANE]
    mixed = jnp.concatenate([scr[...] for scr in mix_scrs], axis=1)
    o_ref[...] = (mixed * ps_ref[...] * _silu(pg_ref[...])).astype(BF16)


def _pool_sample(zs, state_pool_t, a_buf, new_hist, pool_w, pool_scale, layer, n_seq, n_new, row0):
    rows = n_seq * n_new
    assert row0 % rows == 0
    hist_blk = (None, POOL_HIST, n_seq, POOL_GROUP)
    hist_map = lambda g: (layer, 0, 0, g)
    per_seg = D_MODEL // POOL_GROUP

    def u_tile(c):
        return pl.BlockSpec((rows, LANE), lambda g: (0, (SEG_U * per_seg + g) * LANE_TILES + c))

    in_specs = [pl.BlockSpec(hist_blk, hist_map)] + [u_tile(c) for c in range(LANE_TILES)]
    in_specs += [pl.BlockSpec((rows, POOL_GROUP), lambda g: (0, SEG_PG * per_seg + g)),
                 pl.BlockSpec((None, None, POOL_GROUP, POOL_GROUP), lambda g: (layer, g, 0, 0)),
                 pl.BlockSpec((1, POOL_GROUP), lambda g: (0, g)),
                 pl.BlockSpec(memory_space=pl.ANY)]
    args = [state_pool_t] + [zs] * LANE_TILES + [zs, pool_w, pool_scale, a_buf]
    aliases = {len(args) - 1: 0}
    if new_hist is not None:
        aliases[len(args)] = 1
        in_specs.append(pl.BlockSpec(memory_space=pl.ANY))
        args.append(new_hist)
    return pl.pallas_call(
        functools.partial(_pool_sample_kernel, n_seq=n_seq, n_new=n_new),
        out_shape=(jax.ShapeDtypeStruct(a_buf.shape, BF16),
                   jax.ShapeDtypeStruct(state_pool_t.shape, F32)),
        grid=(N_GROUPS,),
        in_specs=in_specs,
        out_specs=(pl.BlockSpec((rows, POOL_GROUP), lambda g: (row0 // rows, g)),
                   pl.BlockSpec(hist_blk, hist_map)),
        scratch_shapes=[pltpu.VMEM((rows, POOL_GROUP), F32)] + [pltpu.VMEM((rows, LANE), F32)] * LANE_TILES,
        input_output_aliases=aliases,
        compiler_params=_cparams("arbitrary"),
        name="pool_sample",
    )(*args)


def _ret_kernel(dec_ref, decay_ref, q_ref, k_ref, v_ref, rg_ref, gn_ref, s0_ref, *rest, n_valid):
    o_ref, sfin_ref, s_scr = rest[-3:]
    c = pl.program_id(1)

    @pl.when(c == 0)
    def _():
        s_scr[...] = s0_ref[...]

    ridx = lax.broadcasted_iota(jnp.int32, (CHUNK, 1), 0).astype(F32)
    for h in range(HEADS):
        hs = slice(h * HEAD_DIM, (h + 1) * HEAD_DIM)
        lg = dec_ref[0, h]
        q = q_ref[:, hs].astype(BF16)
        k = k_ref[:, hs]
        v = v_ref[:, hs].astype(BF16)
        s_old = s_scr[h]
        scores = _dot_nt(q, k.astype(BF16)) * decay_ref[h]
        intra = _dot(scores.astype(BF16), v)
        inter = _dot(q, s_old.astype(BF16)) * jnp.exp((ridx + 1.0) * lg)
        k_dec = (k * jnp.exp((n_valid - 1.0 - ridx) * lg)).astype(BF16)
        s_scr[h] = s_old * dec_ref[1, h] + _dot_tn(k_dec, v)
        o_ref[:, hs] = _head_norm_gate(intra + inter, gn_ref[:, hs], rg_ref[:, hs])

    @pl.when(c == pl.num_programs(1) - 1)
    def _():
        sfin_ref[...] = s_scr[...]


def _retention(srcs, lg, gn, decay, s0, b_buf, fin, fin_shape, fin_index, n_batch, n_chunks, n_valid):
    dec = jnp.stack([lg, jnp.exp(n_valid * lg)])

    def zmap(col):
        return lambda b, c: (b * n_chunks + c, col)

    in_specs = [pl.BlockSpec(memory_space=pltpu.SMEM),
                pl.BlockSpec((HEADS, CHUNK, CHUNK), lambda b, c: (0, 0, 0))]
    in_specs += [pl.BlockSpec((CHUNK, D_MODEL), zmap(col)) for _, col in srcs]
    in_specs += [pl.BlockSpec((1, D_MODEL), lambda b, c: (0, 0)),
                 pl.BlockSpec(_STATE, lambda b, c: (0, 0, 0))]
    args = [dec, decay] + [a for a, _ in srcs] + [gn, s0]
    aliases = {}
    for out_idx, buf in enumerate((b_buf, fin)):
        if buf is not None:
            aliases[len(args)] = out_idx
            in_specs.append(pl.BlockSpec(memory_space=pl.ANY))
            args.append(buf)
    fin_blk = (None,) * (len(fin_shape) - 3) + _STATE
    b_rows = n_batch * n_chunks * CHUNK if b_buf is None else b_buf.shape[0]
    return pl.pallas_call(
        functools.partial(_ret_kernel, n_valid=n_valid),
        out_shape=(jax.ShapeDtypeStruct((b_rows, D_MODEL), BF16),
                   jax.ShapeDtypeStruct(fin_shape, F32)),
        grid=(n_batch, n_chunks),
        in_specs=in_specs,
        out_specs=(pl.BlockSpec((CHUNK, D_MODEL), zmap(0)),
                   pl.BlockSpec(fin_blk, lambda b, c: fin_index(b) + (0, 0, 0))),
        scratch_shapes=[pltpu.VMEM(_STATE, F32)],
        input_output_aliases=aliases,
        compiler_params=_cparams("arbitrary", "arbitrary"),
        name="retention",
    )(*args)


TN_MERGE = 1024


def _merge_kernel(a_ref, b_ref, gpp_ref, grp_ref, gps_ref, grs_ref,
                  am_ref, bm_ref, gpm_ref, grm_ref, wp_ref, wr_ref, o_ref, om_ref, *, n_p, n_s):
    i = pl.program_id(1)

    def tile(a, b, gp, gr, out):
        pool_branch = _dot(a[...], wp_ref[...])
        ret_branch = _dot(b[...], wr_ref[...])
        out[...] = (jax.nn.sigmoid(gp[...]) * pool_branch + jax.nn.sigmoid(gr[...]) * ret_branch).astype(BF16)

    @pl.when(i < n_p)
    def _():
        tile(a_ref, b_ref, gpp_ref, grp_ref, o_ref)

    @pl.when((i >= n_p) & (i < n_p + n_s))
    def _():
        tile(a_ref, b_ref, gps_ref, grs_ref, o_ref)

    @pl.when(i == n_p + n_s)
    def _():
        tile(am_ref, bm_ref, gpm_ref, grm_ref, om_ref)


def _merge(a_in, b_in, z, zs, a_meta, b_meta, zm, proj_pool, proj_ret, layer):
    n_p = z.shape[0] // TM
    n_s = zs.shape[0] // TM
    n_main = n_p + n_s
    assert a_in.shape[0] == n_main * TM and b_in.shape[0] == n_main * TM
    nt = D_MODEL // TN_MERGE

    def p_map(col):
        return lambda j, i: (jnp.minimum(i, n_p - 1), col(j))

    def s_map(col):
        return lambda j, i: (jnp.clip(i - n_p, 0, n_s - 1), col(j))

    row = pl.BlockSpec((TM, D_MODEL), lambda j, i: (jnp.minimum(i, n_main - 1), 0))
    gate = lambda ref_map, seg: pl.BlockSpec((TM, TN_MERGE), ref_map(lambda j: seg * nt + j))
    const = lambda j, i: (0, 0)
    return pl.pallas_call(
        functools.partial(_merge_kernel, n_p=n_p, n_s=n_s),
        out_shape=(jax.ShapeDtypeStruct((n_main * TM, D_MODEL), BF16),
                   jax.ShapeDtypeStruct((CHUNK, D_MODEL), BF16)),
        grid=(nt, n_main + 1),
        in_specs=[row, row,
                  gate(p_map, COL_GP), gate(p_map, COL_GR), gate(s_map, SEG_GP), gate(s_map, SEG_GR),
                  pl.BlockSpec((CHUNK, D_MODEL), const),
                  pl.BlockSpec((CHUNK, D_MODEL), const),
                  pl.BlockSpec((CHUNK, TN_MERGE), lambda j, i: (0, SEG_GP * nt + j)),
                  pl.BlockSpec((CHUNK, TN_MERGE), lambda j, i: (0, SEG_GR * nt + j)),
                  pl.BlockSpec((None, D_MODEL, TN_MERGE), lambda j, i: (layer, 0, j)),
                  pl.BlockSpec((None, D_MODEL, TN_MERGE), lambda j, i: (layer, 0, j))],
        out_specs=(pl.BlockSpec((TM, TN_MERGE), lambda j, i: (jnp.minimum(i, n_main - 1), j)),
                   pl.BlockSpec((CHUNK, TN_MERGE), lambda j, i: (0, j))),
        compiler_params=_cparams("arbitrary", "arbitrary"),
        name="merge_branches",
    )(a_in, b_in, z, z, zs, zs, a_meta, b_meta, zm, zm, proj_pool, proj_ret)


def _out_kernel(m_ref, mm_ref, w_ref, g_ref, *rest, first, last, n_p, n_s):
    rest = list(rest)
    x_refs = [rest.pop(0) for _ in range(3 if first else 2)]
    i = pl.program_id(0)

    def emit(x, merged, outs):
        xn = x + _dot(merged[...], w_ref[...])
        y = _rms(xn, g_ref[...])
        if last:
            outs[0][...] = y
        else:
            outs[0][...] = xn
            outs[1][...] = y.astype(BF16)

    main_outs = [rest[0:1], rest[1:2]] if last else [rest[0:2], rest[0:2]]

    @pl.when(i < n_p)
    def _():
        emit(x_refs[0][...], m_ref, main_outs[0])

    @pl.when((i >= n_p) & (i < n_p + n_s))
    def _():
        emit(x_refs[1 if first else 0][...], m_ref, main_outs[1])

    if not last:
        @pl.when(i == n_p + n_s)
        def _():
            emit(_meta_rows(x_refs[2]) if first else x_refs[1][...], mm_ref, rest[2:4])


def _out_projection(merged, merged_meta, w_out, layer, g_next, x_parts, first, last, n_p, n_s):
    n = n_p + n_s
    main_spec = pl.BlockSpec((TM, D_MODEL), lambda i: (jnp.minimum(i, n - 1), 0))
    meta_spec = pl.BlockSpec((CHUNK, D_MODEL), lambda i: (0, 0))
    in_specs = [main_spec, meta_spec,
                pl.BlockSpec((None, D_MODEL, D_MODEL), lambda i: (layer, 0, 0)),
                pl.BlockSpec((1, D_MODEL), lambda i: (0, 0))]
    if first:
        in_specs += _x_specs(n_p, n_s) + [pl.BlockSpec((N_META, D_MODEL), lambda i: (0, 0))]
    else:
        in_specs += [main_spec, meta_spec]
    if last:
        out_shape = (jax.ShapeDtypeStruct((n_p * TM, D_MODEL), F32),
                     jax.ShapeDtypeStruct((n_s * TM, D_MODEL), F32))
        out_specs = tuple(_x_specs(n_p, n_s))
    else:
        out_shape = (jax.ShapeDtypeStruct((n * TM, D_MODEL), F32), jax.ShapeDtypeStruct((n * TM, D_MODEL), BF16),
                     jax.ShapeDtypeStruct((CHUNK, D_MODEL), F32), jax.ShapeDtypeStruct((CHUNK, D_MODEL), BF16))
        out_specs = (main_spec, main_spec, meta_spec, meta_spec)
    return pl.pallas_call(
        functools.partial(_out_kernel, first=first, last=last, n_p=n_p, n_s=n_s),
        out_shape=out_shape,
        grid=(n if last else n + 1,),
        in_specs=in_specs,
        out_specs=out_specs,
        compiler_params=_cparams("arbitrary"),
        name="out_projection",
    )(merged, merged_meta, w_out, g_next, *x_parts)


def _log_decay():
    return jnp.log1p(-jnp.exp2(-5.0 - jnp.arange(HEADS, dtype=F32)))


def _rope_tables(pos):
    inv_freq = ROPE_BASE ** (-jnp.arange(HALF, dtype=F32) / HALF)
    ang = pos[:, None] * inv_freq[None, :]
    return jnp.cos(ang), jnp.sin(ang)


def _chunk_decay(lg):
    idx = jnp.arange(CHUNK, dtype=F32)
    diff = idx[:, None] - idx[None, :]
    causal = diff >= 0
    return jnp.where(causal[None], jnp.exp(jnp.where(causal, diff, 0.0)[None] * lg[:, None, None]), 0.0)


def _sample_decay_tables(lg, n_new):
    r = jnp.arange(RB * HEADS * n_new)
    head = (r // n_new) % HEADS
    tok = (r % n_new).astype(F32)
    lg_r = lg[head]
    same = (r[:, None] // n_new) == (r[None, :] // n_new)
    diff = tok[:, None] - tok[None, :]
    keep = same & (diff >= 0)
    dmask = jnp.where(keep, jnp.exp(jnp.where(keep, diff, 0.0) * lg_r[:, None]), 0.0)
    qdec = jnp.broadcast_to(jnp.exp((tok + 1.0) * lg_r)[:, None], (r.shape[0], HEAD_DIM))
    kdec = jnp.broadcast_to(jnp.exp((n_new - 1.0 - tok) * lg_r)[:, None], (r.shape[0], HEAD_DIM))
    return dmask, qdec, kdec


def kernel(x_prompt, x_sample, state_pool, state_ret, meta_tokens, norm_gain, w_in, pool_w, pool_scale,
           ret_gn_gain, proj_pool, proj_ret, w_out, final_norm):
    n_batch, seq = x_prompt.shape[:2]
    n_seq, n_new = x_sample.shape[:2]
    depth = norm_gain.shape[0]
    rows_prompt = n_batch * seq
    rows_sample = n_seq * n_new
    assert seq % CHUNK == 0 and n_seq % RB == 0
    assert n_new == 8, "sample tokens of one sequence must fill one f32 sublane tile"
    assert rows_prompt % TM == 0 and rows_sample % TM == 0
    n_p = rows_prompt // TM
    n_s = rows_sample // TM

    xp = x_prompt.reshape(rows_prompt, D_MODEL).astype(F32)
    xs = x_sample.reshape(rows_sample, D_MODEL).astype(F32)
    meta = meta_tokens.astype(F32)
    state_pool = state_pool.astype(F32)
    state_ret = state_ret.astype(F32)
    w_in = w_in.astype(F32)

    trig_p = _rope_tables(N_META + jnp.arange(seq, dtype=F32))
    trig_s = tuple(jnp.tile(t, (n_seq, 1)) for t in _rope_tables(PAST_LEN + jnp.arange(n_new, dtype=F32)))
    trig_m = _rope_tables(jnp.where(jnp.arange(CHUNK) < N_META, jnp.arange(CHUNK, dtype=F32), 0.0))
    lg = _log_decay()
    decay = _chunk_decay(lg)
    gam_new = jnp.exp(n_new * lg)
    tables = _sample_decay_tables(lg, n_new)
    zero_state = jnp.zeros(_STATE, F32)
    row = lambda a: a.reshape(1, D_MODEL).astype(F32)

    h, h_meta = _first_norm(xp, xs, meta, row(norm_gain[0]), n_p, n_s)
    x_parts = (xp, xs, meta)
    pool_p = []
    ret_p = pool_s = ret_s = None
    pw = pool_w.astype(BF16)
    proj_pool = proj_pool.astype(BF16)
    proj_ret = proj_ret.astype(BF16)
    w_out = w_out.astype(BF16)
    state_pool_t = jnp.transpose(state_pool, (0, 2, 1, 3))
    for l in range(depth):
        ps = row(pool_scale[l])
        gn = row(ret_gn_gain[l])

        zs, zm, wb = _in_projection_small(h, h_meta, w_in, l, trig_s, trig_m, rows_prompt, rows_sample)
        zr = _in_projection(h, wb, rows_prompt, ROPE_SEGS, trig=trig_p, seq=seq)
        z, b_in, ret_s = _in_projection(
            h, wb, rows_prompt, PLAIN_SEGS, sample=(gam_new, gn, tables, state_ret, ret_s, l, n_new, zs))

        a_meta = _pool_meta(zm, pw, ps, l)
        a_in, hist_p = _proj_pool(h, wb, zm, pw, ps, l, n_batch, seq, rows_prompt + rows_sample)
        a_in, pool_s = _pool_sample(zs, state_pool_t, a_in, pool_s, pw, ps, l, n_seq, n_new, rows_prompt)
        pool_p.append(hist_p)

        b_meta, s_meta = _retention(((zm, SEG_Q), (zm, SEG_K), (zm, SEG_V), (zm, SEG_RG)), lg, gn, decay,
                                    zero_state, None, None, _STATE, lambda b: (), 1, 1, N_META)
        b_in, ret_p = _retention(((zr, COL_Q), (zr, COL_K), (z, COL_V), (z, COL_RG)), lg, gn, decay,
                                 s_meta, b_in, ret_p, (depth, n_batch) + _STATE, lambda b, l=l: (l, b),
                                 n_batch, seq // CHUNK, CHUNK)

        merged, merged_meta = _merge(a_in, b_in, z, zs, a_meta, b_meta, zm, proj_pool, proj_ret, l)
        last = l == depth - 1
        g_next = row(final_norm if last else norm_gain[l + 1])
        outs = _out_projection(merged, merged_meta, w_out, l, g_next, x_parts, l == 0, last, n_p, n_s)
        if last:
            y_prompt, y_sample = outs
        else:
            x_new, h, x_meta, h_meta = outs
            x_parts = (x_new, x_meta)

    return (y_prompt.reshape(n_batch, seq, D_MODEL), y_sample.reshape(n_seq, n_new, D_MODEL),
            jnp.stack(pool_p), ret_p, jnp.transpose(pool_s, (0, 2, 1, 3)), ret_s)
```

```python
import functools

import jax
import jax.numpy as jnp
from jax import lax
from jax.experimental import pallas as pl
from jax.experimental.pallas import tpu as pltpu

F32 = jnp.float32
BF16 = jnp.bfloat16

D_MODEL = 2048
N_META = 16
POOL_WINDOWS = (2, 4, 8, 16)
N_GROUPS = len(POOL_WINDOWS)
POOL_GROUP = D_MODEL // N_GROUPS
POOL_HIST = max(POOL_WINDOWS) - 1
HEADS = 8
HEAD_DIM = D_MODEL // HEADS
HALF = HEAD_DIM // 2
CHUNK = 128
ROPE_BASE = 10000.0
EPS = 1e-6
PAST_LEN = 16384
SEG_U, SEG_PG, SEG_Q, SEG_K, SEG_V, SEG_RG, SEG_GP, SEG_GR = range(8)
N_SEG = 8
ROPE_SEGS = (SEG_Q, SEG_K)
COL_Q, COL_K = range(len(ROPE_SEGS))
PLAIN_SEGS = (SEG_V, SEG_RG, SEG_GP, SEG_GR)
COL_V, COL_RG, COL_GP, COL_GR = range(len(PLAIN_SEGS))

TM = 512
TM_IN = 1024
TN_IN = 1024
PER_SEG = D_MODEL // TN_IN
VMEM_LIMIT = 56 * 1024 * 1024


def _cparams(*sem, **kw):
    return pltpu.CompilerParams(dimension_semantics=sem, vmem_limit_bytes=VMEM_LIMIT, **kw)


def _dot(a, b):
    return jnp.dot(a, b, preferred_element_type=F32)


def _dot_nt(a, b):
    return lax.dot_general(a, b, (((1,), (1,)), ((), ())), preferred_element_type=F32)


def _dot_tn(a, b):
    return lax.dot_general(a, b, (((0,), (0,)), ((), ())), preferred_element_type=F32)


def _rms(x, g):
    return x * lax.rsqrt(jnp.mean(x * x, axis=-1, keepdims=True) + EPS) * g


def _silu(x):
    return x * jax.nn.sigmoid(x)


def _x_specs(n_p, n_s):
    return [pl.BlockSpec((TM, D_MODEL), lambda i: (jnp.minimum(i, n_p - 1), 0)),
            pl.BlockSpec((TM, D_MODEL), lambda i: (jnp.clip(i - n_p, 0, n_s - 1), 0))]


def _meta_rows(meta_ref):
    return jnp.concatenate([meta_ref[...], jnp.zeros((CHUNK - N_META, D_MODEL), F32)], axis=0)


def _norm_kernel(xp_ref, xs_ref, meta_ref, g_ref, h_ref, hm_ref, *, n_p, n_s):
    i = pl.program_id(0)

    @pl.when(i < n_p)
    def _():
        h_ref[...] = _rms(xp_ref[...], g_ref[...]).astype(BF16)

    @pl.when((i >= n_p) & (i < n_p + n_s))
    def _():
        h_ref[...] = _rms(xs_ref[...], g_ref[...]).astype(BF16)

    @pl.when(i == n_p + n_s)
    def _():
        hm_ref[...] = _rms(_meta_rows(meta_ref), g_ref[...]).astype(BF16)


def _first_norm(xp, xs, meta, g, n_p, n_s):
    n = n_p + n_s
    return pl.pallas_call(
        functools.partial(_norm_kernel, n_p=n_p, n_s=n_s),
        out_shape=(jax.ShapeDtypeStruct((n * TM, D_MODEL), BF16),
                   jax.ShapeDtypeStruct((CHUNK, D_MODEL), BF16)),
        grid=(n + 1,),
        in_specs=_x_specs(n_p, n_s) + [pl.BlockSpec((N_META, D_MODEL), lambda i: (0, 0)),
                                       pl.BlockSpec((1, D_MODEL), lambda i: (0, 0))],
        out_specs=(pl.BlockSpec((TM, D_MODEL), lambda i: (jnp.minimum(i, n - 1), 0)),
                   pl.BlockSpec((CHUNK, D_MODEL), lambda i: (0, 0))),
        compiler_params=_cparams("arbitrary"),
        name="first_norm",
    )(xp, xs, meta, g)


RB = 2
_STATE = (HEADS, HEAD_DIM, HEAD_DIM)


def _head_norm_gate(o, gain, gate):
    mu = jnp.mean(o, axis=-1, keepdims=True)
    oc = o - mu
    on = oc * lax.rsqrt(jnp.mean(oc * oc, axis=-1, keepdims=True) + EPS)
    return ((on * gain) * _silu(gate)).astype(BF16)


def _sample_body(gam_ref, q_ref, k_ref, v_ref, rg_ref, gn_ref, dmask_ref, qdec_ref, kdec_ref, s_ref,
                 o_ref, sout_ref, n_new):
    pairs = [(b, h) for b in range(RB) for h in range(HEADS)]

    def stack(ref):
        return jnp.concatenate(
            [ref[b * n_new:(b + 1) * n_new, h * HEAD_DIM:(h + 1) * HEAD_DIM] for b, h in pairs], axis=0)

    k = stack(k_ref)
    qb = stack(q_ref).astype(BF16)
    v = stack(v_ref)
    vb = v.astype(BF16)
    scores = _dot_nt(qb, k.astype(BF16)) * dmask_ref[...]
    intra = _dot(scores.astype(BF16), vb)
    k_dec = (k * kdec_ref[...]).astype(BF16)
    rows = lax.broadcasted_iota(jnp.int32, v.shape, 0)
    is_even = (rows & n_new) == 0
    v_even = jnp.where(is_even, v, 0.0).astype(BF16)
    v_odd = jnp.where(is_even, 0.0, v).astype(BF16)
    grp = 2 * n_new
    inter_parts = []
    for idx, (b, h) in enumerate(pairs):
        g0 = (idx // 2) * grp
        off = (idx % 2) * n_new
        s_old = s_ref[b, h]
        inter = _dot(qb[g0:g0 + grp], s_old.astype(BF16))
        inter_parts.append(inter[off:off + n_new])
        v_sel = v_even if idx % 2 == 0 else v_odd
        sout_ref[b, h] = s_old * gam_ref[h] + _dot_tn(k_dec[g0:g0 + grp], v_sel[g0:g0 + grp])
    o = intra + jnp.concatenate(inter_parts, axis=0) * qdec_ref[...]
    gain = jnp.concatenate(
        [jnp.broadcast_to(gn_ref[:, h * HEAD_DIM:(h + 1) * HEAD_DIM], (n_new, HEAD_DIM)) for _, h in pairs],
        axis=0)
    out = _head_norm_gate(o, gain, stack(rg_ref)).astype(F32)
    seqs = []
    for b in range(RB):
        seqs.append(jnp.concatenate(
            [out[(b * HEADS + h) * n_new:(b * HEADS + h + 1) * n_new] for h in range(HEADS)], axis=1))
    o_ref[...] = jnp.concatenate(seqs, axis=0).astype(BF16)


def _rope_store(z, o_ref, cos_ref, sin_ref, scale):
    cos = cos_ref[...]
    sin = sin_ref[...]
    for hh in range(TN_IN // HEAD_DIM):
        lo = slice(hh * HEAD_DIM, hh * HEAD_DIM + HALF)
        hi = slice(hh * HEAD_DIM + HALF, (hh + 1) * HEAD_DIM)
        t1 = z[:, lo]
        t2 = z[:, hi]
        o_ref[:, lo] = (t1 * cos - t2 * sin) * scale
        o_ref[:, hi] = (t2 * cos + t1 * sin) * scale


def _k_scale(is_k):
    return jnp.where(is_k, HEAD_DIM ** -0.5, 1.0).astype(F32)


def _inproj_small_kernel(hs_ref, hm_ref, w_ref, cos_ref, sin_ref, cosm_ref, sinm_ref, zs_ref, zm_ref, wb_ref):
    s = pl.program_id(0)
    wb_ref[...] = w_ref[...].astype(BF16)
    zs_ref[...] = _dot(hs_ref[...], wb_ref[...])
    zm_ref[...] = _dot(hm_ref[...], wb_ref[...])

    @pl.when((s >= SEG_Q * PER_SEG) & (s < (SEG_K + 1) * PER_SEG))
    def _():
        scale = _k_scale(s >= SEG_K * PER_SEG)
        _rope_store(zs_ref[...], zs_ref, cos_ref, sin_ref, scale)
        _rope_store(zm_ref[...], zm_ref, cosm_ref, sinm_ref, scale)


def _in_projection_small(h, h_meta, w_in, layer, trig_s, trig_m, row0, rows):
    assert row0 % rows == 0
    const = lambda s: (0, 0)
    col = lambda s: (0, s)
    return pl.pallas_call(
        _inproj_small_kernel,
        out_shape=(jax.ShapeDtypeStruct((rows, N_SEG * D_MODEL), F32),
                   jax.ShapeDtypeStruct((CHUNK, N_SEG * D_MODEL), F32),
                   jax.ShapeDtypeStruct((D_MODEL, N_SEG * D_MODEL), BF16)),
        grid=(N_SEG * PER_SEG,),
        in_specs=[pl.BlockSpec((rows, D_MODEL), lambda s: (row0 // rows, 0)),
                  pl.BlockSpec((CHUNK, D_MODEL), const),
                  pl.BlockSpec((None, D_MODEL, TN_IN), lambda s: (layer, 0, s)),
                  pl.BlockSpec((rows, HALF), const),
                  pl.BlockSpec((rows, HALF), const),
                  pl.BlockSpec((CHUNK, HALF), const),
                  pl.BlockSpec((CHUNK, HALF), const)],
        out_specs=(pl.BlockSpec((rows, TN_IN), col),
                   pl.BlockSpec((CHUNK, TN_IN), col),
                   pl.BlockSpec((D_MODEL, TN_IN), col)),
        compiler_params=_cparams("arbitrary"),
        name="in_projection_small",
    )(h, h_meta, w_in, *trig_s, *trig_m)


N_SAMPLE_IN = 10


def _sample_operands(sample, group_of):
    gam, gn, (dmask, qdec, kdec), state_ret, layer, n_new, zs, prompt_rows = sample
    rows_g = RB * n_new
    stack_rows = RB * HEADS * n_new
    assert prompt_rows % rows_g == 0

    def zmap(seg):
        return lambda *idx: (group_of(*idx), seg)

    const = lambda *idx: (0, 0)
    sblk = (None, RB) + _STATE
    state_map = lambda *idx: (layer, group_of(*idx), 0, 0, 0)
    in_specs = [pl.BlockSpec(memory_space=pltpu.SMEM),
                pl.BlockSpec((rows_g, D_MODEL), zmap(SEG_Q)),
                pl.BlockSpec((rows_g, D_MODEL), zmap(SEG_K)),
                pl.BlockSpec((rows_g, D_MODEL), zmap(SEG_V)),
                pl.BlockSpec((rows_g, D_MODEL), zmap(SEG_RG)),
                pl.BlockSpec((1, D_MODEL), const),
                pl.BlockSpec((stack_rows, stack_rows), const),
                pl.BlockSpec((stack_rows, HEAD_DIM), const),
                pl.BlockSpec((stack_rows, HEAD_DIM), const),
                pl.BlockSpec(sblk, state_map)]
    args = [gam, zs, zs, zs, zs, gn, dmask, qdec, kdec, state_ret]
    out_shape = [jax.ShapeDtypeStruct((prompt_rows + zs.shape[0], D_MODEL), BF16),
                 jax.ShapeDtypeStruct(state_ret.shape, F32)]
    out_specs = [pl.BlockSpec((rows_g, D_MODEL), lambda *idx: (prompt_rows // rows_g + group_of(*idx), 0)),
                 pl.BlockSpec(sblk, state_map)]
    return in_specs, args, out_shape, out_specs


def _inproj_kernel(h_ref, w_ref, *rest, rope, n_new, n_groups):
    if rope:
        cos_ref, sin_ref = rest[:2]
        n_w = (len(rest) - 3) // 2
        o_ref = rest[2 + n_w]
        z = _dot(h_ref[...], w_ref[...])
        _rope_store(z, o_ref, cos_ref, sin_ref, _k_scale(pl.program_id(0) >= PER_SEG))
        for src, dst in zip(rest[2:2 + n_w], rest[3 + n_w:]):
            dst[...] = src[...].astype(BF16)
        return
    if n_new is None:
        rest[-1][...] = _dot(h_ref[...], w_ref[...])
        return
    o_ref, so_ref, sout_ref = rest[-3:]
    step = pl.program_id(0) * pl.num_programs(1) + pl.program_id(1)

    @pl.when(step < n_groups)
    def _():
        o_ref[...] = _dot(h_ref[...], w_ref[...])
        _sample_body(*rest[:N_SAMPLE_IN], so_ref, sout_ref, n_new)

    @pl.when(step >= n_groups)
    def _():
        o_ref[...] = _dot(h_ref[...], w_ref[...])


def _in_projection(h, wb, rows, segs, trig=None, seq=None, sample=None, convert=None):
    assert rows % TM_IN == 0
    tiles = rows // TM_IN
    rope = trig is not None
    gap_at = next((j for j in range(1, len(segs)) if segs[j] != segs[j - 1] + 1), len(segs))
    gap = segs[gap_at] - segs[gap_at - 1] - 1 if gap_at < len(segs) else 0
    assert all(segs[j] == segs[0] + j + (gap if j >= gap_at else 0) for j in range(len(segs)))

    def wmap(s, i):
        j = s // PER_SEG
        seg = segs[0] + j + jnp.where(j >= gap_at, gap, 0)
        return (0, seg * PER_SEG + s % PER_SEG)

    in_specs = [pl.BlockSpec((TM_IN, D_MODEL), lambda s, i: (i, 0)),
                pl.BlockSpec((D_MODEL, TN_IN), wmap)]
    args = [h, wb]
    out_shape = [jax.ShapeDtypeStruct((rows, len(segs) * D_MODEL), F32)]
    out_specs = [pl.BlockSpec((TM_IN, TN_IN), lambda s, i: (i, s))]
    aliases = {}
    n_new = n_groups = None
    if rope:
        assert seq % TM_IN == 0
        in_specs += [pl.BlockSpec((TM_IN, HALF), lambda s, i: (i % (seq // TM_IN), 0)) for _ in range(2)]
        args += list(trig)
    if convert is not None:
        weights, w_layer = convert
        steps = len(segs) * PER_SEG * tiles
        assert rope and D_MODEL % steps == 0
        w_rows = D_MODEL // steps
        in_specs += [pl.BlockSpec((None, w_rows, D_MODEL), lambda s, i: (w_layer, s * tiles + i, 0))
                     for _ in weights]
        args += list(weights)
        out_shape += [jax.ShapeDtypeStruct((D_MODEL, D_MODEL), BF16) for _ in weights]
        out_specs += [pl.BlockSpec((w_rows, D_MODEL), lambda s, i: (s * tiles + i, 0)) for _ in weights]
    if sample is not None:
        sample, first_group, b_buf, new_state = sample
        n_new = sample[5]
        n_groups = sample[3].shape[1] // RB - first_group
        assert 0 < n_groups <= len(segs) * PER_SEG * tiles, "one group of sample sequences per grid step"

        def group(s, i):
            return first_group + jnp.minimum(s * tiles + i, n_groups - 1)

        s_in, s_args, s_shape, s_specs = _sample_operands(sample, group)
        in_specs += s_in
        args += s_args
        out_shape += s_shape
        out_specs += s_specs
        for out_idx, buf in ((1, b_buf), (2, new_state)):
            aliases[len(args)] = out_idx
            in_specs.append(pl.BlockSpec(memory_space=pl.ANY))
            args.append(buf)
    outs = pl.pallas_call(
        functools.partial(_inproj_kernel, rope=rope, n_new=n_new, n_groups=n_groups),
        out_shape=tuple(out_shape),
        grid=(len(segs) * PER_SEG, tiles),
        in_specs=in_specs,
        out_specs=tuple(out_specs),
        input_output_aliases=aliases,
        compiler_params=_cparams("arbitrary", "arbitrary"),
        name="in_projection_rope" if rope else "in_projection",
    )(*args)
    return outs if len(outs) > 1 else outs[0]


def _window_sum(ext, w, base, rows):
    s = ext
    size = 1
    while size < w:
        s = s[size:] + s[:-size]
        size *= 2
    start = base - (w - 1)
    return s[start:start + rows]


def _pool_group(g, wsum, u, inv_cnt, pw_ref, ps_ref, gate):
    sl = slice(g * POOL_GROUP, (g + 1) * POOL_GROUP)
    pooled = wsum * inv_cnt - u
    mixed = _dot(pooled.astype(BF16), pw_ref[g])
    return (mixed * ps_ref[:, sl] * _silu(gate)).astype(BF16)


def _pool_w_spec(layer):
    return pl.BlockSpec((None, N_GROUPS, POOL_GROUP, POOL_GROUP), lambda *_: (layer, 0, 0, 0))


def _pool_meta_kernel(u_ref, pg_ref, pw_ref, ps_ref, o_ref):
    avail = lax.broadcasted_iota(jnp.int32, (N_META, 1), 0).astype(F32) + 1.0
    o_ref[N_META:] = jnp.zeros((CHUNK - N_META, D_MODEL), BF16)
    for g, w in enumerate(POOL_WINDOWS):
        sl = slice(g * POOL_GROUP, (g + 1) * POOL_GROUP)
        u = u_ref[:, sl]
        wsum = _window_sum(jnp.concatenate([jnp.zeros_like(u), u], axis=0), w, N_META, N_META)
        inv_cnt = 1.0 / jnp.minimum(float(w), avail)
        o_ref[0:N_META, sl] = _pool_group(g, wsum, u, inv_cnt, pw_ref, ps_ref, pg_ref[:, sl])


def _pool_meta(zm, pool_w, pool_scale, layer):
    return pl.pallas_call(
        _pool_meta_kernel,
        out_shape=jax.ShapeDtypeStruct((CHUNK, D_MODEL), BF16),
        grid=(1,),
        in_specs=[pl.BlockSpec((N_META, D_MODEL), lambda i: (0, SEG_U)),
                  pl.BlockSpec((N_META, D_MODEL), lambda i: (0, SEG_PG)),
                  _pool_w_spec(layer),
                  pl.BlockSpec((1, D_MODEL), lambda i: (0, 0))],
        out_specs=pl.BlockSpec((CHUNK, D_MODEL), lambda i: (0, 0)),
        compiler_params=_cparams("arbitrary"),
        name="pool_meta",
    )(zm, zm, pool_w, pool_scale)


GROUPS_PER_BLOCK = TN_IN // POOL_GROUP


def _proj_pool_kernel(h_ref, wu_ref, wpg_ref, meta_ref, pw_ref, ps_ref, *rest, tiles_per_seq, n_new):
    sample_refs = rest[:N_SAMPLE_IN]
    o_ref, hist_ref, so_ref, sout_ref, carry = rest[-5:]
    c = pl.program_id(0)
    i = pl.program_id(1)
    first = (i % tiles_per_seq) == 0
    last = (i % tiles_per_seq) == tiles_per_seq - 1
    for cc in range(D_MODEL // TN_IN):
        @pl.when(c == cc)
        def _(cc=cc):
            u = _dot(h_ref[...], wu_ref[...])
            pg = _dot(h_ref[...], wpg_ref[...])
            prev = jnp.where(first, meta_ref[...], carry[...])
            for gg in range(GROUPS_PER_BLOCK):
                w = POOL_WINDOWS[cc * GROUPS_PER_BLOCK + gg]
                sl = slice(gg * POOL_GROUP, (gg + 1) * POOL_GROUP)
                wsum = _window_sum(jnp.concatenate([prev[:, sl], u[:, sl]], axis=0), w, N_META, TM_IN)
                pooled = wsum * (1.0 / w) - u[:, sl]
                mixed = _dot(pooled.astype(BF16), pw_ref[gg])
                o_ref[:, sl] = (mixed * ps_ref[:, sl] * _silu(pg[:, sl])).astype(BF16)
            carry[...] = u[TM_IN - N_META:]
            _sample_body(*sample_refs, so_ref, sout_ref, n_new)

            @pl.when(last)
            def _():
                hist_ref[0] = u[TM_IN - POOL_HIST:]


def _proj_pool(h, wb, zm, pool_w, pool_scale, layer, n_batch, seq, sample, new_state):
    assert seq % TM_IN == 0 and TN_IN % POOL_GROUP == 0
    tiles_per_seq = seq // TM_IN
    tiles = n_batch * tiles_per_seq
    step = lambda c, i: c * tiles + i
    in_specs = [pl.BlockSpec((TM_IN, D_MODEL), lambda c, i: (i, 0)),
                pl.BlockSpec((D_MODEL, TN_IN), lambda c, i: (0, SEG_U * PER_SEG + c)),
                pl.BlockSpec((D_MODEL, TN_IN), lambda c, i: (0, SEG_PG * PER_SEG + c)),
                pl.BlockSpec((N_META, TN_IN), lambda c, i: (0, SEG_U * PER_SEG + c)),
                pl.BlockSpec((None, GROUPS_PER_BLOCK, POOL_GROUP, POOL_GROUP), lambda c, i: (layer, c, 0, 0)),
                pl.BlockSpec((1, TN_IN), lambda c, i: (0, c))]
    args = [h, wb, wb, zm, pool_w, pool_scale]
    s_in, s_args, s_shape, s_specs = _sample_operands(sample, step)
    in_specs += s_in
    args += s_args
    total_rows = s_shape[0].shape[0]
    out_shape = [jax.ShapeDtypeStruct((total_rows, D_MODEL), BF16),
                 jax.ShapeDtypeStruct((n_batch, POOL_HIST, D_MODEL), F32)] + s_shape
    out_specs = [pl.BlockSpec((TM_IN, TN_IN), lambda c, i: (i, c)),
                 pl.BlockSpec((1, POOL_HIST, TN_IN), lambda c, i: (i // tiles_per_seq, 0, c))] + s_specs
    aliases = {}
    if new_state is not None:
        aliases[len(args)] = 3
        in_specs.append(pl.BlockSpec(memory_space=pl.ANY))
        args.append(new_state)
    return pl.pallas_call(
        functools.partial(_proj_pool_kernel, tiles_per_seq=tiles_per_seq, n_new=sample[5]),
        out_shape=tuple(out_shape),
        grid=(D_MODEL // TN_IN, tiles),
        in_specs=in_specs,
        out_specs=tuple(out_specs),
        scratch_shapes=[pltpu.VMEM((N_META, TN_IN), F32)],
        input_output_aliases=aliases,
        compiler_params=_cparams("arbitrary", "arbitrary"),
        name="projection_pool",
    )(*args)


LANE = 128
LANE_TILES = POOL_GROUP // LANE


def _pool_sample_kernel(hist_ref, *rest, n_seq, n_new):
    u_refs = rest[:LANE_TILES]
    pg_ref, pw_ref, ps_ref = rest[LANE_TILES:LANE_TILES + 3]
    o_ref, nh_ref, pooled_scr = rest[-3 - LANE_TILES:-LANE_TILES]
    mix_scrs = rest[-LANE_TILES:]
    g = pl.program_id(0)

    def token_rows(t):
        return pl.ds(t, n_seq, stride=n_new)

    u_t = [jnp.concatenate([r[token_rows(t), :] for r in u_refs], axis=1) for t in range(n_new)]
    ext = [hist_ref[j] for j in range(POOL_HIST)] + u_t
    for j in range(POOL_HIST):
        nh_ref[j] = ext[n_new + j]
    for k, w in enumerate(POOL_WINDOWS):
        @pl.when(g == k)
        def _(w=w):
            s = ext
            size = 1
            while size < w:
                s = [s[i + size] + s[i] for i in range(len(s) - size)]
                size *= 2
            for t in range(n_new):
                pooled_scr[t * n_seq:(t + 1) * n_seq] = s[POOL_HIST + 1 + t - w] * (1.0 / w) - u_t[t]
    mixed = _dot(pooled_scr[...].astype(BF16), pw_ref[...])
    for t in range(n_new):
        for c, scr in enumerate(mix_scrs):
            scr[token_rows(t), :] = mixed[t * n_seq:(t + 1) * n_seq, c * LANE:(c + 1) * LANE]
    mixed = jnp.concatenate([scr[...] for scr in mix_scrs], axis=1)
    o_ref[...] = (mixed * ps_ref[...] * _silu(pg_ref[...])).astype(BF16)


def _pool_sample(zs, state_pool_t, a_buf, new_hist, pool_w, pool_scale, layer, n_seq, n_new, row0):
    rows = n_seq * n_new
    assert row0 % rows == 0
    hist_blk = (None, POOL_HIST, n_seq, POOL_GROUP)
    hist_map = lambda g: (layer, 0, 0, g)
    per_seg = D_MODEL // POOL_GROUP

    def u_tile(c):
        return pl.BlockSpec((rows, LANE), lambda g: (0, (SEG_U * per_seg + g) * LANE_TILES + c))

    in_specs = [pl.BlockSpec(hist_blk, hist_map)] + [u_tile(c) for c in range(LANE_TILES)]
    in_specs += [pl.BlockSpec((rows, POOL_GROUP), lambda g: (0, SEG_PG * per_seg + g)),
                 pl.BlockSpec((None, None, POOL_GROUP, POOL_GROUP), lambda g: (layer, g, 0, 0)),
                 pl.BlockSpec((1, POOL_GROUP), lambda g: (0, g)),
                 pl.BlockSpec(memory_space=pl.ANY)]
    args = [state_pool_t] + [zs] * LANE_TILES + [zs, pool_w, pool_scale, a_buf]
    aliases = {len(args) - 1: 0}
    if new_hist is not None:
        aliases[len(args)] = 1
        in_specs.append(pl.BlockSpec(memory_space=pl.ANY))
        args.append(new_hist)
    return pl.pallas_call(
        functools.partial(_pool_sample_kernel, n_seq=n_seq, n_new=n_new),
        out_shape=(jax.ShapeDtypeStruct(a_buf.shape, BF16),
                   jax.ShapeDtypeStruct(state_pool_t.shape, F32)),
        grid=(N_GROUPS,),
        in_specs=in_specs,
        out_specs=(pl.BlockSpec((rows, POOL_GROUP), lambda g: (row0 // rows, g)),
                   pl.BlockSpec(hist_blk, hist_map)),
        scratch_shapes=[pltpu.VMEM((rows, POOL_GROUP), F32)] + [pltpu.VMEM((rows, LANE), F32)] * LANE_TILES,
        input_output_aliases=aliases,
        compiler_params=_cparams("arbitrary"),
        name="pool_sample",
    )(*args)


def _ret_kernel(dec_ref, decay_ref, q_ref, k_ref, v_ref, rg_ref, gn_ref, s0_ref, *rest, n_valid):
    o_ref, sfin_ref, s_scr = rest[-3:]
    c = pl.program_id(1)

    @pl.when(c == 0)
    def _():
        s_scr[...] = s0_ref[...]

    ridx = lax.broadcasted_iota(jnp.int32, (CHUNK, 1), 0).astype(F32)
    for h in range(HEADS):
        hs = slice(h * HEAD_DIM, (h + 1) * HEAD_DIM)
        lg = dec_ref[0, h]
        q = q_ref[:, hs].astype(BF16)
        k = k_ref[:, hs]
        v = v_ref[:, hs].astype(BF16)
        s_old = s_scr[h]
        scores = _dot_nt(q, k.astype(BF16)) * decay_ref[h]
        intra = _dot(scores.astype(BF16), v)
        inter = _dot(q, s_old.astype(BF16)) * jnp.exp((ridx + 1.0) * lg)
        k_dec = (k * jnp.exp((n_valid - 1.0 - ridx) * lg)).astype(BF16)
        s_scr[h] = s_old * dec_ref[1, h] + _dot_tn(k_dec, v)
        o_ref[:, hs] = _head_norm_gate(intra + inter, gn_ref[:, hs], rg_ref[:, hs])

    @pl.when(c == pl.num_programs(1) - 1)
    def _():
        sfin_ref[...] = s_scr[...]


def _retention(srcs, lg, gn, decay, s0, b_buf, fin, fin_shape, fin_index, n_batch, n_chunks, n_valid):
    dec = jnp.stack([lg, jnp.exp(n_valid * lg)])

    def zmap(col):
        return lambda b, c: (b * n_chunks + c, col)

    in_specs = [pl.BlockSpec(memory_space=pltpu.SMEM),
                pl.BlockSpec((HEADS, CHUNK, CHUNK), lambda b, c: (0, 0, 0))]
    in_specs += [pl.BlockSpec((CHUNK, D_MODEL), zmap(col)) for _, col in srcs]
    in_specs += [pl.BlockSpec((1, D_MODEL), lambda b, c: (0, 0)),
                 pl.BlockSpec(_STATE, lambda b, c: (0, 0, 0))]
    args = [dec, decay] + [a for a, _ in srcs] + [gn, s0]
    aliases = {}
    for out_idx, buf in enumerate((b_buf, fin)):
        if buf is not None:
            aliases[len(args)] = out_idx
            in_specs.append(pl.BlockSpec(memory_space=pl.ANY))
            args.append(buf)
    fin_blk = (None,) * (len(fin_shape) - 3) + _STATE
    b_rows = n_batch * n_chunks * CHUNK if b_buf is None else b_buf.shape[0]
    return pl.pallas_call(
        functools.partial(_ret_kernel, n_valid=n_valid),
        out_shape=(jax.ShapeDtypeStruct((b_rows, D_MODEL), BF16),
                   jax.ShapeDtypeStruct(fin_shape, F32)),
        grid=(n_batch, n_chunks),
        in_specs=in_specs,
        out_specs=(pl.BlockSpec((CHUNK, D_MODEL), zmap(0)),
                   pl.BlockSpec(fin_blk, lambda b, c: fin_index(b) + (0, 0, 0))),
        scratch_shapes=[pltpu.VMEM(_STATE, F32)],
        input_output_aliases=aliases,
        compiler_params=_cparams("arbitrary", "arbitrary"),
        name="retention",
    )(*args)


TN_MERGE = 1024


def _merge_kernel(a_ref, b_ref, gpp_ref, grp_ref, gps_ref, grs_ref,
                  am_ref, bm_ref, gpm_ref, grm_ref, wp_ref, wr_ref, o_ref, om_ref, *, n_p, n_s):
    i = pl.program_id(1)

    def tile(a, b, gp, gr, out):
        pool_branch = _dot(a[...], wp_ref[...])
        ret_branch = _dot(b[...], wr_ref[...])
        out[...] = (jax.nn.sigmoid(gp[...]) * pool_branch + jax.nn.sigmoid(gr[...]) * ret_branch).astype(BF16)

    @pl.when(i < n_p)
    def _():
        tile(a_ref, b_ref, gpp_ref, grp_ref, o_ref)

    @pl.when((i >= n_p) & (i < n_p + n_s))
    def _():
        tile(a_ref, b_ref, gps_ref, grs_ref, o_ref)

    @pl.when(i == n_p + n_s)
    def _():
        tile(am_ref, bm_ref, gpm_ref, grm_ref, om_ref)


def _merge(a_in, b_in, z, zs, a_meta, b_meta, zm, proj_pool, proj_ret):
    n_p = z.shape[0] // TM
    n_s = zs.shape[0] // TM
    n_main = n_p + n_s
    assert a_in.shape[0] == n_main * TM and b_in.shape[0] == n_main * TM
    nt = D_MODEL // TN_MERGE

    def p_map(col):
        return lambda j, i: (jnp.minimum(i, n_p - 1), col(j))

    def s_map(col):
        return lambda j, i: (jnp.clip(i - n_p, 0, n_s - 1), col(j))

    row = pl.BlockSpec((TM, D_MODEL), lambda j, i: (jnp.minimum(i, n_main - 1), 0))
    gate = lambda ref_map, seg: pl.BlockSpec((TM, TN_MERGE), ref_map(lambda j: seg * nt + j))
    const = lambda j, i: (0, 0)
    return pl.pallas_call(
        functools.partial(_merge_kernel, n_p=n_p, n_s=n_s),
        out_shape=(jax.ShapeDtypeStruct((n_main * TM, D_MODEL), BF16),
                   jax.ShapeDtypeStruct((CHUNK, D_MODEL), BF16)),
        grid=(nt, n_main + 1),
        in_specs=[row, row,
                  gate(p_map, COL_GP), gate(p_map, COL_GR), gate(s_map, SEG_GP), gate(s_map, SEG_GR),
                  pl.BlockSpec((CHUNK, D_MODEL), const),
                  pl.BlockSpec((CHUNK, D_MODEL), const),
                  pl.BlockSpec((CHUNK, TN_MERGE), lambda j, i: (0, SEG_GP * nt + j)),
                  pl.BlockSpec((CHUNK, TN_MERGE), lambda j, i: (0, SEG_GR * nt + j)),
                  pl.BlockSpec((D_MODEL, TN_MERGE), lambda j, i: (0, j)),
                  pl.BlockSpec((D_MODEL, TN_MERGE), lambda j, i: (0, j))],
        out_specs=(pl.BlockSpec((TM, TN_MERGE), lambda j, i: (jnp.minimum(i, n_main - 1), j)),
                   pl.BlockSpec((CHUNK, TN_MERGE), lambda j, i: (0, j))),
        compiler_params=_cparams("arbitrary", "arbitrary"),
        name="merge_branches",
    )(a_in, b_in, z, z, zs, zs, a_meta, b_meta, zm, zm, proj_pool, proj_ret)


def _out_kernel(m_ref, mm_ref, w_ref, g_ref, *rest, first, last, n_p, n_s):
    rest = list(rest)
    x_refs = [rest.pop(0) for _ in range(3 if first else 2)]
    i = pl.program_id(0)

    def emit(x, merged, outs):
        xn = x + _dot(merged[...], w_ref[...])
        y = _rms(xn, g_ref[...])
        if last:
            outs[0][...] = y
        else:
            outs[0][...] = xn
            outs[1][...] = y.astype(BF16)

    main_outs = [rest[0:1], rest[1:2]] if last else [rest[0:2], rest[0:2]]

    @pl.when(i < n_p)
    def _():
        emit(x_refs[0][...], m_ref, main_outs[0])

    @pl.when((i >= n_p) & (i < n_p + n_s))
    def _():
        emit(x_refs[1 if first else 0][...], m_ref, main_outs[1])

    if not last:
        @pl.when(i == n_p + n_s)
        def _():
            emit(_meta_rows(x_refs[2]) if first else x_refs[1][...], mm_ref, rest[2:4])


def _out_projection(merged, merged_meta, w_out, g_next, x_parts, first, last, n_p, n_s):
    n = n_p + n_s
    main_spec = pl.BlockSpec((TM, D_MODEL), lambda i: (jnp.minimum(i, n - 1), 0))
    meta_spec = pl.BlockSpec((CHUNK, D_MODEL), lambda i: (0, 0))
    in_specs = [main_spec, meta_spec,
                pl.BlockSpec((D_MODEL, D_MODEL), lambda i: (0, 0)),
                pl.BlockSpec((1, D_MODEL), lambda i: (0, 0))]
    if first:
        in_specs += _x_specs(n_p, n_s) + [pl.BlockSpec((N_META, D_MODEL), lambda i: (0, 0))]
    else:
        in_specs += [main_spec, meta_spec]
    if last:
        out_shape = (jax.ShapeDtypeStruct((n_p * TM, D_MODEL), F32),
                     jax.ShapeDtypeStruct((n_s * TM, D_MODEL), F32))
        out_specs = tuple(_x_specs(n_p, n_s))
    else:
        out_shape = (jax.ShapeDtypeStruct((n * TM, D_MODEL), F32), jax.ShapeDtypeStruct((n * TM, D_MODEL), BF16),
                     jax.ShapeDtypeStruct((CHUNK, D_MODEL), F32), jax.ShapeDtypeStruct((CHUNK, D_MODEL), BF16))
        out_specs = (main_spec, main_spec, meta_spec, meta_spec)
    return pl.pallas_call(
        functools.partial(_out_kernel, first=first, last=last, n_p=n_p, n_s=n_s),
        out_shape=out_shape,
        grid=(n if last else n + 1,),
        in_specs=in_specs,
        out_specs=out_specs,
        compiler_params=_cparams("arbitrary"),
        name="out_projection",
    )(merged, merged_meta, w_out, g_next, *x_parts)


def _log_decay():
    return jnp.log1p(-jnp.exp2(-5.0 - jnp.arange(HEADS, dtype=F32)))


def _rope_tables(pos):
    inv_freq = ROPE_BASE ** (-jnp.arange(HALF, dtype=F32) / HALF)
    ang = pos[:, None] * inv_freq[None, :]
    return jnp.cos(ang), jnp.sin(ang)


def _chunk_decay(lg):
    idx = jnp.arange(CHUNK, dtype=F32)
    diff = idx[:, None] - idx[None, :]
    causal = diff >= 0
    return jnp.where(causal[None], jnp.exp(jnp.where(causal, diff, 0.0)[None] * lg[:, None, None]), 0.0)


def _sample_decay_tables(lg, n_new):
    r = jnp.arange(RB * HEADS * n_new)
    head = (r // n_new) % HEADS
    tok = (r % n_new).astype(F32)
    lg_r = lg[head]
    same = (r[:, None] // n_new) == (r[None, :] // n_new)
    diff = tok[:, None] - tok[None, :]
    keep = same & (diff >= 0)
    dmask = jnp.where(keep, jnp.exp(jnp.where(keep, diff, 0.0) * lg_r[:, None]), 0.0)
    qdec = jnp.broadcast_to(jnp.exp((tok + 1.0) * lg_r)[:, None], (r.shape[0], HEAD_DIM))
    kdec = jnp.broadcast_to(jnp.exp((n_new - 1.0 - tok) * lg_r)[:, None], (r.shape[0], HEAD_DIM))
    return dmask, qdec, kdec


def kernel(x_prompt, x_sample, state_pool, state_ret, meta_tokens, norm_gain, w_in, pool_w, pool_scale,
           ret_gn_gain, proj_pool, proj_ret, w_out, final_norm):
    n_batch, seq = x_prompt.shape[:2]
    n_seq, n_new = x_sample.shape[:2]
    depth = norm_gain.shape[0]
    rows_prompt = n_batch * seq
    rows_sample = n_seq * n_new
    assert seq % CHUNK == 0 and n_seq % RB == 0
    assert n_new == 8, "sample tokens of one sequence must fill one f32 sublane tile"
    assert rows_prompt % TM == 0 and rows_sample % TM == 0
    n_p = rows_prompt // TM
    n_s = rows_sample // TM

    xp = x_prompt.reshape(rows_prompt, D_MODEL).astype(F32)
    xs = x_sample.reshape(rows_sample, D_MODEL).astype(F32)
    meta = meta_tokens.astype(F32)
    state_pool = state_pool.astype(F32)
    state_ret = state_ret.astype(F32)
    w_in = w_in.astype(F32)

    trig_p = _rope_tables(N_META + jnp.arange(seq, dtype=F32))
    trig_s = tuple(jnp.tile(t, (n_seq, 1)) for t in _rope_tables(PAST_LEN + jnp.arange(n_new, dtype=F32)))
    trig_m = _rope_tables(jnp.where(jnp.arange(CHUNK) < N_META, jnp.arange(CHUNK, dtype=F32), 0.0))
    lg = _log_decay()
    decay = _chunk_decay(lg)
    gam_new = jnp.exp(n_new * lg)
    tables = _sample_decay_tables(lg, n_new)
    zero_state = jnp.zeros(_STATE, F32)
    row = lambda a: a.reshape(1, D_MODEL).astype(F32)

    h, h_meta = _first_norm(xp, xs, meta, row(norm_gain[0]), n_p, n_s)
    x_parts = (xp, xs, meta)
    pool_p = []
    ret_p = pool_s = ret_s = None
    pw = pool_w.astype(BF16)
    out_weights = (proj_pool.astype(F32), proj_ret.astype(F32), w_out.astype(F32))
    state_pool_t = jnp.transpose(state_pool, (0, 2, 1, 3))
    for l in range(depth):
        ps = row(pool_scale[l])
        gn = row(ret_gn_gain[l])

        zs, zm, wb = _in_projection_small(h, h_meta, w_in, l, trig_s, trig_m, rows_prompt, rows_sample)
        zr, wp_l, wr_l, wo_l = _in_projection(h, wb, rows_prompt, ROPE_SEGS, trig=trig_p, seq=seq,
                                              convert=(out_weights, l))
        sample = (gam_new, gn, tables, state_ret, l, n_new, zs, rows_prompt)
        a_in, hist_p, b_in, ret_s = _proj_pool(h, wb, zm, pw, ps, l, n_batch, seq, sample, ret_s)
        pool_groups = (D_MODEL // TN_IN) * (rows_prompt // TM_IN)
        z, b_in, ret_s = _in_projection(h, wb, rows_prompt, PLAIN_SEGS,
                                        sample=(sample, pool_groups, b_in, ret_s))

        a_meta = _pool_meta(zm, pw, ps, l)
        a_in, pool_s = _pool_sample(zs, state_pool_t, a_in, pool_s, pw, ps, l, n_seq, n_new, rows_prompt)
        pool_p.append(hist_p)

        b_meta, s_meta = _retention(((zm, SEG_Q), (zm, SEG_K), (zm, SEG_V), (zm, SEG_RG)), lg, gn, decay,
                                    zero_state, None, None, _STATE, lambda b: (), 1, 1, N_META)
        b_in, ret_p = _retention(((zr, COL_Q), (zr, COL_K), (z, COL_V), (z, COL_RG)), lg, gn, decay,
                                 s_meta, b_in, ret_p, (depth, n_batch) + _STATE, lambda b, l=l: (l, b),
                                 n_batch, seq // CHUNK, CHUNK)

        merged, merged_meta = _merge(a_in, b_in, z, zs, a_meta, b_meta, zm, wp_l, wr_l)
        last = l == depth - 1
        g_next = row(final_norm if last else norm_gain[l + 1])
        outs = _out_projection(merged, merged_meta, wo_l, g_next, x_parts, l == 0, last, n_p, n_s)
        if last:
            y_prompt, y_sample = outs
        else:
            x_new, h, x_meta, h_meta = outs
            x_parts = (x_new, x_meta)

    return (y_prompt.reshape(n_batch, seq, D_MODEL), y_sample.reshape(n_seq, n_new, D_MODEL),
            jnp.stack(pool_p), ret_p, jnp.transpose(pool_s, (0, 2, 1, 3)), ret_s)
```

```python
import functools

import jax
import jax.numpy as jnp
from jax import lax
from jax.experimental import pallas as pl
from jax.experimental.pallas import tpu as pltpu

F32 = jnp.float32
BF16 = jnp.bfloat16

D_MODEL = 2048
N_META = 16
POOL_WINDOWS = (2, 4, 8, 16)
N_GROUPS = len(POOL_WINDOWS)
POOL_GROUP = D_MODEL // N_GROUPS
POOL_HIST = max(POOL_WINDOWS) - 1
HEADS = 8
HEAD_DIM = D_MODEL // HEADS
HALF = HEAD_DIM // 2
CHUNK = 128
ROPE_BASE = 10000.0
EPS = 1e-6
PAST_LEN = 16384
SEG_U, SEG_PG, SEG_Q, SEG_K, SEG_V, SEG_RG, SEG_GP, SEG_GR = range(8)
N_SEG = 8
ROPE_SEGS = (SEG_Q, SEG_K)
COL_Q, COL_K = range(len(ROPE_SEGS))
VALUE_SEGS = (SEG_V, SEG_RG)
COL_V, COL_RG = range(len(VALUE_SEGS))
GATE_SEGS = (SEG_GP, SEG_GR)
COL_GP, COL_GR = range(len(GATE_SEGS))

TM = 512
TM_IN = 1024
TN_IN = 1024
PER_SEG = D_MODEL // TN_IN
VMEM_LIMIT = 56 * 1024 * 1024


def _cparams(*sem, **kw):
    return pltpu.CompilerParams(dimension_semantics=sem, vmem_limit_bytes=VMEM_LIMIT, **kw)


def _dot(a, b):
    return jnp.dot(a, b, preferred_element_type=F32)


def _dot_nt(a, b):
    return lax.dot_general(a, b, (((1,), (1,)), ((), ())), preferred_element_type=F32)


def _dot_tn(a, b):
    return lax.dot_general(a, b, (((0,), (0,)), ((), ())), preferred_element_type=F32)


def _rms(x, g):
    return x * lax.rsqrt(jnp.mean(x * x, axis=-1, keepdims=True) + EPS) * g


def _silu(x):
    return x * jax.nn.sigmoid(x)


def _x_specs(n_p, n_s):
    return [pl.BlockSpec((TM, D_MODEL), lambda i: (jnp.minimum(i, n_p - 1), 0)),
            pl.BlockSpec((TM, D_MODEL), lambda i: (jnp.clip(i - n_p, 0, n_s - 1), 0))]


def _meta_rows(meta_ref):
    return jnp.concatenate([meta_ref[...], jnp.zeros((CHUNK - N_META, D_MODEL), F32)], axis=0)


def _norm_kernel(xp_ref, xs_ref, meta_ref, g_ref, h_ref, hm_ref, *, n_p, n_s):
    i = pl.program_id(0)

    @pl.when(i < n_p)
    def _():
        h_ref[...] = _rms(xp_ref[...], g_ref[...]).astype(BF16)

    @pl.when((i >= n_p) & (i < n_p + n_s))
    def _():
        h_ref[...] = _rms(xs_ref[...], g_ref[...]).astype(BF16)

    @pl.when(i == n_p + n_s)
    def _():
        hm_ref[...] = _rms(_meta_rows(meta_ref), g_ref[...]).astype(BF16)


def _first_norm(xp, xs, meta, g, n_p, n_s):
    n = n_p + n_s
    return pl.pallas_call(
        functools.partial(_norm_kernel, n_p=n_p, n_s=n_s),
        out_shape=(jax.ShapeDtypeStruct((n * TM, D_MODEL), BF16),
                   jax.ShapeDtypeStruct((CHUNK, D_MODEL), BF16)),
        grid=(n + 1,),
        in_specs=_x_specs(n_p, n_s) + [pl.BlockSpec((N_META, D_MODEL), lambda i: (0, 0)),
                                       pl.BlockSpec((1, D_MODEL), lambda i: (0, 0))],
        out_specs=(pl.BlockSpec((TM, D_MODEL), lambda i: (jnp.minimum(i, n - 1), 0)),
                   pl.BlockSpec((CHUNK, D_MODEL), lambda i: (0, 0))),
        compiler_params=_cparams("arbitrary"),
        name="first_norm",
    )(xp, xs, meta, g)


RB = 2
_STATE = (HEADS, HEAD_DIM, HEAD_DIM)


def _head_norm_gate(o, gain, gate):
    mu = jnp.mean(o, axis=-1, keepdims=True)
    oc = o - mu
    on = oc * lax.rsqrt(jnp.mean(oc * oc, axis=-1, keepdims=True) + EPS)
    return ((on * gain) * _silu(gate)).astype(BF16)


def _sample_body(gam_ref, q_ref, k_ref, v_ref, rg_ref, gn_ref, dmask_ref, qdec_ref, kdec_ref, s_ref,
                 o_ref, sout_ref, n_new):
    pairs = [(b, h) for b in range(RB) for h in range(HEADS)]

    def stack(ref):
        return jnp.concatenate(
            [ref[b * n_new:(b + 1) * n_new, h * HEAD_DIM:(h + 1) * HEAD_DIM] for b, h in pairs], axis=0)

    k = stack(k_ref)
    qb = stack(q_ref).astype(BF16)
    v = stack(v_ref)
    vb = v.astype(BF16)
    scores = _dot_nt(qb, k.astype(BF16)) * dmask_ref[...]
    intra = _dot(scores.astype(BF16), vb)
    k_dec = (k * kdec_ref[...]).astype(BF16)
    rows = lax.broadcasted_iota(jnp.int32, v.shape, 0)
    is_even = (rows & n_new) == 0
    v_even = jnp.where(is_even, v, 0.0).astype(BF16)
    v_odd = jnp.where(is_even, 0.0, v).astype(BF16)
    grp = 2 * n_new
    inter_parts = []
    for idx, (b, h) in enumerate(pairs):
        g0 = (idx // 2) * grp
        off = (idx % 2) * n_new
        s_old = s_ref[b, h]
        inter = _dot(qb[g0:g0 + grp], s_old.astype(BF16))
        inter_parts.append(inter[off:off + n_new])
        v_sel = v_even if idx % 2 == 0 else v_odd
        sout_ref[b, h] = s_old * gam_ref[h] + _dot_tn(k_dec[g0:g0 + grp], v_sel[g0:g0 + grp])
    o = intra + jnp.concatenate(inter_parts, axis=0) * qdec_ref[...]
    gain = jnp.concatenate(
        [jnp.broadcast_to(gn_ref[:, h * HEAD_DIM:(h + 1) * HEAD_DIM], (n_new, HEAD_DIM)) for _, h in pairs],
        axis=0)
    out = _head_norm_gate(o, gain, stack(rg_ref)).astype(F32)
    seqs = []
    for b in range(RB):
        seqs.append(jnp.concatenate(
            [out[(b * HEADS + h) * n_new:(b * HEADS + h + 1) * n_new] for h in range(HEADS)], axis=1))
    o_ref[...] = jnp.concatenate(seqs, axis=0).astype(BF16)


def _rope_store(z, o_ref, cos_ref, sin_ref, scale):
    cos = cos_ref[...]
    sin = sin_ref[...]
    for hh in range(TN_IN // HEAD_DIM):
        lo = slice(hh * HEAD_DIM, hh * HEAD_DIM + HALF)
        hi = slice(hh * HEAD_DIM + HALF, (hh + 1) * HEAD_DIM)
        t1 = z[:, lo]
        t2 = z[:, hi]
        o_ref[:, lo] = (t1 * cos - t2 * sin) * scale
        o_ref[:, hi] = (t2 * cos + t1 * sin) * scale


def _k_scale(is_k):
    return jnp.where(is_k, HEAD_DIM ** -0.5, 1.0).astype(F32)


def _inproj_small_kernel(hs_ref, hm_ref, w_ref, cos_ref, sin_ref, cosm_ref, sinm_ref, zs_ref, zm_ref, wb_ref):
    s = pl.program_id(0)
    wb_ref[...] = w_ref[...].astype(BF16)
    zs_ref[...] = _dot(hs_ref[...], wb_ref[...])
    zm_ref[...] = _dot(hm_ref[...], wb_ref[...])

    @pl.when((s >= SEG_Q * PER_SEG) & (s < (SEG_K + 1) * PER_SEG))
    def _():
        scale = _k_scale(s >= SEG_K * PER_SEG)
        _rope_store(zs_ref[...], zs_ref, cos_ref, sin_ref, scale)
        _rope_store(zm_ref[...], zm_ref, cosm_ref, sinm_ref, scale)


def _in_projection_small(h, h_meta, w_in, layer, trig_s, trig_m, row0, rows):
    assert row0 % rows == 0
    const = lambda s: (0, 0)
    col = lambda s: (0, s)
    return pl.pallas_call(
        _inproj_small_kernel,
        out_shape=(jax.ShapeDtypeStruct((rows, N_SEG * D_MODEL), F32),
                   jax.ShapeDtypeStruct((CHUNK, N_SEG * D_MODEL), F32),
                   jax.ShapeDtypeStruct((D_MODEL, N_SEG * D_MODEL), BF16)),
        grid=(N_SEG * PER_SEG,),
        in_specs=[pl.BlockSpec((rows, D_MODEL), lambda s: (row0 // rows, 0)),
                  pl.BlockSpec((CHUNK, D_MODEL), const),
                  pl.BlockSpec((None, D_MODEL, TN_IN), lambda s: (layer, 0, s)),
                  pl.BlockSpec((rows, HALF), const),
                  pl.BlockSpec((rows, HALF), const),
                  pl.BlockSpec((CHUNK, HALF), const),
                  pl.BlockSpec((CHUNK, HALF), const)],
        out_specs=(pl.BlockSpec((rows, TN_IN), col),
                   pl.BlockSpec((CHUNK, TN_IN), col),
                   pl.BlockSpec((D_MODEL, TN_IN), col)),
        compiler_params=_cparams("arbitrary"),
        name="in_projection_small",
    )(h, h_meta, w_in, *trig_s, *trig_m)


N_SAMPLE_IN = 10


def _sample_operands(sample, group_of):
    gam, gn, (dmask, qdec, kdec), state_ret, layer, n_new, zs, prompt_rows = sample
    rows_g = RB * n_new
    stack_rows = RB * HEADS * n_new
    assert prompt_rows % rows_g == 0

    def zmap(seg):
        return lambda *idx: (group_of(*idx), seg)

    const = lambda *idx: (0, 0)
    sblk = (None, RB) + _STATE
    state_map = lambda *idx: (layer, group_of(*idx), 0, 0, 0)
    in_specs = [pl.BlockSpec(memory_space=pltpu.SMEM),
                pl.BlockSpec((rows_g, D_MODEL), zmap(SEG_Q)),
                pl.BlockSpec((rows_g, D_MODEL), zmap(SEG_K)),
                pl.BlockSpec((rows_g, D_MODEL), zmap(SEG_V)),
                pl.BlockSpec((rows_g, D_MODEL), zmap(SEG_RG)),
                pl.BlockSpec((1, D_MODEL), const),
                pl.BlockSpec((stack_rows, stack_rows), const),
                pl.BlockSpec((stack_rows, HEAD_DIM), const),
                pl.BlockSpec((stack_rows, HEAD_DIM), const),
                pl.BlockSpec(sblk, state_map)]
    args = [gam, zs, zs, zs, zs, gn, dmask, qdec, kdec, state_ret]
    out_shape = [jax.ShapeDtypeStruct((prompt_rows + zs.shape[0], D_MODEL), BF16),
                 jax.ShapeDtypeStruct(state_ret.shape, F32)]
    out_specs = [pl.BlockSpec((rows_g, D_MODEL), lambda *idx: (prompt_rows // rows_g + group_of(*idx), 0)),
                 pl.BlockSpec(sblk, state_map)]
    return in_specs, args, out_shape, out_specs


N_CHUNK_IN = 8
CHUNKS_PER_STEP = 2


def _inproj_kernel(h_ref, w_ref, *rest, rope, n_convert, n_new, n_groups, chunk_steps):
    pos = 0
    if rope:
        cos_ref, sin_ref = rest[:2]
        pos = 2
    conv_in = rest[pos:pos + n_convert]
    pos += n_convert
    sample_refs = rest[pos:pos + N_SAMPLE_IN] if n_new is not None else None
    chunk_refs = rest[pos:pos + N_CHUNK_IN] if chunk_steps is not None else None
    n_scr = 0 if chunk_steps is None else 1
    n_out = 1 + n_convert + (0 if n_new is None else 2) + (0 if chunk_steps is None else 2)
    outs = rest[len(rest) - n_out - n_scr:len(rest) - n_scr]
    o_ref = outs[0]
    conv_out = outs[1:1 + n_convert]
    step = pl.program_id(0) * pl.num_programs(1) + pl.program_id(1)

    def tile():
        z = _dot(h_ref[...], w_ref[...])
        if rope:
            _rope_store(z, o_ref, cos_ref, sin_ref, _k_scale(pl.program_id(0) >= PER_SEG))
        else:
            o_ref[...] = z
        for src, dst in zip(conv_in, conv_out):
            dst[...] = src[...].astype(BF16)

    if chunk_steps is not None:
        dec_ref, decay_ref, q_ref, k_ref, v_ref, rg_ref, gn_ref, s0_ref = chunk_refs
        bo_ref, fin_ref = outs[-2:]
        s_scr = rest[-1]

        @pl.when(step % chunk_steps == 0)
        def _():
            s_scr[...] = s0_ref[...]

        tile()
        for j in range(CHUNKS_PER_STEP):
            _chunk_body(dec_ref, decay_ref, q_ref, k_ref, v_ref, rg_ref, gn_ref, s_scr, bo_ref, CHUNK,
                        rows=slice(j * CHUNK, (j + 1) * CHUNK))

        @pl.when(step % chunk_steps == chunk_steps - 1)
        def _():
            fin_ref[...] = s_scr[...]
    elif n_new is not None:
        so_ref, sout_ref = outs[-2:]

        @pl.when(step < n_groups)
        def _():
            tile()
            _sample_body(*sample_refs, so_ref, sout_ref, n_new)

        @pl.when(step >= n_groups)
        def _():
            tile()
    else:
        tile()


def _in_projection(h, wb, rows, segs, trig=None, seq=None, convert=None, sample=None, chunks=None):
    assert rows % TM_IN == 0 and (sample is None or chunks is None)
    tiles = rows // TM_IN
    steps = len(segs) * PER_SEG * tiles
    step_of = lambda s, i: s * tiles + i
    rope = trig is not None
    gap_at = next((j for j in range(1, len(segs)) if segs[j] != segs[j - 1] + 1), len(segs))
    gap = segs[gap_at] - segs[gap_at - 1] - 1 if gap_at < len(segs) else 0
    assert all(segs[j] == segs[0] + j + (gap if j >= gap_at else 0) for j in range(len(segs)))

    def wmap(s, i):
        j = s // PER_SEG
        seg = segs[0] + j + jnp.where(j >= gap_at, gap, 0)
        return (0, seg * PER_SEG + s % PER_SEG)

    in_specs = [pl.BlockSpec((TM_IN, D_MODEL), lambda s, i: (i, 0)),
                pl.BlockSpec((D_MODEL, TN_IN), wmap)]
    args = [h, wb]
    out_shape = [jax.ShapeDtypeStruct((rows, len(segs) * D_MODEL), F32)]
    out_specs = [pl.BlockSpec((TM_IN, TN_IN), lambda s, i: (i, s))]
    alias_bufs = []
    scratch = []
    n_convert = 0
    n_new = n_groups = chunk_steps = None
    if rope:
        assert seq % TM_IN == 0
        in_specs += [pl.BlockSpec((TM_IN, HALF), lambda s, i: (i % (seq // TM_IN), 0)) for _ in range(2)]
        args += list(trig)
    if convert is not None:
        weights, w_layer = convert
        n_convert = len(weights)
        assert D_MODEL % steps == 0
        w_rows = D_MODEL // steps
        in_specs += [pl.BlockSpec((None, w_rows, D_MODEL), lambda s, i: (w_layer, step_of(s, i), 0))
                     for _ in weights]
        args += list(weights)
        out_shape += [jax.ShapeDtypeStruct((D_MODEL, D_MODEL), BF16) for _ in weights]
        out_specs += [pl.BlockSpec((w_rows, D_MODEL), lambda s, i: (step_of(s, i), 0)) for _ in weights]
    if sample is not None:
        sample, first_group, n_groups, b_buf, new_state = sample
        n_new = sample[5]
        assert 0 < n_groups <= steps, "one group of sample sequences per grid step"

        def group(s, i):
            return first_group + jnp.minimum(step_of(s, i), n_groups - 1)

        s_in, s_args, s_shape, s_specs = _sample_operands(sample, group)
        in_specs += s_in
        args += s_args
        alias_bufs += [(len(out_shape), b_buf), (len(out_shape) + 1, new_state)]
        out_shape += s_shape
        out_specs += s_specs
    if chunks is not None:
        qk, vg, lg, gn, decay, s0, b_buf, fin, fin_shape, layer, n_batch, n_chunks = chunks
        assert steps * CHUNKS_PER_STEP == n_batch * n_chunks and n_chunks % CHUNKS_PER_STEP == 0
        chunk_steps = n_chunks // CHUNKS_PER_STEP
        crows = CHUNKS_PER_STEP * CHUNK

        def cmap(col):
            return lambda s, i: (step_of(s, i), col)

        in_specs += [pl.BlockSpec(memory_space=pltpu.SMEM),
                     pl.BlockSpec((HEADS, CHUNK, CHUNK), lambda s, i: (0, 0, 0)),
                     pl.BlockSpec((crows, D_MODEL), cmap(COL_Q)),
                     pl.BlockSpec((crows, D_MODEL), cmap(COL_K)),
                     pl.BlockSpec((crows, D_MODEL), cmap(COL_V)),
                     pl.BlockSpec((crows, D_MODEL), cmap(COL_RG)),
                     pl.BlockSpec((1, D_MODEL), lambda s, i: (0, 0)),
                     pl.BlockSpec(_STATE, lambda s, i: (0, 0, 0))]
        args += [jnp.stack([lg, jnp.exp(CHUNK * lg)]), decay, qk, qk, vg, vg, gn, s0]
        alias_bufs += [(len(out_shape), b_buf)]
        if fin is not None:
            alias_bufs += [(len(out_shape) + 1, fin)]
        out_shape += [jax.ShapeDtypeStruct(b_buf.shape, BF16), jax.ShapeDtypeStruct(fin_shape, F32)]
        out_specs += [pl.BlockSpec((crows, D_MODEL), cmap(0)),
                      pl.BlockSpec((None, None) + _STATE,
                                   lambda s, i: (layer, step_of(s, i) // chunk_steps, 0, 0, 0))]
        scratch = [pltpu.VMEM(_STATE, F32)]
    aliases = {}
    for out_idx, buf in alias_bufs:
        aliases[len(args)] = out_idx
        in_specs.append(pl.BlockSpec(memory_space=pl.ANY))
        args.append(buf)
    outs = pl.pallas_call(
        functools.partial(_inproj_kernel, rope=rope, n_convert=n_convert, n_new=n_new, n_groups=n_groups,
                          chunk_steps=chunk_steps),
        out_shape=tuple(out_shape),
        grid=(len(segs) * PER_SEG, tiles),
        in_specs=in_specs,
        out_specs=tuple(out_specs),
        scratch_shapes=scratch,
        input_output_aliases=aliases,
        compiler_params=_cparams("arbitrary", "arbitrary"),
        name="in_projection_" + "_".join(str(g) for g in segs),
    )(*args)
    return outs if len(outs) > 1 else outs[0]


def _window_sum(ext, w, base, rows):
    s = ext
    size = 1
    while size < w:
        s = s[size:] + s[:-size]
        size *= 2
    start = base - (w - 1)
    return s[start:start + rows]


def _pool_group(g, wsum, u, inv_cnt, pw_ref, ps_ref, gate):
    sl = slice(g * POOL_GROUP, (g + 1) * POOL_GROUP)
    pooled = wsum * inv_cnt - u
    mixed = _dot(pooled.astype(BF16), pw_ref[g])
    return (mixed * ps_ref[:, sl] * _silu(gate)).astype(BF16)


def _pool_w_spec(layer):
    return pl.BlockSpec((None, N_GROUPS, POOL_GROUP, POOL_GROUP), lambda *_: (layer, 0, 0, 0))


def _pool_meta_kernel(u_ref, pg_ref, pw_ref, ps_ref, o_ref):
    avail = lax.broadcasted_iota(jnp.int32, (N_META, 1), 0).astype(F32) + 1.0
    o_ref[N_META:] = jnp.zeros((CHUNK - N_META, D_MODEL), BF16)
    for g, w in enumerate(POOL_WINDOWS):
        sl = slice(g * POOL_GROUP, (g + 1) * POOL_GROUP)
        u = u_ref[:, sl]
        wsum = _window_sum(jnp.concatenate([jnp.zeros_like(u), u], axis=0), w, N_META, N_META)
        inv_cnt = 1.0 / jnp.minimum(float(w), avail)
        o_ref[0:N_META, sl] = _pool_group(g, wsum, u, inv_cnt, pw_ref, ps_ref, pg_ref[:, sl])


def _pool_meta(zm, pool_w, pool_scale, layer):
    return pl.pallas_call(
        _pool_meta_kernel,
        out_shape=jax.ShapeDtypeStruct((CHUNK, D_MODEL), BF16),
        grid=(1,),
        in_specs=[pl.BlockSpec((N_META, D_MODEL), lambda i: (0, SEG_U)),
                  pl.BlockSpec((N_META, D_MODEL), lambda i: (0, SEG_PG)),
                  _pool_w_spec(layer),
                  pl.BlockSpec((1, D_MODEL), lambda i: (0, 0))],
        out_specs=pl.BlockSpec((CHUNK, D_MODEL), lambda i: (0, 0)),
        compiler_params=_cparams("arbitrary"),
        name="pool_meta",
    )(zm, zm, pool_w, pool_scale)


GROUPS_PER_BLOCK = TN_IN // POOL_GROUP


def _proj_pool_kernel(h_ref, wu_ref, wpg_ref, meta_ref, pw_ref, ps_ref, *rest, tiles_per_seq, n_new):
    sample_refs = rest[:N_SAMPLE_IN]
    o_ref, hist_ref, so_ref, sout_ref, carry = rest[-5:]
    c = pl.program_id(0)
    i = pl.program_id(1)
    first = (i % tiles_per_seq) == 0
    last = (i % tiles_per_seq) == tiles_per_seq - 1
    for cc in range(D_MODEL // TN_IN):
        @pl.when(c == cc)
        def _(cc=cc):
            u = _dot(h_ref[...], wu_ref[...])
            pg = _dot(h_ref[...], wpg_ref[...])
            prev = jnp.where(first, meta_ref[...], carry[...])
            for gg in range(GROUPS_PER_BLOCK):
                w = POOL_WINDOWS[cc * GROUPS_PER_BLOCK + gg]
                sl = slice(gg * POOL_GROUP, (gg + 1) * POOL_GROUP)
                wsum = _window_sum(jnp.concatenate([prev[:, sl], u[:, sl]], axis=0), w, N_META, TM_IN)
                pooled = wsum * (1.0 / w) - u[:, sl]
                mixed = _dot(pooled.astype(BF16), pw_ref[gg])
                o_ref[:, sl] = (mixed * ps_ref[:, sl] * _silu(pg[:, sl])).astype(BF16)
            carry[...] = u[TM_IN - N_META:]
            _sample_body(*sample_refs, so_ref, sout_ref, n_new)

            @pl.when(last)
            def _():
                hist_ref[0] = u[TM_IN - POOL_HIST:]


def _proj_pool(h, wb, zm, pool_w, pool_scale, layer, n_batch, seq, sample, new_state):
    assert seq % TM_IN == 0 and TN_IN % POOL_GROUP == 0
    tiles_per_seq = seq // TM_IN
    tiles = n_batch * tiles_per_seq
    step = lambda c, i: c * tiles + i
    in_specs = [pl.BlockSpec((TM_IN, D_MODEL), lambda c, i: (i, 0)),
                pl.BlockSpec((D_MODEL, TN_IN), lambda c, i: (0, SEG_U * PER_SEG + c)),
                pl.BlockSpec((D_MODEL, TN_IN), lambda c, i: (0, SEG_PG * PER_SEG + c)),
                pl.BlockSpec((N_META, TN_IN), lambda c, i: (0, SEG_U * PER_SEG + c)),
                pl.BlockSpec((None, GROUPS_PER_BLOCK, POOL_GROUP, POOL_GROUP), lambda c, i: (layer, c, 0, 0)),
                pl.BlockSpec((1, TN_IN), lambda c, i: (0, c))]
    args = [h, wb, wb, zm, pool_w, pool_scale]
    s_in, s_args, s_shape, s_specs = _sample_operands(sample, step)
    in_specs += s_in
    args += s_args
    total_rows = s_shape[0].shape[0]
    out_shape = [jax.ShapeDtypeStruct((total_rows, D_MODEL), BF16),
                 jax.ShapeDtypeStruct((n_batch, POOL_HIST, D_MODEL), F32)] + s_shape
    out_specs = [pl.BlockSpec((TM_IN, TN_IN), lambda c, i: (i, c)),
                 pl.BlockSpec((1, POOL_HIST, TN_IN), lambda c, i: (i // tiles_per_seq, 0, c))] + s_specs
    aliases = {}
    if new_state is not None:
        aliases[len(args)] = 3
        in_specs.append(pl.BlockSpec(memory_space=pl.ANY))
        args.append(new_state)
    return pl.pallas_call(
        functools.partial(_proj_pool_kernel, tiles_per_seq=tiles_per_seq, n_new=sample[5]),
        out_shape=tuple(out_shape),
        grid=(D_MODEL // TN_IN, tiles),
        in_specs=in_specs,
        out_specs=tuple(out_specs),
        scratch_shapes=[pltpu.VMEM((N_META, TN_IN), F32)],
        input_output_aliases=aliases,
        compiler_params=_cparams("arbitrary", "arbitrary"),
        name="projection_pool",
    )(*args)


LANE = 128
LANE_TILES = POOL_GROUP // LANE


def _pool_sample_kernel(hist_ref, *rest, n_seq, n_new):
    u_refs = rest[:LANE_TILES]
    pg_ref, pw_ref, ps_ref = rest[LANE_TILES:LANE_TILES + 3]
    o_ref, nh_ref, pooled_scr = rest[-3 - LANE_TILES:-LANE_TILES]
    mix_scrs = rest[-LANE_TILES:]
    g = pl.program_id(0)

    def token_rows(t):
        return pl.ds(t, n_seq, stride=n_new)

    u_t = [jnp.concatenate([r[token_rows(t), :] for r in u_refs], axis=1) for t in range(n_new)]
    ext = [hist_ref[j] for j in range(POOL_HIST)] + u_t
    for j in range(POOL_HIST):
        nh_ref[j] = ext[n_new + j]
    for k, w in enumerate(POOL_WINDOWS):
        @pl.when(g == k)
        def _(w=w):
            s = ext
            size = 1
            while size < w:
                s = [s[i + size] + s[i] for i in range(len(s) - size)]
                size *= 2
            for t in range(n_new):
                pooled_scr[t * n_seq:(t + 1) * n_seq] = s[POOL_HIST + 1 + t - w] * (1.0 / w) - u_t[t]
    mixed = _dot(pooled_scr[...].astype(BF16), pw_ref[...])
    for t in range(n_new):
        for c, scr in enumerate(mix_scrs):
            scr[token_rows(t), :] = mixed[t * n_seq:(t + 1) * n_seq, c * LANE:(c + 1) * LANE]
    mixed = jnp.concatenate([scr[...] for scr in mix_scrs], axis=1)
    o_ref[...] = (mixed * ps_ref[...] * _silu(pg_ref[...])).astype(BF16)


def _pool_sample(zs, state_pool_t, a_buf, new_hist, pool_w, pool_scale, layer, n_seq, n_new, row0):
    rows = n_seq * n_new
    assert row0 % rows == 0
    hist_blk = (None, POOL_HIST, n_seq, POOL_GROUP)
    hist_map = lambda g: (layer, 0, 0, g)
    per_seg = D_MODEL // POOL_GROUP

    def u_tile(c):
        return pl.BlockSpec((rows, LANE), lambda g: (0, (SEG_U * per_seg + g) * LANE_TILES + c))

    in_specs = [pl.BlockSpec(hist_blk, hist_map)] + [u_tile(c) for c in range(LANE_TILES)]
    in_specs += [pl.BlockSpec((rows, POOL_GROUP), lambda g: (0, SEG_PG * per_seg + g)),
                 pl.BlockSpec((None, None, POOL_GROUP, POOL_GROUP), lambda g: (layer, g, 0, 0)),
                 pl.BlockSpec((1, POOL_GROUP), lambda g: (0, g)),
                 pl.BlockSpec(memory_space=pl.ANY)]
    args = [state_pool_t] + [zs] * LANE_TILES + [zs, pool_w, pool_scale, a_buf]
    aliases = {len(args) - 1: 0}
    if new_hist is not None:
        aliases[len(args)] = 1
        in_specs.append(pl.BlockSpec(memory_space=pl.ANY))
        args.append(new_hist)
    return pl.pallas_call(
        functools.partial(_pool_sample_kernel, n_seq=n_seq, n_new=n_new),
        out_shape=(jax.ShapeDtypeStruct(a_buf.shape, BF16),
                   jax.ShapeDtypeStruct(state_pool_t.shape, F32)),
        grid=(N_GROUPS,),
        in_specs=in_specs,
        out_specs=(pl.BlockSpec((rows, POOL_GROUP), lambda g: (row0 // rows, g)),
                   pl.BlockSpec(hist_blk, hist_map)),
        scratch_shapes=[pltpu.VMEM((rows, POOL_GROUP), F32)] + [pltpu.VMEM((rows, LANE), F32)] * LANE_TILES,
        input_output_aliases=aliases,
        compiler_params=_cparams("arbitrary"),
        name="pool_sample",
    )(*args)


def _chunk_body(dec_ref, decay_ref, q_ref, k_ref, v_ref, rg_ref, gn_ref, s_scr, o_ref, n_valid,
                rows=slice(None)):
    ridx = lax.broadcasted_iota(jnp.int32, (CHUNK, 1), 0).astype(F32)
    for h in range(HEADS):
        hs = slice(h * HEAD_DIM, (h + 1) * HEAD_DIM)
        lg = dec_ref[0, h]
        q = q_ref[rows, hs].astype(BF16)
        k = k_ref[rows, hs]
        v = v_ref[rows, hs].astype(BF16)
        s_old = s_scr[h]
        scores = _dot_nt(q, k.astype(BF16)) * decay_ref[h]
        intra = _dot(scores.astype(BF16), v)
        inter = _dot(q, s_old.astype(BF16)) * jnp.exp((ridx + 1.0) * lg)
        k_dec = (k * jnp.exp((n_valid - 1.0 - ridx) * lg)).astype(BF16)
        s_scr[h] = s_old * dec_ref[1, h] + _dot_tn(k_dec, v)
        o_ref[rows, hs] = _head_norm_gate(intra + inter, gn_ref[:, hs], rg_ref[rows, hs])


def _ret_kernel(dec_ref, decay_ref, q_ref, k_ref, v_ref, rg_ref, gn_ref, s0_ref, *rest, n_valid):
    o_ref, sfin_ref, s_scr = rest[-3:]
    c = pl.program_id(1)

    @pl.when(c == 0)
    def _():
        s_scr[...] = s0_ref[...]

    _chunk_body(dec_ref, decay_ref, q_ref, k_ref, v_ref, rg_ref, gn_ref, s_scr, o_ref, n_valid)

    @pl.when(c == pl.num_programs(1) - 1)
    def _():
        sfin_ref[...] = s_scr[...]


def _retention(srcs, lg, gn, decay, s0, b_buf, fin, fin_shape, fin_index, n_batch, n_chunks, n_valid):
    dec = jnp.stack([lg, jnp.exp(n_valid * lg)])

    def zmap(col):
        return lambda b, c: (b * n_chunks + c, col)

    in_specs = [pl.BlockSpec(memory_space=pltpu.SMEM),
                pl.BlockSpec((HEADS, CHUNK, CHUNK), lambda b, c: (0, 0, 0))]
    in_specs += [pl.BlockSpec((CHUNK, D_MODEL), zmap(col)) for _, col in srcs]
    in_specs += [pl.BlockSpec((1, D_MODEL), lambda b, c: (0, 0)),
                 pl.BlockSpec(_STATE, lambda b, c: (0, 0, 0))]
    args = [dec, decay] + [a for a, _ in srcs] + [gn, s0]
    aliases = {}
    for out_idx, buf in enumerate((b_buf, fin)):
        if buf is not None:
            aliases[len(args)] = out_idx
            in_specs.append(pl.BlockSpec(memory_space=pl.ANY))
            args.append(buf)
    fin_blk = (None,) * (len(fin_shape) - 3) + _STATE
    b_rows = n_batch * n_chunks * CHUNK if b_buf is None else b_buf.shape[0]
    return pl.pallas_call(
        functools.partial(_ret_kernel, n_valid=n_valid),
        out_shape=(jax.ShapeDtypeStruct((b_rows, D_MODEL), BF16),
                   jax.ShapeDtypeStruct(fin_shape, F32)),
        grid=(n_batch, n_chunks),
        in_specs=in_specs,
        out_specs=(pl.BlockSpec((CHUNK, D_MODEL), zmap(0)),
                   pl.BlockSpec(fin_blk, lambda b, c: fin_index(b) + (0, 0, 0))),
        scratch_shapes=[pltpu.VMEM(_STATE, F32)],
        input_output_aliases=aliases,
        compiler_params=_cparams("arbitrary", "arbitrary"),
        name="retention",
    )(*args)


TN_MERGE = 1024


def _merge_kernel(a_ref, b_ref, gpp_ref, grp_ref, gps_ref, grs_ref,
                  am_ref, bm_ref, gpm_ref, grm_ref, wp_ref, wr_ref, o_ref, om_ref, *, n_p, n_s):
    i = pl.program_id(1)

    def tile(a, b, gp, gr, out):
        pool_branch = _dot(a[...], wp_ref[...])
        ret_branch = _dot(b[...], wr_ref[...])
        out[...] = (jax.nn.sigmoid(gp[...]) * pool_branch + jax.nn.sigmoid(gr[...]) * ret_branch).astype(BF16)

    @pl.when(i < n_p)
    def _():
        tile(a_ref, b_ref, gpp_ref, grp_ref, o_ref)

    @pl.when((i >= n_p) & (i < n_p + n_s))
    def _():
        tile(a_ref, b_ref, gps_ref, grs_ref, o_ref)

    @pl.when(i == n_p + n_s)
    def _():
        tile(am_ref, bm_ref, gpm_ref, grm_ref, om_ref)


def _merge(a_in, b_in, z, zs, a_meta, b_meta, zm, proj_pool, proj_ret):
    n_p = z.shape[0] // TM
    n_s = zs.shape[0] // TM
    n_main = n_p + n_s
    assert a_in.shape[0] == n_main * TM and b_in.shape[0] == n_main * TM
    nt = D_MODEL // TN_MERGE

    def p_map(col):
        return lambda j, i: (jnp.minimum(i, n_p - 1), col(j))

    def s_map(col):
        return lambda j, i: (jnp.clip(i - n_p, 0, n_s - 1), col(j))

    row = pl.BlockSpec((TM, D_MODEL), lambda j, i: (jnp.minimum(i, n_main - 1), 0))
    gate = lambda ref_map, seg: pl.BlockSpec((TM, TN_MERGE), ref_map(lambda j: seg * nt + j))
    const = lambda j, i: (0, 0)
    return pl.pallas_call(
        functools.partial(_merge_kernel, n_p=n_p, n_s=n_s),
        out_shape=(jax.ShapeDtypeStruct((n_main * TM, D_MODEL), BF16),
                   jax.ShapeDtypeStruct((CHUNK, D_MODEL), BF16)),
        grid=(nt, n_main + 1),
        in_specs=[row, row,
                  gate(p_map, COL_GP), gate(p_map, COL_GR), gate(s_map, SEG_GP), gate(s_map, SEG_GR),
                  pl.BlockSpec((CHUNK, D_MODEL), const),
                  pl.BlockSpec((CHUNK, D_MODEL), const),
                  pl.BlockSpec((CHUNK, TN_MERGE), lambda j, i: (0, SEG_GP * nt + j)),
                  pl.BlockSpec((CHUNK, TN_MERGE), lambda j, i: (0, SEG_GR * nt + j)),
                  pl.BlockSpec((D_MODEL, TN_MERGE), lambda j, i: (0, j)),
                  pl.BlockSpec((D_MODEL, TN_MERGE), lambda j, i: (0, j))],
        out_specs=(pl.BlockSpec((TM, TN_MERGE), lambda j, i: (jnp.minimum(i, n_main - 1), j)),
                   pl.BlockSpec((CHUNK, TN_MERGE), lambda j, i: (0, j))),
        compiler_params=_cparams("arbitrary", "arbitrary"),
        name="merge_branches",
    )(a_in, b_in, z, z, zs, zs, a_meta, b_meta, zm, zm, proj_pool, proj_ret)


def _out_kernel(m_ref, mm_ref, w_ref, g_ref, *rest, first, last, n_p, n_s):
    rest = list(rest)
    x_refs = [rest.pop(0) for _ in range(3 if first else 2)]
    i = pl.program_id(0)

    def emit(x, merged, outs):
        xn = x + _dot(merged[...], w_ref[...])
        y = _rms(xn, g_ref[...])
        if last:
            outs[0][...] = y
        else:
            outs[0][...] = xn
            outs[1][...] = y.astype(BF16)

    main_outs = [rest[0:1], rest[1:2]] if last else [rest[0:2], rest[0:2]]

    @pl.when(i < n_p)
    def _():
        emit(x_refs[0][...], m_ref, main_outs[0])

    @pl.when((i >= n_p) & (i < n_p + n_s))
    def _():
        emit(x_refs[1 if first else 0][...], m_ref, main_outs[1])

    if not last:
        @pl.when(i == n_p + n_s)
        def _():
            emit(_meta_rows(x_refs[2]) if first else x_refs[1][...], mm_ref, rest[2:4])


def _out_projection(merged, merged_meta, w_out, g_next, x_parts, first, last, n_p, n_s):
    n = n_p + n_s
    main_spec = pl.BlockSpec((TM, D_MODEL), lambda i: (jnp.minimum(i, n - 1), 0))
    meta_spec = pl.BlockSpec((CHUNK, D_MODEL), lambda i: (0, 0))
    in_specs = [main_spec, meta_spec,
                pl.BlockSpec((D_MODEL, D_MODEL), lambda i: (0, 0)),
                pl.BlockSpec((1, D_MODEL), lambda i: (0, 0))]
    if first:
        in_specs += _x_specs(n_p, n_s) + [pl.BlockSpec((N_META, D_MODEL), lambda i: (0, 0))]
    else:
        in_specs += [main_spec, meta_spec]
    if last:
        out_shape = (jax.ShapeDtypeStruct((n_p * TM, D_MODEL), F32),
                     jax.ShapeDtypeStruct((n_s * TM, D_MODEL), F32))
        out_specs = tuple(_x_specs(n_p, n_s))
    else:
        out_shape = (jax.ShapeDtypeStruct((n * TM, D_MODEL), F32), jax.ShapeDtypeStruct((n * TM, D_MODEL), BF16),
                     jax.ShapeDtypeStruct((CHUNK, D_MODEL), F32), jax.ShapeDtypeStruct((CHUNK, D_MODEL), BF16))
        out_specs = (main_spec, main_spec, meta_spec, meta_spec)
    return pl.pallas_call(
        functools.partial(_out_kernel, first=first, last=last, n_p=n_p, n_s=n_s),
        out_shape=out_shape,
        grid=(n if last else n + 1,),
        in_specs=in_specs,
        out_specs=out_specs,
        compiler_params=_cparams("arbitrary"),
        name="out_projection",
    )(merged, merged_meta, w_out, g_next, *x_parts)


def _log_decay():
    return jnp.log1p(-jnp.exp2(-5.0 - jnp.arange(HEADS, dtype=F32)))


def _rope_tables(pos):
    inv_freq = ROPE_BASE ** (-jnp.arange(HALF, dtype=F32) / HALF)
    ang = pos[:, None] * inv_freq[None, :]
    return jnp.cos(ang), jnp.sin(ang)


def _chunk_decay(lg):
    idx = jnp.arange(CHUNK, dtype=F32)
    diff = idx[:, None] - idx[None, :]
    causal = diff >= 0
    return jnp.where(causal[None], jnp.exp(jnp.where(causal, diff, 0.0)[None] * lg[:, None, None]), 0.0)


def _sample_decay_tables(lg, n_new):
    r = jnp.arange(RB * HEADS * n_new)
    head = (r // n_new) % HEADS
    tok = (r % n_new).astype(F32)
    lg_r = lg[head]
    same = (r[:, None] // n_new) == (r[None, :] // n_new)
    diff = tok[:, None] - tok[None, :]
    keep = same & (diff >= 0)
    dmask = jnp.where(keep, jnp.exp(jnp.where(keep, diff, 0.0) * lg_r[:, None]), 0.0)
    qdec = jnp.broadcast_to(jnp.exp((tok + 1.0) * lg_r)[:, None], (r.shape[0], HEAD_DIM))
    kdec = jnp.broadcast_to(jnp.exp((n_new - 1.0 - tok) * lg_r)[:, None], (r.shape[0], HEAD_DIM))
    return dmask, qdec, kdec


def kernel(x_prompt, x_sample, state_pool, state_ret, meta_tokens, norm_gain, w_in, pool_w, pool_scale,
           ret_gn_gain, proj_pool, proj_ret, w_out, final_norm):
    n_batch, seq = x_prompt.shape[:2]
    n_seq, n_new = x_sample.shape[:2]
    depth = norm_gain.shape[0]
    rows_prompt = n_batch * seq
    rows_sample = n_seq * n_new
    assert seq % CHUNK == 0 and n_seq % RB == 0
    assert n_new == 8, "sample tokens of one sequence must fill one f32 sublane tile"
    assert rows_prompt % TM == 0 and rows_sample % TM == 0
    n_p = rows_prompt // TM
    n_s = rows_sample // TM

    xp = x_prompt.reshape(rows_prompt, D_MODEL).astype(F32)
    xs = x_sample.reshape(rows_sample, D_MODEL).astype(F32)
    meta = meta_tokens.astype(F32)
    state_pool = state_pool.astype(F32)
    state_ret = state_ret.astype(F32)
    w_in = w_in.astype(F32)

    trig_p = _rope_tables(N_META + jnp.arange(seq, dtype=F32))
    trig_s = tuple(jnp.tile(t, (n_seq, 1)) for t in _rope_tables(PAST_LEN + jnp.arange(n_new, dtype=F32)))
    trig_m = _rope_tables(jnp.where(jnp.arange(CHUNK) < N_META, jnp.arange(CHUNK, dtype=F32), 0.0))
    lg = _log_decay()
    decay = _chunk_decay(lg)
    gam_new = jnp.exp(n_new * lg)
    tables = _sample_decay_tables(lg, n_new)
    zero_state = jnp.zeros(_STATE, F32)
    row = lambda a: a.reshape(1, D_MODEL).astype(F32)

    h, h_meta = _first_norm(xp, xs, meta, row(norm_gain[0]), n_p, n_s)
    x_parts = (xp, xs, meta)
    pool_p = []
    ret_p = pool_s = ret_s = None
    pw = pool_w.astype(BF16)
    out_weights = (proj_pool.astype(F32), proj_ret.astype(F32), w_out.astype(F32))
    state_pool_t = jnp.transpose(state_pool, (0, 2, 1, 3))
    for l in range(depth):
        ps = row(pool_scale[l])
        gn = row(ret_gn_gain[l])

        zs, zm, wb = _in_projection_small(h, h_meta, w_in, l, trig_s, trig_m, rows_prompt, rows_sample)
        b_meta, s_meta = _retention(((zm, SEG_Q), (zm, SEG_K), (zm, SEG_V), (zm, SEG_RG)), lg, gn, decay,
                                    zero_state, None, None, _STATE, lambda b: (), 1, 1, N_META)
        sample = (gam_new, gn, tables, state_ret, l, n_new, zs, rows_prompt)
        n_groups = n_seq // RB
        a_in, hist_p, b_in, ret_s = _proj_pool(h, wb, zm, pw, ps, l, n_batch, seq, sample, ret_s)
        done = (D_MODEL // TN_IN) * (rows_prompt // TM_IN)
        value_groups = min(n_groups - done, len(VALUE_SEGS) * PER_SEG * (rows_prompt // TM_IN))
        zv, b_in, ret_s = _in_projection(h, wb, rows_prompt, VALUE_SEGS,
                                         sample=(sample, done, value_groups, b_in, ret_s))
        done += value_groups
        zr, wp_l, wr_l, wo_l, b_in, ret_s = _in_projection(
            h, wb, rows_prompt, ROPE_SEGS, trig=trig_p, seq=seq, convert=(out_weights, l),
            sample=(sample, done, n_groups - done, b_in, ret_s))
        zg, b_in, ret_p = _in_projection(
            h, wb, rows_prompt, GATE_SEGS,
            chunks=(zr, zv, lg, gn, decay, s_meta, b_in, ret_p, (depth, n_batch) + _STATE, l, n_batch,
                    seq // CHUNK))

        a_meta = _pool_meta(zm, pw, ps, l)
        a_in, pool_s = _pool_sample(zs, state_pool_t, a_in, pool_s, pw, ps, l, n_seq, n_new, rows_prompt)
        pool_p.append(hist_p)

        merged, merged_meta = _merge(a_in, b_in, zg, zs, a_meta, b_meta, zm, wp_l, wr_l)
        last = l == depth - 1
        g_next = row(final_norm if last else norm_gain[l + 1])
        outs = _out_projection(merged, merged_meta, wo_l, g_next, x_parts, l == 0, last, n_p, n_s)
        if last:
            y_prompt, y_sample = outs
        else:
            x_new, h, x_meta, h_meta = outs
            x_parts = (x_new, x_meta)

    return (y_prompt.reshape(n_batch, seq, D_MODEL), y_sample.reshape(n_seq, n_new, D_MODEL),
            jnp.stack(pool_p), ret_p, jnp.transpose(pool_s, (0, 2, 1, 3)), ret_s)
```

```python
import functools

import jax
import jax.numpy as jnp
from jax import lax
from jax.experimental import pallas as pl
from jax.experimental.pallas import tpu as pltpu

F32 = jnp.float32
BF16 = jnp.bfloat16

D_MODEL = 2048
N_META = 16
POOL_WINDOWS = (2, 4, 8, 16)
N_GROUPS = len(POOL_WINDOWS)
POOL_GROUP = D_MODEL // N_GROUPS
POOL_HIST = max(POOL_WINDOWS) - 1
HEADS = 8
HEAD_DIM = D_MODEL // HEADS
HALF = HEAD_DIM // 2
CHUNK = 128
ROPE_BASE = 10000.0
EPS = 1e-6
PAST_LEN = 16384
SEG_U, SEG_PG, SEG_Q, SEG_K, SEG_V, SEG_RG, SEG_GP, SEG_GR = range(8)
N_SEG = 8
ROPE_SEGS = (SEG_Q, SEG_K)
COL_Q, COL_K = range(len(ROPE_SEGS))
VALUE_SEGS = (SEG_V, SEG_RG)
COL_V, COL_RG = range(len(VALUE_SEGS))
GATE_SEGS = (SEG_GP, SEG_GR)
COL_GP, COL_GR = range(len(GATE_SEGS))

TM = 512
TM_IN = 1024
TN_IN = 1024
PER_SEG = D_MODEL // TN_IN
VMEM_LIMIT = 56 * 1024 * 1024


VMEM_LIMIT_MAX = 60 * 1024 * 1024


def _cparams(*sem, vmem=VMEM_LIMIT):
    return pltpu.CompilerParams(dimension_semantics=sem, vmem_limit_bytes=vmem)


def _dot(a, b):
    return jnp.dot(a, b, preferred_element_type=F32)


def _dot_nt(a, b):
    return lax.dot_general(a, b, (((1,), (1,)), ((), ())), preferred_element_type=F32)


def _dot_tn(a, b):
    return lax.dot_general(a, b, (((0,), (0,)), ((), ())), preferred_element_type=F32)


def _rms(x, g):
    return x * lax.rsqrt(jnp.mean(x * x, axis=-1, keepdims=True) + EPS) * g


def _silu(x):
    return x * jax.nn.sigmoid(x)


def _x_specs(n_p, n_s):
    return [pl.BlockSpec((TM, D_MODEL), lambda i: (jnp.minimum(i, n_p - 1), 0)),
            pl.BlockSpec((TM, D_MODEL), lambda i: (jnp.clip(i - n_p, 0, n_s - 1), 0))]


def _meta_rows(meta_ref):
    return jnp.concatenate([meta_ref[...], jnp.zeros((CHUNK - N_META, D_MODEL), F32)], axis=0)


def _norm_kernel(xp_ref, xs_ref, meta_ref, g_ref, h_ref, hm_ref, *, n_p, n_s):
    i = pl.program_id(0)

    @pl.when(i < n_p)
    def _():
        h_ref[...] = _rms(xp_ref[...], g_ref[...]).astype(BF16)

    @pl.when((i >= n_p) & (i < n_p + n_s))
    def _():
        h_ref[...] = _rms(xs_ref[...], g_ref[...]).astype(BF16)

    @pl.when(i == n_p + n_s)
    def _():
        hm_ref[...] = _rms(_meta_rows(meta_ref), g_ref[...]).astype(BF16)


def _first_norm(xp, xs, meta, g, n_p, n_s):
    n = n_p + n_s
    return pl.pallas_call(
        functools.partial(_norm_kernel, n_p=n_p, n_s=n_s),
        out_shape=(jax.ShapeDtypeStruct((n * TM, D_MODEL), BF16),
                   jax.ShapeDtypeStruct((CHUNK, D_MODEL), BF16)),
        grid=(n + 1,),
        in_specs=_x_specs(n_p, n_s) + [pl.BlockSpec((N_META, D_MODEL), lambda i: (0, 0)),
                                       pl.BlockSpec((1, D_MODEL), lambda i: (0, 0))],
        out_specs=(pl.BlockSpec((TM, D_MODEL), lambda i: (jnp.minimum(i, n - 1), 0)),
                   pl.BlockSpec((CHUNK, D_MODEL), lambda i: (0, 0))),
        compiler_params=_cparams("arbitrary"),
        name="first_norm",
    )(xp, xs, meta, g)


RB = 2
_STATE = (HEADS, HEAD_DIM, HEAD_DIM)


def _head_norm_gate(o, gain, gate):
    mu = jnp.mean(o, axis=-1, keepdims=True)
    oc = o - mu
    on = oc * lax.rsqrt(jnp.mean(oc * oc, axis=-1, keepdims=True) + EPS)
    return ((on * gain) * _silu(gate)).astype(BF16)


def _sample_body(gam_ref, q_ref, k_ref, v_ref, rg_ref, gn_ref, dmask_ref, qdec_ref, kdec_ref, s_ref,
                 o_ref, sout_ref, n_new):
    pairs = [(b, h) for b in range(RB) for h in range(HEADS)]

    def stack(ref):
        return jnp.concatenate(
            [ref[b * n_new:(b + 1) * n_new, h * HEAD_DIM:(h + 1) * HEAD_DIM] for b, h in pairs], axis=0)

    k = stack(k_ref)
    qb = stack(q_ref).astype(BF16)
    v = stack(v_ref)
    vb = v.astype(BF16)
    scores = _dot_nt(qb, k.astype(BF16)) * dmask_ref[...]
    intra = _dot(scores.astype(BF16), vb)
    k_dec = (k * kdec_ref[...]).astype(BF16)
    rows = lax.broadcasted_iota(jnp.int32, v.shape, 0)
    is_even = (rows & n_new) == 0
    v_even = jnp.where(is_even, v, 0.0).astype(BF16)
    v_odd = jnp.where(is_even, 0.0, v).astype(BF16)
    grp = 2 * n_new
    inter_parts = []
    for idx, (b, h) in enumerate(pairs):
        g0 = (idx // 2) * grp
        off = (idx % 2) * n_new
        s_old = s_ref[b, h]
        inter = _dot(qb[g0:g0 + grp], s_old.astype(BF16))
        inter_parts.append(inter[off:off + n_new])
        v_sel = v_even if idx % 2 == 0 else v_odd
        sout_ref[b, h] = s_old * gam_ref[h] + _dot_tn(k_dec[g0:g0 + grp], v_sel[g0:g0 + grp])
    o = intra + jnp.concatenate(inter_parts, axis=0) * qdec_ref[...]
    gain = jnp.concatenate(
        [jnp.broadcast_to(gn_ref[:, h * HEAD_DIM:(h + 1) * HEAD_DIM], (n_new, HEAD_DIM)) for _, h in pairs],
        axis=0)
    out = _head_norm_gate(o, gain, stack(rg_ref)).astype(F32)
    seqs = []
    for b in range(RB):
        seqs.append(jnp.concatenate(
            [out[(b * HEADS + h) * n_new:(b * HEADS + h + 1) * n_new] for h in range(HEADS)], axis=1))
    o_ref[...] = jnp.concatenate(seqs, axis=0).astype(BF16)


def _rope_store(z, o_ref, cos_ref, sin_ref, scale):
    cos = cos_ref[...]
    sin = sin_ref[...]
    for hh in range(TN_IN // HEAD_DIM):
        lo = slice(hh * HEAD_DIM, hh * HEAD_DIM + HALF)
        hi = slice(hh * HEAD_DIM + HALF, (hh + 1) * HEAD_DIM)
        t1 = z[:, lo]
        t2 = z[:, hi]
        o_ref[:, lo] = (t1 * cos - t2 * sin) * scale
        o_ref[:, hi] = (t2 * cos + t1 * sin) * scale


def _k_scale(is_k):
    return jnp.where(is_k, HEAD_DIM ** -0.5, 1.0).astype(F32)


def _inproj_small_kernel(hs_ref, hm_ref, w_ref, cos_ref, sin_ref, cosm_ref, sinm_ref, zs_ref, zm_ref, wb_ref):
    s = pl.program_id(0)
    wb_ref[...] = w_ref[...].astype(BF16)
    zs_ref[...] = _dot(hs_ref[...], wb_ref[...])
    zm_ref[...] = _dot(hm_ref[...], wb_ref[...])

    @pl.when((s >= SEG_Q * PER_SEG) & (s < (SEG_K + 1) * PER_SEG))
    def _():
        scale = _k_scale(s >= SEG_K * PER_SEG)
        _rope_store(zs_ref[...], zs_ref, cos_ref, sin_ref, scale)
        _rope_store(zm_ref[...], zm_ref, cosm_ref, sinm_ref, scale)


def _in_projection_small(h, h_meta, w_in, layer, trig_s, trig_m, row0, rows):
    assert row0 % rows == 0
    const = lambda s: (0, 0)
    col = lambda s: (0, s)
    return pl.pallas_call(
        _inproj_small_kernel,
        out_shape=(jax.ShapeDtypeStruct((rows, N_SEG * D_MODEL), F32),
                   jax.ShapeDtypeStruct((CHUNK, N_SEG * D_MODEL), F32),
                   jax.ShapeDtypeStruct((D_MODEL, N_SEG * D_MODEL), BF16)),
        grid=(N_SEG * PER_SEG,),
        in_specs=[pl.BlockSpec((rows, D_MODEL), lambda s: (row0 // rows, 0)),
                  pl.BlockSpec((CHUNK, D_MODEL), const),
                  pl.BlockSpec((None, D_MODEL, TN_IN), lambda s: (layer, 0, s)),
                  pl.BlockSpec((rows, HALF), const),
                  pl.BlockSpec((rows, HALF), const),
                  pl.BlockSpec((CHUNK, HALF), const),
                  pl.BlockSpec((CHUNK, HALF), const)],
        out_specs=(pl.BlockSpec((rows, TN_IN), col),
                   pl.BlockSpec((CHUNK, TN_IN), col),
                   pl.BlockSpec((D_MODEL, TN_IN), col)),
        compiler_params=_cparams("arbitrary"),
        name="in_projection_small",
    )(h, h_meta, w_in, *trig_s, *trig_m)


N_SAMPLE_IN = 10


def _sample_operands(sample, group_of):
    gam, gn, (dmask, qdec, kdec), state_ret, layer, n_new, zs, prompt_rows = sample
    rows_g = RB * n_new
    stack_rows = RB * HEADS * n_new
    assert prompt_rows % rows_g == 0

    def zmap(seg):
        return lambda *idx: (group_of(*idx), seg)

    const = lambda *idx: (0, 0)
    sblk = (None, RB) + _STATE
    state_map = lambda *idx: (layer, group_of(*idx), 0, 0, 0)
    in_specs = [pl.BlockSpec(memory_space=pltpu.SMEM),
                pl.BlockSpec((rows_g, D_MODEL), zmap(SEG_Q)),
                pl.BlockSpec((rows_g, D_MODEL), zmap(SEG_K)),
                pl.BlockSpec((rows_g, D_MODEL), zmap(SEG_V)),
                pl.BlockSpec((rows_g, D_MODEL), zmap(SEG_RG)),
                pl.BlockSpec((1, D_MODEL), const),
                pl.BlockSpec((stack_rows, stack_rows), const),
                pl.BlockSpec((stack_rows, HEAD_DIM), const),
                pl.BlockSpec((stack_rows, HEAD_DIM), const),
                pl.BlockSpec(sblk, state_map)]
    args = [gam, zs, zs, zs, zs, gn, dmask, qdec, kdec, state_ret]
    out_shape = [jax.ShapeDtypeStruct((prompt_rows + zs.shape[0], D_MODEL), BF16),
                 jax.ShapeDtypeStruct(state_ret.shape, F32)]
    out_specs = [pl.BlockSpec((rows_g, D_MODEL), lambda *idx: (prompt_rows // rows_g + group_of(*idx), 0)),
                 pl.BlockSpec(sblk, state_map)]
    return in_specs, args, out_shape, out_specs


N_CHUNK_IN = 8
CHUNKS_PER_STEP = 2


def _inproj_kernel(h_ref, w_ref, *rest, rope, n_convert, n_new, n_groups, chunk_steps):
    pos = 0
    if rope:
        cos_ref, sin_ref = rest[:2]
        pos = 2
    conv_in = rest[pos:pos + n_convert]
    pos += n_convert
    sample_refs = rest[pos:pos + N_SAMPLE_IN] if n_new is not None else None
    chunk_refs = rest[pos:pos + N_CHUNK_IN] if chunk_steps is not None else None
    n_scr = 0 if chunk_steps is None else 1
    n_out = 1 + n_convert + (0 if n_new is None else 2) + (0 if chunk_steps is None else 2)
    outs = rest[len(rest) - n_out - n_scr:len(rest) - n_scr]
    o_ref = outs[0]
    conv_out = outs[1:1 + n_convert]
    step = pl.program_id(0) * pl.num_programs(1) + pl.program_id(1)

    def tile():
        z = _dot(h_ref[...], w_ref[...])
        if rope:
            _rope_store(z, o_ref, cos_ref, sin_ref, _k_scale(pl.program_id(0) >= PER_SEG))
        else:
            o_ref[...] = z
        for src, dst in zip(conv_in, conv_out):
            dst[...] = src[...].astype(BF16)

    if chunk_steps is not None:
        dec_ref, decay_ref, q_ref, k_ref, v_ref, rg_ref, gn_ref, s0_ref = chunk_refs
        bo_ref, fin_ref = outs[-2:]
        s_scr = rest[-1]

        @pl.when(step % chunk_steps == 0)
        def _():
            s_scr[...] = s0_ref[...]

        _chunk_body(dec_ref, decay_ref, q_ref, k_ref, v_ref, rg_ref, gn_ref, s_scr, bo_ref, CHUNK,
                    chunks=[slice(j * CHUNK, (j + 1) * CHUNK) for j in range(CHUNKS_PER_STEP)])
        tile()

        @pl.when(step % chunk_steps == chunk_steps - 1)
        def _():
            fin_ref[...] = s_scr[...]
    elif n_new is not None:
        so_ref, sout_ref = outs[-2:]

        @pl.when(step < n_groups)
        def _():
            tile()
            _sample_body(*sample_refs, so_ref, sout_ref, n_new)

        @pl.when(step >= n_groups)
        def _():
            tile()
    else:
        tile()


def _in_projection(h, wb, rows, segs, trig=None, seq=None, convert=None, sample=None, chunks=None):
    assert rows % TM_IN == 0 and (sample is None or chunks is None)
    tiles = rows // TM_IN
    steps = len(segs) * PER_SEG * tiles
    step_of = lambda s, i: s * tiles + i
    rope = trig is not None
    gap_at = next((j for j in range(1, len(segs)) if segs[j] != segs[j - 1] + 1), len(segs))
    gap = segs[gap_at] - segs[gap_at - 1] - 1 if gap_at < len(segs) else 0
    assert all(segs[j] == segs[0] + j + (gap if j >= gap_at else 0) for j in range(len(segs)))

    def wmap(s, i):
        j = s // PER_SEG
        seg = segs[0] + j + jnp.where(j >= gap_at, gap, 0)
        return (0, seg * PER_SEG + s % PER_SEG)

    in_specs = [pl.BlockSpec((TM_IN, D_MODEL), lambda s, i: (i, 0)),
                pl.BlockSpec((D_MODEL, TN_IN), wmap)]
    args = [h, wb]
    out_shape = [jax.ShapeDtypeStruct((rows, len(segs) * D_MODEL), F32)]
    out_specs = [pl.BlockSpec((TM_IN, TN_IN), lambda s, i: (i, s))]
    alias_bufs = []
    scratch = []
    n_convert = 0
    n_new = n_groups = chunk_steps = None
    if rope:
        assert seq % TM_IN == 0
        in_specs += [pl.BlockSpec((TM_IN, HALF), lambda s, i: (i % (seq // TM_IN), 0)) for _ in range(2)]
        args += list(trig)
    if convert is not None:
        weights, w_layer = convert
        n_convert = len(weights)
        busy = 0 if sample is None or sample[2] == steps else sample[2]
        assert D_MODEL % (steps - busy) == 0
        w_rows = D_MODEL // (steps - busy)
        w_block = lambda s, i: jnp.maximum(step_of(s, i) - busy, 0)
        in_specs += [pl.BlockSpec((None, w_rows, D_MODEL), lambda s, i: (w_layer, w_block(s, i), 0))
                     for _ in weights]
        args += list(weights)
        out_shape += [jax.ShapeDtypeStruct((D_MODEL, D_MODEL), BF16) for _ in weights]
        out_specs += [pl.BlockSpec((w_rows, D_MODEL), lambda s, i: (w_block(s, i), 0)) for _ in weights]
    if sample is not None:
        sample, first_group, n_groups, b_buf, new_state = sample
        n_new = sample[5]
        assert 0 < n_groups <= steps, "one group of sample sequences per grid step"

        def group(s, i):
            return first_group + jnp.minimum(step_of(s, i), n_groups - 1)

        s_in, s_args, s_shape, s_specs = _sample_operands(sample, group)
        in_specs += s_in
        args += s_args
        alias_bufs += [(len(out_shape), b_buf), (len(out_shape) + 1, new_state)]
        out_shape += s_shape
        out_specs += s_specs
    if chunks is not None:
        qk, vg, lg, gn, decay, s0, b_buf, fin, fin_shape, layer, n_batch, n_chunks = chunks
        assert steps * CHUNKS_PER_STEP == n_batch * n_chunks and n_chunks % CHUNKS_PER_STEP == 0
        chunk_steps = n_chunks // CHUNKS_PER_STEP
        crows = CHUNKS_PER_STEP * CHUNK

        def cmap(col):
            return lambda s, i: (step_of(s, i), col)

        in_specs += [pl.BlockSpec(memory_space=pltpu.SMEM),
                     pl.BlockSpec((HEADS, CHUNK, CHUNK), lambda s, i: (0, 0, 0)),
                     pl.BlockSpec((crows, D_MODEL), cmap(COL_Q)),
                     pl.BlockSpec((crows, D_MODEL), cmap(COL_K)),
                     pl.BlockSpec((crows, D_MODEL), cmap(COL_V)),
                     pl.BlockSpec((crows, D_MODEL), cmap(COL_RG)),
                     pl.BlockSpec((1, D_MODEL), lambda s, i: (0, 0)),
                     pl.BlockSpec(_STATE, lambda s, i: (0, 0, 0))]
        args += [jnp.stack([lg, jnp.exp(CHUNK * lg)]), decay, qk, qk, vg, vg, gn, s0]
        alias_bufs += [(len(out_shape), b_buf)]
        if fin is not None:
            alias_bufs += [(len(out_shape) + 1, fin)]
        out_shape += [jax.ShapeDtypeStruct(b_buf.shape, BF16), jax.ShapeDtypeStruct(fin_shape, F32)]
        out_specs += [pl.BlockSpec((crows, D_MODEL), cmap(0)),
                      pl.BlockSpec((None, None) + _STATE,
                                   lambda s, i: (layer, step_of(s, i) // chunk_steps, 0, 0, 0))]
        scratch = [pltpu.VMEM(_STATE, F32)]
    aliases = {}
    for out_idx, buf in alias_bufs:
        aliases[len(args)] = out_idx
        in_specs.append(pl.BlockSpec(memory_space=pl.ANY))
        args.append(buf)
    outs = pl.pallas_call(
        functools.partial(_inproj_kernel, rope=rope, n_convert=n_convert, n_new=n_new, n_groups=n_groups,
                          chunk_steps=chunk_steps),
        out_shape=tuple(out_shape),
        grid=(len(segs) * PER_SEG, tiles),
        in_specs=in_specs,
        out_specs=tuple(out_specs),
        scratch_shapes=scratch,
        input_output_aliases=aliases,
        compiler_params=_cparams("arbitrary", "arbitrary",
                                 vmem=VMEM_LIMIT_MAX if convert and sample else VMEM_LIMIT),
        name="in_projection_" + "_".join(str(g) for g in segs),
    )(*args)
    return outs if len(outs) > 1 else outs[0]


def _window_sum(ext, w, base, rows):
    s = ext
    size = 1
    while size < w:
        s = s[size:] + s[:-size]
        size *= 2
    start = base - (w - 1)
    return s[start:start + rows]


def _pool_group(g, wsum, u, inv_cnt, pw_ref, ps_ref, gate):
    sl = slice(g * POOL_GROUP, (g + 1) * POOL_GROUP)
    pooled = wsum * inv_cnt - u
    mixed = _dot(pooled.astype(BF16), pw_ref[g])
    return (mixed * ps_ref[:, sl] * _silu(gate)).astype(BF16)


def _pool_w_spec(layer):
    return pl.BlockSpec((None, N_GROUPS, POOL_GROUP, POOL_GROUP), lambda *_: (layer, 0, 0, 0))


def _pool_meta_kernel(u_ref, pg_ref, pw_ref, ps_ref, o_ref):
    avail = lax.broadcasted_iota(jnp.int32, (N_META, 1), 0).astype(F32) + 1.0
    o_ref[N_META:] = jnp.zeros((CHUNK - N_META, D_MODEL), BF16)
    for g, w in enumerate(POOL_WINDOWS):
        sl = slice(g * POOL_GROUP, (g + 1) * POOL_GROUP)
        u = u_ref[:, sl]
        wsum = _window_sum(jnp.concatenate([jnp.zeros_like(u), u], axis=0), w, N_META, N_META)
        inv_cnt = 1.0 / jnp.minimum(float(w), avail)
        o_ref[0:N_META, sl] = _pool_group(g, wsum, u, inv_cnt, pw_ref, ps_ref, pg_ref[:, sl])


def _pool_meta(zm, pool_w, pool_scale, layer):
    return pl.pallas_call(
        _pool_meta_kernel,
        out_shape=jax.ShapeDtypeStruct((CHUNK, D_MODEL), BF16),
        grid=(1,),
        in_specs=[pl.BlockSpec((N_META, D_MODEL), lambda i: (0, SEG_U)),
                  pl.BlockSpec((N_META, D_MODEL), lambda i: (0, SEG_PG)),
                  _pool_w_spec(layer),
                  pl.BlockSpec((1, D_MODEL), lambda i: (0, 0))],
        out_specs=pl.BlockSpec((CHUNK, D_MODEL), lambda i: (0, 0)),
        compiler_params=_cparams("arbitrary"),
        name="pool_meta",
    )(zm, zm, pool_w, pool_scale)


GROUPS_PER_BLOCK = TN_IN // POOL_GROUP


def _proj_pool_kernel(h_ref, wu_ref, wpg_ref, meta_ref, pw_ref, ps_ref, *rest, tiles_per_seq, n_new):
    sample_refs = rest[:N_SAMPLE_IN]
    o_ref, hist_ref, so_ref, sout_ref, carry = rest[-5:]
    c = pl.program_id(0)
    i = pl.program_id(1)
    first = (i % tiles_per_seq) == 0
    last = (i % tiles_per_seq) == tiles_per_seq - 1
    for cc in range(D_MODEL // TN_IN):
        @pl.when(c == cc)
        def _(cc=cc):
            u = _dot(h_ref[...], wu_ref[...])
            pg = _dot(h_ref[...], wpg_ref[...])
            prev = jnp.where(first, meta_ref[...], carry[...])
            for gg in range(GROUPS_PER_BLOCK):
                w = POOL_WINDOWS[cc * GROUPS_PER_BLOCK + gg]
                sl = slice(gg * POOL_GROUP, (gg + 1) * POOL_GROUP)
                wsum = _window_sum(jnp.concatenate([prev[:, sl], u[:, sl]], axis=0), w, N_META, TM_IN)
                pooled = wsum * (1.0 / w) - u[:, sl]
                mixed = _dot(pooled.astype(BF16), pw_ref[gg])
                o_ref[:, sl] = (mixed * ps_ref[:, sl] * _silu(pg[:, sl])).astype(BF16)
            carry[...] = u[TM_IN - N_META:]
            _sample_body(*sample_refs, so_ref, sout_ref, n_new)

            @pl.when(last)
            def _():
                hist_ref[0] = u[TM_IN - POOL_HIST:]


def _proj_pool(h, wb, zm, pool_w, pool_scale, layer, n_batch, seq, sample, new_state):
    assert seq % TM_IN == 0 and TN_IN % POOL_GROUP == 0
    tiles_per_seq = seq // TM_IN
    tiles = n_batch * tiles_per_seq
    step = lambda c, i: c * tiles + i
    in_specs = [pl.BlockSpec((TM_IN, D_MODEL), lambda c, i: (i, 0)),
                pl.BlockSpec((D_MODEL, TN_IN), lambda c, i: (0, SEG_U * PER_SEG + c)),
                pl.BlockSpec((D_MODEL, TN_IN), lambda c, i: (0, SEG_PG * PER_SEG + c)),
                pl.BlockSpec((N_META, TN_IN), lambda c, i: (0, SEG_U * PER_SEG + c)),
                pl.BlockSpec((None, GROUPS_PER_BLOCK, POOL_GROUP, POOL_GROUP), lambda c, i: (layer, c, 0, 0)),
                pl.BlockSpec((1, TN_IN), lambda c, i: (0, c))]
    args = [h, wb, wb, zm, pool_w, pool_scale]
    s_in, s_args, s_shape, s_specs = _sample_operands(sample, step)
    in_specs += s_in
    args += s_args
    total_rows = s_shape[0].shape[0]
    out_shape = [jax.ShapeDtypeStruct((total_rows, D_MODEL), BF16),
                 jax.ShapeDtypeStruct((n_batch, POOL_HIST, D_MODEL), F32)] + s_shape
    out_specs = [pl.BlockSpec((TM_IN, TN_IN), lambda c, i: (i, c)),
                 pl.BlockSpec((1, POOL_HIST, TN_IN), lambda c, i: (i // tiles_per_seq, 0, c))] + s_specs
    aliases = {}
    if new_state is not None:
        aliases[len(args)] = 3
        in_specs.append(pl.BlockSpec(memory_space=pl.ANY))
        args.append(new_state)
    return pl.pallas_call(
        functools.partial(_proj_pool_kernel, tiles_per_seq=tiles_per_seq, n_new=sample[5]),
        out_shape=tuple(out_shape),
        grid=(D_MODEL // TN_IN, tiles),
        in_specs=in_specs,
        out_specs=tuple(out_specs),
        scratch_shapes=[pltpu.VMEM((N_META, TN_IN), F32)],
        input_output_aliases=aliases,
        compiler_params=_cparams("arbitrary", "arbitrary"),
        name="projection_pool",
    )(*args)


LANE = 128
LANE_TILES = POOL_GROUP // LANE


def _pool_sample_kernel(hist_ref, *rest, n_seq, n_new):
    u_refs = rest[:LANE_TILES]
    pg_ref, pw_ref, ps_ref = rest[LANE_TILES:LANE_TILES + 3]
    o_ref, nh_ref, pooled_scr = rest[-3 - LANE_TILES:-LANE_TILES]
    mix_scrs = rest[-LANE_TILES:]
    g = pl.program_id(0)

    def token_rows(t):
        return pl.ds(t, n_seq, stride=n_new)

    u_t = [jnp.concatenate([r[token_rows(t), :] for r in u_refs], axis=1) for t in range(n_new)]
    ext = [hist_ref[j] for j in range(POOL_HIST)] + u_t
    for j in range(POOL_HIST):
        nh_ref[j] = ext[n_new + j]
    for k, w in enumerate(POOL_WINDOWS):
        @pl.when(g == k)
        def _(w=w):
            s = ext
            size = 1
            while size < w:
                s = [s[i + size] + s[i] for i in range(len(s) - size)]
                size *= 2
            for t in range(n_new):
                pooled_scr[t * n_seq:(t + 1) * n_seq] = s[POOL_HIST + 1 + t - w] * (1.0 / w) - u_t[t]
    mixed = _dot(pooled_scr[...].astype(BF16), pw_ref[...])
    for t in range(n_new):
        for c, scr in enumerate(mix_scrs):
            scr[token_rows(t), :] = mixed[t * n_seq:(t + 1) * n_seq, c * LANE:(c + 1) * LANE]
    mixed = jnp.concatenate([scr[...] for scr in mix_scrs], axis=1)
    o_ref[...] = (mixed * ps_ref[...] * _silu(pg_ref[...])).astype(BF16)


def _pool_sample(zs, state_pool_t, a_buf, new_hist, pool_w, pool_scale, layer, n_seq, n_new, row0):
    rows = n_seq * n_new
    assert row0 % rows == 0
    hist_blk = (None, POOL_HIST, n_seq, POOL_GROUP)
    hist_map = lambda g: (layer, 0, 0, g)
    per_seg = D_MODEL // POOL_GROUP

    def u_tile(c):
        return pl.BlockSpec((rows, LANE), lambda g: (0, (SEG_U * per_seg + g) * LANE_TILES + c))

    in_specs = [pl.BlockSpec(hist_blk, hist_map)] + [u_tile(c) for c in range(LANE_TILES)]
    in_specs += [pl.BlockSpec((rows, POOL_GROUP), lambda g: (0, SEG_PG * per_seg + g)),
                 pl.BlockSpec((None, None, POOL_GROUP, POOL_GROUP), lambda g: (layer, g, 0, 0)),
                 pl.BlockSpec((1, POOL_GROUP), lambda g: (0, g)),
                 pl.BlockSpec(memory_space=pl.ANY)]
    args = [state_pool_t] + [zs] * LANE_TILES + [zs, pool_w, pool_scale, a_buf]
    aliases = {len(args) - 1: 0}
    if new_hist is not None:
        aliases[len(args)] = 1
        in_specs.append(pl.BlockSpec(memory_space=pl.ANY))
        args.append(new_hist)
    return pl.pallas_call(
        functools.partial(_pool_sample_kernel, n_seq=n_seq, n_new=n_new),
        out_shape=(jax.ShapeDtypeStruct(a_buf.shape, BF16),
                   jax.ShapeDtypeStruct(state_pool_t.shape, F32)),
        grid=(N_GROUPS,),
        in_specs=in_specs,
        out_specs=(pl.BlockSpec((rows, POOL_GROUP), lambda g: (row0 // rows, g)),
                   pl.BlockSpec(hist_blk, hist_map)),
        scratch_shapes=[pltpu.VMEM((rows, POOL_GROUP), F32)] + [pltpu.VMEM((rows, LANE), F32)] * LANE_TILES,
        input_output_aliases=aliases,
        compiler_params=_cparams("arbitrary"),
        name="pool_sample",
    )(*args)


def _chunk_body(dec_ref, decay_ref, q_ref, k_ref, v_ref, rg_ref, gn_ref, s_scr, o_ref, n_valid,
                chunks=(slice(None),)):
    ridx = lax.broadcasted_iota(jnp.int32, (CHUNK, 1), 0).astype(F32)
    for h in range(HEADS):
        hs = slice(h * HEAD_DIM, (h + 1) * HEAD_DIM)
        lg = dec_ref[0, h]
        q_dec = jnp.exp((ridx + 1.0) * lg)
        k_dec = jnp.exp((n_valid - 1.0 - ridx) * lg)
        state = s_scr[h]
        for rows in chunks:
            q = q_ref[rows, hs].astype(BF16)
            k = k_ref[rows, hs]
            v = v_ref[rows, hs].astype(BF16)
            scores = _dot_nt(q, k.astype(BF16)) * decay_ref[h]
            intra = _dot(scores.astype(BF16), v)
            inter = _dot(q, state.astype(BF16)) * q_dec
            state = state * dec_ref[1, h] + _dot_tn((k * k_dec).astype(BF16), v)
            o_ref[rows, hs] = _head_norm_gate(intra + inter, gn_ref[:, hs], rg_ref[rows, hs])
        s_scr[h] = state


def _ret_kernel(dec_ref, decay_ref, q_ref, k_ref, v_ref, rg_ref, gn_ref, s0_ref, *rest, n_valid):
    o_ref, sfin_ref, s_scr = rest[-3:]
    c = pl.program_id(1)

    @pl.when(c == 0)
    def _():
        s_scr[...] = s0_ref[...]

    _chunk_body(dec_ref, decay_ref, q_ref, k_ref, v_ref, rg_ref, gn_ref, s_scr, o_ref, n_valid)

    @pl.when(c == pl.num_programs(1) - 1)
    def _():
        sfin_ref[...] = s_scr[...]


def _retention(srcs, lg, gn, decay, s0, b_buf, fin, fin_shape, fin_index, n_batch, n_chunks, n_valid):
    dec = jnp.stack([lg, jnp.exp(n_valid * lg)])

    def zmap(col):
        return lambda b, c: (b * n_chunks + c, col)

    in_specs = [pl.BlockSpec(memory_space=pltpu.SMEM),
                pl.BlockSpec((HEADS, CHUNK, CHUNK), lambda b, c: (0, 0, 0))]
    in_specs += [pl.BlockSpec((CHUNK, D_MODEL), zmap(col)) for _, col in srcs]
    in_specs += [pl.BlockSpec((1, D_MODEL), lambda b, c: (0, 0)),
                 pl.BlockSpec(_STATE, lambda b, c: (0, 0, 0))]
    args = [dec, decay] + [a for a, _ in srcs] + [gn, s0]
    aliases = {}
    for out_idx, buf in enumerate((b_buf, fin)):
        if buf is not None:
            aliases[len(args)] = out_idx
            in_specs.append(pl.BlockSpec(memory_space=pl.ANY))
            args.append(buf)
    fin_blk = (None,) * (len(fin_shape) - 3) + _STATE
    b_rows = n_batch * n_chunks * CHUNK if b_buf is None else b_buf.shape[0]
    return pl.pallas_call(
        functools.partial(_ret_kernel, n_valid=n_valid),
        out_shape=(jax.ShapeDtypeStruct((b_rows, D_MODEL), BF16),
                   jax.ShapeDtypeStruct(fin_shape, F32)),
        grid=(n_batch, n_chunks),
        in_specs=in_specs,
        out_specs=(pl.BlockSpec((CHUNK, D_MODEL), zmap(0)),
                   pl.BlockSpec(fin_blk, lambda b, c: fin_index(b) + (0, 0, 0))),
        scratch_shapes=[pltpu.VMEM(_STATE, F32)],
        input_output_aliases=aliases,
        compiler_params=_cparams("arbitrary", "arbitrary"),
        name="retention",
    )(*args)


TN_MERGE = 1024


def _merge_kernel(a_ref, b_ref, gpp_ref, grp_ref, gps_ref, grs_ref,
                  am_ref, bm_ref, gpm_ref, grm_ref, wp_ref, wr_ref, o_ref, om_ref, *, n_p, n_s):
    i = pl.program_id(1)

    def tile(a, b, gp, gr, out):
        pool_branch = _dot(a[...], wp_ref[...])
        ret_branch = _dot(b[...], wr_ref[...])
        out[...] = (jax.nn.sigmoid(gp[...]) * pool_branch + jax.nn.sigmoid(gr[...]) * ret_branch).astype(BF16)

    @pl.when(i < n_p)
    def _():
        tile(a_ref, b_ref, gpp_ref, grp_ref, o_ref)

    @pl.when((i >= n_p) & (i < n_p + n_s))
    def _():
        tile(a_ref, b_ref, gps_ref, grs_ref, o_ref)

    @pl.when(i == n_p + n_s)
    def _():
        tile(am_ref, bm_ref, gpm_ref, grm_ref, om_ref)


def _merge(a_in, b_in, z, zs, a_meta, b_meta, zm, proj_pool, proj_ret):
    n_p = z.shape[0] // TM
    n_s = zs.shape[0] // TM
    n_main = n_p + n_s
    assert a_in.shape[0] == n_main * TM and b_in.shape[0] == n_main * TM
    nt = D_MODEL // TN_MERGE

    def p_map(col):
        return lambda j, i: (jnp.minimum(i, n_p - 1), col(j))

    def s_map(col):
        return lambda j, i: (jnp.clip(i - n_p, 0, n_s - 1), col(j))

    row = pl.BlockSpec((TM, D_MODEL), lambda j, i: (jnp.minimum(i, n_main - 1), 0))
    gate = lambda ref_map, seg: pl.BlockSpec((TM, TN_MERGE), ref_map(lambda j: seg * nt + j))
    const = lambda j, i: (0, 0)
    return pl.pallas_call(
        functools.partial(_merge_kernel, n_p=n_p, n_s=n_s),
        out_shape=(jax.ShapeDtypeStruct((n_main * TM, D_MODEL), BF16),
                   jax.ShapeDtypeStruct((CHUNK, D_MODEL), BF16)),
        grid=(nt, n_main + 1),
        in_specs=[row, row,
                  gate(p_map, COL_GP), gate(p_map, COL_GR), gate(s_map, SEG_GP), gate(s_map, SEG_GR),
                  pl.BlockSpec((CHUNK, D_MODEL), const),
                  pl.BlockSpec((CHUNK, D_MODEL), const),
                  pl.BlockSpec((CHUNK, TN_MERGE), lambda j, i: (0, SEG_GP * nt + j)),
                  pl.BlockSpec((CHUNK, TN_MERGE), lambda j, i: (0, SEG_GR * nt + j)),
                  pl.BlockSpec((D_MODEL, TN_MERGE), lambda j, i: (0, j)),
                  pl.BlockSpec((D_MODEL, TN_MERGE), lambda j, i: (0, j))],
        out_specs=(pl.BlockSpec((TM, TN_MERGE), lambda j, i: (jnp.minimum(i, n_main - 1), j)),
                   pl.BlockSpec((CHUNK, TN_MERGE), lambda j, i: (0, j))),
        compiler_params=_cparams("arbitrary", "arbitrary"),
        name="merge_branches",
    )(a_in, b_in, z, z, zs, zs, a_meta, b_meta, zm, zm, proj_pool, proj_ret)


def _out_kernel(m_ref, mm_ref, w_ref, g_ref, *rest, first, last, n_p, n_s):
    rest = list(rest)
    x_refs = [rest.pop(0) for _ in range(3 if first else 2)]
    i = pl.program_id(0)

    def emit(x, merged, outs):
        xn = x + _dot(merged[...], w_ref[...])
        y = _rms(xn, g_ref[...])
        if last:
            outs[0][...] = y
        else:
            outs[0][...] = xn
            outs[1][...] = y.astype(BF16)

    main_outs = [rest[0:1], rest[1:2]] if last else [rest[0:2], rest[0:2]]

    @pl.when(i < n_p)
    def _():
        emit(x_refs[0][...], m_ref, main_outs[0])

    @pl.when((i >= n_p) & (i < n_p + n_s))
    def _():
        emit(x_refs[1 if first else 0][...], m_ref, main_outs[1])

    if not last:
        @pl.when(i == n_p + n_s)
        def _():
            emit(_meta_rows(x_refs[2]) if first else x_refs[1][...], mm_ref, rest[2:4])


def _out_projection(merged, merged_meta, w_out, g_next, x_parts, first, last, n_p, n_s):
    n = n_p + n_s
    main_spec = pl.BlockSpec((TM, D_MODEL), lambda i: (jnp.minimum(i, n - 1), 0))
    meta_spec = pl.BlockSpec((CHUNK, D_MODEL), lambda i: (0, 0))
    in_specs = [main_spec, meta_spec,
                pl.BlockSpec((D_MODEL, D_MODEL), lambda i: (0, 0)),
                pl.BlockSpec((1, D_MODEL), lambda i: (0, 0))]
    if first:
        in_specs += _x_specs(n_p, n_s) + [pl.BlockSpec((N_META, D_MODEL), lambda i: (0, 0))]
    else:
        in_specs += [main_spec, meta_spec]
    if last:
        out_shape = (jax.ShapeDtypeStruct((n_p * TM, D_MODEL), F32),
                     jax.ShapeDtypeStruct((n_s * TM, D_MODEL), F32))
        out_specs = tuple(_x_specs(n_p, n_s))
    else:
        out_shape = (jax.ShapeDtypeStruct((n * TM, D_MODEL), F32), jax.ShapeDtypeStruct((n * TM, D_MODEL), BF16),
                     jax.ShapeDtypeStruct((CHUNK, D_MODEL), F32), jax.ShapeDtypeStruct((CHUNK, D_MODEL), BF16))
        out_specs = (main_spec, main_spec, meta_spec, meta_spec)
    return pl.pallas_call(
        functools.partial(_out_kernel, first=first, last=last, n_p=n_p, n_s=n_s),
        out_shape=out_shape,
        grid=(n if last else n + 1,),
        in_specs=in_specs,
        out_specs=out_specs,
        compiler_params=_cparams("arbitrary"),
        name="out_projection",
    )(merged, merged_meta, w_out, g_next, *x_parts)


def _log_decay():
    return jnp.log1p(-jnp.exp2(-5.0 - jnp.arange(HEADS, dtype=F32)))


def _rope_tables(pos):
    inv_freq = ROPE_BASE ** (-jnp.arange(HALF, dtype=F32) / HALF)
    ang = pos[:, None] * inv_freq[None, :]
    return jnp.cos(ang), jnp.sin(ang)


def _chunk_decay(lg):
    idx = jnp.arange(CHUNK, dtype=F32)
    diff = idx[:, None] - idx[None, :]
    causal = diff >= 0
    return jnp.where(causal[None], jnp.exp(jnp.where(causal, diff, 0.0)[None] * lg[:, None, None]), 0.0)


def _sample_decay_tables(lg, n_new):
    r = jnp.arange(RB * HEADS * n_new)
    head = (r // n_new) % HEADS
    tok = (r % n_new).astype(F32)
    lg_r = lg[head]
    same = (r[:, None] // n_new) == (r[None, :] // n_new)
    diff = tok[:, None] - tok[None, :]
    keep = same & (diff >= 0)
    dmask = jnp.where(keep, jnp.exp(jnp.where(keep, diff, 0.0) * lg_r[:, None]), 0.0)
    qdec = jnp.broadcast_to(jnp.exp((tok + 1.0) * lg_r)[:, None], (r.shape[0], HEAD_DIM))
    kdec = jnp.broadcast_to(jnp.exp((n_new - 1.0 - tok) * lg_r)[:, None], (r.shape[0], HEAD_DIM))
    return dmask, qdec, kdec


def kernel(x_prompt, x_sample, state_pool, state_ret, meta_tokens, norm_gain, w_in, pool_w, pool_scale,
           ret_gn_gain, proj_pool, proj_ret, w_out, final_norm):
    n_batch, seq = x_prompt.shape[:2]
    n_seq, n_new = x_sample.shape[:2]
    depth = norm_gain.shape[0]
    rows_prompt = n_batch * seq
    rows_sample = n_seq * n_new
    assert seq % CHUNK == 0 and n_seq % RB == 0
    assert n_new == 8, "sample tokens of one sequence must fill one f32 sublane tile"
    assert rows_prompt % TM == 0 and rows_sample % TM == 0
    n_p = rows_prompt // TM
    n_s = rows_sample // TM

    xp = x_prompt.reshape(rows_prompt, D_MODEL).astype(F32)
    xs = x_sample.reshape(rows_sample, D_MODEL).astype(F32)
    meta = meta_tokens.astype(F32)
    state_pool = state_pool.astype(F32)
    state_ret = state_ret.astype(F32)
    w_in = w_in.astype(F32)

    trig_p = _rope_tables(N_META + jnp.arange(seq, dtype=F32))
    trig_s = tuple(jnp.tile(t, (n_seq, 1)) for t in _rope_tables(PAST_LEN + jnp.arange(n_new, dtype=F32)))
    trig_m = _rope_tables(jnp.where(jnp.arange(CHUNK) < N_META, jnp.arange(CHUNK, dtype=F32), 0.0))
    lg = _log_decay()
    decay = _chunk_decay(lg)
    gam_new = jnp.exp(n_new * lg)
    tables = _sample_decay_tables(lg, n_new)
    zero_state = jnp.zeros(_STATE, F32)
    row = lambda a: a.reshape(1, D_MODEL).astype(F32)

    h, h_meta = _first_norm(xp, xs, meta, row(norm_gain[0]), n_p, n_s)
    x_parts = (xp, xs, meta)
    pool_p = []
    ret_p = pool_s = ret_s = None
    pw = pool_w.astype(BF16)
    out_weights = (proj_pool.astype(F32), proj_ret.astype(F32), w_out.astype(F32))
    state_pool_t = jnp.transpose(state_pool, (0, 2, 1, 3))
    for l in range(depth):
        ps = row(pool_scale[l])
        gn = row(ret_gn_gain[l])

        zs, zm, wb = _in_projection_small(h, h_meta, w_in, l, trig_s, trig_m, rows_prompt, rows_sample)
        b_meta, s_meta = _retention(((zm, SEG_Q), (zm, SEG_K), (zm, SEG_V), (zm, SEG_RG)), lg, gn, decay,
                                    zero_state, None, None, _STATE, lambda b: (), 1, 1, N_META)
        sample = (gam_new, gn, tables, state_ret, l, n_new, zs, rows_prompt)
        n_groups = n_seq // RB
        a_in, hist_p, b_in, ret_s = _proj_pool(h, wb, zm, pw, ps, l, n_batch, seq, sample, ret_s)
        done = (D_MODEL // TN_IN) * (rows_prompt // TM_IN)
        value_groups = min(n_groups - done, len(VALUE_SEGS) * PER_SEG * (rows_prompt // TM_IN))
        zv, b_in, ret_s = _in_projection(h, wb, rows_prompt, VALUE_SEGS,
                                         sample=(sample, done, value_groups, b_in, ret_s))
        done += value_groups
        zr, wp_l, wr_l, wo_l, b_in, ret_s = _in_projection(
            h, wb, rows_prompt, ROPE_SEGS, trig=trig_p, seq=seq, convert=(out_weights, l),
            sample=(sample, done, n_groups - done, b_in, ret_s))
        zg, b_in, ret_p = _in_projection(
            h, wb, rows_prompt, GATE_SEGS,
            chunks=(zr, zv, lg, gn, decay, s_meta, b_in, ret_p, (depth, n_batch) + _STATE, l, n_batch,
                    seq // CHUNK))

        a_meta = _pool_meta(zm, pw, ps, l)
        a_in, pool_s = _pool_sample(zs, state_pool_t, a_in, pool_s, pw, ps, l, n_seq, n_new, rows_prompt)
        pool_p.append(hist_p)

        merged, merged_meta = _merge(a_in, b_in, zg, zs, a_meta, b_meta, zm, wp_l, wr_l)
        last = l == depth - 1
        g_next = row(final_norm if last else norm_gain[l + 1])
        outs = _out_projection(merged, merged_meta, wo_l, g_next, x_parts, l == 0, last, n_p, n_s)
        if last:
            y_prompt, y_sample = outs
        else:
            x_new, h, x_meta, h_meta = outs
            x_parts = (x_new, x_meta)

    return (y_prompt.reshape(n_batch, seq, D_MODEL), y_sample.reshape(n_seq, n_new, D_MODEL),
            jnp.stack(pool_p), ret_p, jnp.transpose(pool_s, (0, 2, 1, 3)), ret_s)
```

```python
import functools

import jax
import jax.numpy as jnp
from jax import lax
from jax.experimental import pallas as pl
from jax.experimental.pallas import tpu as pltpu

F32 = jnp.float32
BF16 = jnp.bfloat16

D_MODEL = 2048
N_META = 16
POOL_WINDOWS = (2, 4, 8, 16)
N_GROUPS = len(POOL_WINDOWS)
POOL_GROUP = D_MODEL // N_GROUPS
POOL_HIST = max(POOL_WINDOWS) - 1
HEADS = 8
HEAD_DIM = D_MODEL // HEADS
HALF = HEAD_DIM // 2
CHUNK = 128
ROPE_BASE = 10000.0
EPS = 1e-6
PAST_LEN = 16384
SEG_U, SEG_PG, SEG_Q, SEG_K, SEG_V, SEG_RG, SEG_GP, SEG_GR = range(8)
N_SEG = 8
ROPE_SEGS = (SEG_Q, SEG_K)
COL_Q, COL_K = range(len(ROPE_SEGS))
VALUE_SEGS = (SEG_V, SEG_RG)
COL_V, COL_RG = range(len(VALUE_SEGS))
GATE_SEGS = (SEG_GP, SEG_GR)
COL_GP, COL_GR = range(len(GATE_SEGS))

TM = 512
TM_IN = 1024
TN_IN = 1024
PER_SEG = D_MODEL // TN_IN
VMEM_LIMIT = 56 * 1024 * 1024


VMEM_LIMIT_MAX = 60 * 1024 * 1024


def _cparams(*sem, vmem=VMEM_LIMIT):
    return pltpu.CompilerParams(dimension_semantics=sem, vmem_limit_bytes=vmem)


def _dot(a, b):
    return jnp.dot(a, b, preferred_element_type=F32)


def _dot_nt(a, b):
    return lax.dot_general(a, b, (((1,), (1,)), ((), ())), preferred_element_type=F32)


def _dot_tn(a, b):
    return lax.dot_general(a, b, (((0,), (0,)), ((), ())), preferred_element_type=F32)


def _rms(x, g):
    return x * lax.rsqrt(jnp.mean(x * x, axis=-1, keepdims=True) + EPS) * g


def _silu(x):
    return x * jax.nn.sigmoid(x)


def _x_specs(n_p, n_s):
    return [pl.BlockSpec((TM, D_MODEL), lambda i: (jnp.minimum(i, n_p - 1), 0)),
            pl.BlockSpec((TM, D_MODEL), lambda i: (jnp.clip(i - n_p, 0, n_s - 1), 0))]


def _meta_rows(meta_ref):
    return jnp.concatenate([meta_ref[...], jnp.zeros((CHUNK - N_META, D_MODEL), F32)], axis=0)


def _norm_kernel(xp_ref, xs_ref, meta_ref, g_ref, h_ref, hm_ref, *, n_p, n_s):
    i = pl.program_id(0)

    @pl.when(i < n_p)
    def _():
        h_ref[...] = _rms(xp_ref[...], g_ref[...]).astype(BF16)

    @pl.when((i >= n_p) & (i < n_p + n_s))
    def _():
        h_ref[...] = _rms(xs_ref[...], g_ref[...]).astype(BF16)

    @pl.when(i == n_p + n_s)
    def _():
        hm_ref[...] = _rms(_meta_rows(meta_ref), g_ref[...]).astype(BF16)


def _first_norm(xp, xs, meta, g, n_p, n_s):
    n = n_p + n_s
    return pl.pallas_call(
        functools.partial(_norm_kernel, n_p=n_p, n_s=n_s),
        out_shape=(jax.ShapeDtypeStruct((n * TM, D_MODEL), BF16),
                   jax.ShapeDtypeStruct((CHUNK, D_MODEL), BF16)),
        grid=(n + 1,),
        in_specs=_x_specs(n_p, n_s) + [pl.BlockSpec((N_META, D_MODEL), lambda i: (0, 0)),
                                       pl.BlockSpec((1, D_MODEL), lambda i: (0, 0))],
        out_specs=(pl.BlockSpec((TM, D_MODEL), lambda i: (jnp.minimum(i, n - 1), 0)),
                   pl.BlockSpec((CHUNK, D_MODEL), lambda i: (0, 0))),
        compiler_params=_cparams("arbitrary"),
        name="first_norm",
    )(xp, xs, meta, g)


RB = 2
_STATE = (HEADS, HEAD_DIM, HEAD_DIM)


def _head_norm_gate(o, gain, gate):
    mu = jnp.mean(o, axis=-1, keepdims=True)
    oc = o - mu
    on = oc * lax.rsqrt(jnp.mean(oc * oc, axis=-1, keepdims=True) + EPS)
    return ((on * gain) * _silu(gate)).astype(BF16)


def _sample_body(gam_ref, q_ref, k_ref, v_ref, rg_ref, gn_ref, dmask_ref, qdec_ref, kdec_ref, s_ref,
                 o_ref, sout_ref, n_new):
    pairs = [(b, h) for b in range(RB) for h in range(HEADS)]

    def stack(ref):
        return jnp.concatenate(
            [ref[b * n_new:(b + 1) * n_new, h * HEAD_DIM:(h + 1) * HEAD_DIM] for b, h in pairs], axis=0)

    k = stack(k_ref)
    qb = stack(q_ref).astype(BF16)
    v = stack(v_ref)
    vb = v.astype(BF16)
    scores = _dot_nt(qb, k.astype(BF16)) * dmask_ref[...]
    intra = _dot(scores.astype(BF16), vb)
    k_dec = (k * kdec_ref[...]).astype(BF16)
    rows = lax.broadcasted_iota(jnp.int32, v.shape, 0)
    is_even = (rows & n_new) == 0
    v_even = jnp.where(is_even, v, 0.0).astype(BF16)
    v_odd = jnp.where(is_even, 0.0, v).astype(BF16)
    grp = 2 * n_new
    inter_parts = []
    for idx, (b, h) in enumerate(pairs):
        g0 = (idx // 2) * grp
        off = (idx % 2) * n_new
        s_old = s_ref[b, h]
        inter = _dot(qb[g0:g0 + grp], s_old.astype(BF16))
        inter_parts.append(inter[off:off + n_new])
        v_sel = v_even if idx % 2 == 0 else v_odd
        sout_ref[b, h] = s_old * gam_ref[h] + _dot_tn(k_dec[g0:g0 + grp], v_sel[g0:g0 + grp])
    o = intra + jnp.concatenate(inter_parts, axis=0) * qdec_ref[...]
    gain = jnp.concatenate(
        [jnp.broadcast_to(gn_ref[:, h * HEAD_DIM:(h + 1) * HEAD_DIM], (n_new, HEAD_DIM)) for _, h in pairs],
        axis=0)
    out = _head_norm_gate(o, gain, stack(rg_ref)).astype(F32)
    seqs = []
    for b in range(RB):
        seqs.append(jnp.concatenate(
            [out[(b * HEADS + h) * n_new:(b * HEADS + h + 1) * n_new] for h in range(HEADS)], axis=1))
    o_ref[...] = jnp.concatenate(seqs, axis=0).astype(BF16)


def _rope_store(z, o_ref, cos_ref, sin_ref, scale):
    cos = cos_ref[...]
    sin = sin_ref[...]
    for hh in range(TN_IN // HEAD_DIM):
        lo = slice(hh * HEAD_DIM, hh * HEAD_DIM + HALF)
        hi = slice(hh * HEAD_DIM + HALF, (hh + 1) * HEAD_DIM)
        t1 = z[:, lo]
        t2 = z[:, hi]
        o_ref[:, lo] = (t1 * cos - t2 * sin) * scale
        o_ref[:, hi] = (t2 * cos + t1 * sin) * scale


def _k_scale(is_k):
    return jnp.where(is_k, HEAD_DIM ** -0.5, 1.0).astype(F32)


def _inproj_small_kernel(hs_ref, hm_ref, w_ref, cos_ref, sin_ref, cosm_ref, sinm_ref, zs_ref, zm_ref, wb_ref):
    s = pl.program_id(0)
    wb_ref[...] = w_ref[...].astype(BF16)
    zs_ref[...] = _dot(hs_ref[...], wb_ref[...])
    zm_ref[...] = _dot(hm_ref[...], wb_ref[...])

    @pl.when((s >= SEG_Q * PER_SEG) & (s < (SEG_K + 1) * PER_SEG))
    def _():
        scale = _k_scale(s >= SEG_K * PER_SEG)
        _rope_store(zs_ref[...], zs_ref, cos_ref, sin_ref, scale)
        _rope_store(zm_ref[...], zm_ref, cosm_ref, sinm_ref, scale)


def _in_projection_small(h, h_meta, w_in, layer, trig_s, trig_m, row0, rows):
    assert row0 % rows == 0
    const = lambda s: (0, 0)
    col = lambda s: (0, s)
    return pl.pallas_call(
        _inproj_small_kernel,
        out_shape=(jax.ShapeDtypeStruct((rows, N_SEG * D_MODEL), F32),
                   jax.ShapeDtypeStruct((CHUNK, N_SEG * D_MODEL), F32),
                   jax.ShapeDtypeStruct((D_MODEL, N_SEG * D_MODEL), BF16)),
        grid=(N_SEG * PER_SEG,),
        in_specs=[pl.BlockSpec((rows, D_MODEL), lambda s: (row0 // rows, 0)),
                  pl.BlockSpec((CHUNK, D_MODEL), const),
                  pl.BlockSpec((None, D_MODEL, TN_IN), lambda s: (layer, 0, s)),
                  pl.BlockSpec((rows, HALF), const),
                  pl.BlockSpec((rows, HALF), const),
                  pl.BlockSpec((CHUNK, HALF), const),
                  pl.BlockSpec((CHUNK, HALF), const)],
        out_specs=(pl.BlockSpec((rows, TN_IN), col),
                   pl.BlockSpec((CHUNK, TN_IN), col),
                   pl.BlockSpec((D_MODEL, TN_IN), col)),
        compiler_params=_cparams("arbitrary"),
        name="in_projection_small",
    )(h, h_meta, w_in, *trig_s, *trig_m)


N_SAMPLE_IN = 10


def _sample_operands(sample, group_of):
    gam, gn, (dmask, qdec, kdec), state_ret, layer, n_new, zs, prompt_rows = sample
    rows_g = RB * n_new
    stack_rows = RB * HEADS * n_new
    assert prompt_rows % rows_g == 0

    def zmap(seg):
        return lambda *idx: (group_of(*idx), seg)

    const = lambda *idx: (0, 0)
    sblk = (None, RB) + _STATE
    state_map = lambda *idx: (layer, group_of(*idx), 0, 0, 0)
    in_specs = [pl.BlockSpec(memory_space=pltpu.SMEM),
                pl.BlockSpec((rows_g, D_MODEL), zmap(SEG_Q)),
                pl.BlockSpec((rows_g, D_MODEL), zmap(SEG_K)),
                pl.BlockSpec((rows_g, D_MODEL), zmap(SEG_V)),
                pl.BlockSpec((rows_g, D_MODEL), zmap(SEG_RG)),
                pl.BlockSpec((1, D_MODEL), const),
                pl.BlockSpec((stack_rows, stack_rows), const),
                pl.BlockSpec((stack_rows, HEAD_DIM), const),
                pl.BlockSpec((stack_rows, HEAD_DIM), const),
                pl.BlockSpec(sblk, state_map)]
    args = [gam, zs, zs, zs, zs, gn, dmask, qdec, kdec, state_ret]
    out_shape = [jax.ShapeDtypeStruct((prompt_rows + zs.shape[0], D_MODEL), BF16),
                 jax.ShapeDtypeStruct(state_ret.shape, F32)]
    out_specs = [pl.BlockSpec((rows_g, D_MODEL), lambda *idx: (prompt_rows // rows_g + group_of(*idx), 0)),
                 pl.BlockSpec(sblk, state_map)]
    return in_specs, args, out_shape, out_specs


N_CHUNK_IN = 8
CHUNKS_PER_STEP = 2
STEP_CHUNK = CHUNKS_PER_STEP * CHUNK


def _inproj_kernel(h_ref, w_ref, *rest, rope, n_convert, n_new, n_groups, chunk_steps):
    pos = 0
    if rope:
        cos_ref, sin_ref = rest[:2]
        pos = 2
    conv_in = rest[pos:pos + n_convert]
    pos += n_convert
    sample_refs = rest[pos:pos + N_SAMPLE_IN] if n_new is not None else None
    chunk_refs = rest[pos:pos + N_CHUNK_IN] if chunk_steps is not None else None
    n_scr = 0 if chunk_steps is None else 1
    n_out = 1 + n_convert + (0 if n_new is None else 2) + (0 if chunk_steps is None else 2)
    outs = rest[len(rest) - n_out - n_scr:len(rest) - n_scr]
    o_ref = outs[0]
    conv_out = outs[1:1 + n_convert]
    step = pl.program_id(0) * pl.num_programs(1) + pl.program_id(1)

    def tile():
        z = _dot(h_ref[...], w_ref[...])
        if rope:
            _rope_store(z, o_ref, cos_ref, sin_ref, _k_scale(pl.program_id(0) >= PER_SEG))
        else:
            o_ref[...] = z
        for src, dst in zip(conv_in, conv_out):
            dst[...] = src[...].astype(BF16)

    if chunk_steps is not None:
        dec_ref, decay_ref, q_ref, k_ref, v_ref, rg_ref, gn_ref, s0_ref = chunk_refs
        bo_ref, fin_ref = outs[-2:]
        s_scr = rest[-1]

        @pl.when(step % chunk_steps == 0)
        def _():
            s_scr[...] = s0_ref[...]

        _chunk_body(dec_ref, decay_ref, q_ref, k_ref, v_ref, rg_ref, gn_ref, s_scr, bo_ref, STEP_CHUNK)
        tile()

        @pl.when(step % chunk_steps == chunk_steps - 1)
        def _():
            fin_ref[...] = s_scr[...]
    elif n_new is not None:
        so_ref, sout_ref = outs[-2:]

        @pl.when(step < n_groups)
        def _():
            tile()
            _sample_body(*sample_refs, so_ref, sout_ref, n_new)

        @pl.when(step >= n_groups)
        def _():
            tile()
    else:
        tile()


def _in_projection(h, wb, rows, segs, trig=None, seq=None, convert=None, sample=None, chunks=None):
    assert rows % TM_IN == 0 and (sample is None or chunks is None)
    tiles = rows // TM_IN
    steps = len(segs) * PER_SEG * tiles
    step_of = lambda s, i: s * tiles + i
    rope = trig is not None
    gap_at = next((j for j in range(1, len(segs)) if segs[j] != segs[j - 1] + 1), len(segs))
    gap = segs[gap_at] - segs[gap_at - 1] - 1 if gap_at < len(segs) else 0
    assert all(segs[j] == segs[0] + j + (gap if j >= gap_at else 0) for j in range(len(segs)))

    def wmap(s, i):
        j = s // PER_SEG
        seg = segs[0] + j + jnp.where(j >= gap_at, gap, 0)
        return (0, seg * PER_SEG + s % PER_SEG)

    in_specs = [pl.BlockSpec((TM_IN, D_MODEL), lambda s, i: (i, 0)),
                pl.BlockSpec((D_MODEL, TN_IN), wmap)]
    args = [h, wb]
    out_shape = [jax.ShapeDtypeStruct((rows, len(segs) * D_MODEL), F32)]
    out_specs = [pl.BlockSpec((TM_IN, TN_IN), lambda s, i: (i, s))]
    alias_bufs = []
    scratch = []
    n_convert = 0
    n_new = n_groups = chunk_steps = None
    if rope:
        assert seq % TM_IN == 0
        in_specs += [pl.BlockSpec((TM_IN, HALF), lambda s, i: (i % (seq // TM_IN), 0)) for _ in range(2)]
        args += list(trig)
    if convert is not None:
        weights, w_layer = convert
        n_convert = len(weights)
        busy = 0 if sample is None or sample[2] == steps else sample[2]
        assert D_MODEL % (steps - busy) == 0
        w_rows = D_MODEL // (steps - busy)
        w_block = lambda s, i: jnp.maximum(step_of(s, i) - busy, 0)
        in_specs += [pl.BlockSpec((None, w_rows, D_MODEL), lambda s, i: (w_layer, w_block(s, i), 0))
                     for _ in weights]
        args += list(weights)
        out_shape += [jax.ShapeDtypeStruct((D_MODEL, D_MODEL), BF16) for _ in weights]
        out_specs += [pl.BlockSpec((w_rows, D_MODEL), lambda s, i: (w_block(s, i), 0)) for _ in weights]
    if sample is not None:
        sample, first_group, n_groups, b_buf, new_state = sample
        n_new = sample[5]
        assert 0 < n_groups <= steps, "one group of sample sequences per grid step"

        def group(s, i):
            return first_group + jnp.minimum(step_of(s, i), n_groups - 1)

        s_in, s_args, s_shape, s_specs = _sample_operands(sample, group)
        in_specs += s_in
        args += s_args
        alias_bufs += [(len(out_shape), b_buf), (len(out_shape) + 1, new_state)]
        out_shape += s_shape
        out_specs += s_specs
    if chunks is not None:
        qk, vg, lg, gn, decay, s0, b_buf, fin, fin_shape, layer, n_batch, n_chunks = chunks
        assert steps * CHUNKS_PER_STEP == n_batch * n_chunks and n_chunks % CHUNKS_PER_STEP == 0
        chunk_steps = n_chunks // CHUNKS_PER_STEP
        crows = STEP_CHUNK
        assert decay.shape == (HEADS, crows, crows)

        def cmap(col):
            return lambda s, i: (step_of(s, i), col)

        in_specs += [pl.BlockSpec(memory_space=pltpu.SMEM),
                     pl.BlockSpec((HEADS, crows, crows), lambda s, i: (0, 0, 0)),
                     pl.BlockSpec((crows, D_MODEL), cmap(COL_Q)),
                     pl.BlockSpec((crows, D_MODEL), cmap(COL_K)),
                     pl.BlockSpec((crows, D_MODEL), cmap(COL_V)),
                     pl.BlockSpec((crows, D_MODEL), cmap(COL_RG)),
                     pl.BlockSpec((1, D_MODEL), lambda s, i: (0, 0)),
                     pl.BlockSpec(_STATE, lambda s, i: (0, 0, 0))]
        args += [jnp.stack([lg, jnp.exp(crows * lg)]), decay, qk, qk, vg, vg, gn, s0]
        alias_bufs += [(len(out_shape), b_buf)]
        if fin is not None:
            alias_bufs += [(len(out_shape) + 1, fin)]
        out_shape += [jax.ShapeDtypeStruct(b_buf.shape, BF16), jax.ShapeDtypeStruct(fin_shape, F32)]
        out_specs += [pl.BlockSpec((crows, D_MODEL), cmap(0)),
                      pl.BlockSpec((None, None) + _STATE,
                                   lambda s, i: (layer, step_of(s, i) // chunk_steps, 0, 0, 0))]
        scratch = [pltpu.VMEM(_STATE, F32)]
    aliases = {}
    for out_idx, buf in alias_bufs:
        aliases[len(args)] = out_idx
        in_specs.append(pl.BlockSpec(memory_space=pl.ANY))
        args.append(buf)
    outs = pl.pallas_call(
        functools.partial(_inproj_kernel, rope=rope, n_convert=n_convert, n_new=n_new, n_groups=n_groups,
                          chunk_steps=chunk_steps),
        out_shape=tuple(out_shape),
        grid=(len(segs) * PER_SEG, tiles),
        in_specs=in_specs,
        out_specs=tuple(out_specs),
        scratch_shapes=scratch,
        input_output_aliases=aliases,
        compiler_params=_cparams("arbitrary", "arbitrary",
                                 vmem=VMEM_LIMIT_MAX if (convert and sample) or chunks else VMEM_LIMIT),
        name="in_projection_" + "_".join(str(g) for g in segs),
    )(*args)
    return outs if len(outs) > 1 else outs[0]


def _window_sum(ext, w, base, rows):
    s = ext
    size = 1
    while size < w:
        s = s[size:] + s[:-size]
        size *= 2
    start = base - (w - 1)
    return s[start:start + rows]


def _pool_group(g, wsum, u, inv_cnt, pw_ref, ps_ref, gate):
    sl = slice(g * POOL_GROUP, (g + 1) * POOL_GROUP)
    pooled = wsum * inv_cnt - u
    mixed = _dot(pooled.astype(BF16), pw_ref[g])
    return (mixed * ps_ref[:, sl] * _silu(gate)).astype(BF16)


def _pool_w_spec(layer):
    return pl.BlockSpec((None, N_GROUPS, POOL_GROUP, POOL_GROUP), lambda *_: (layer, 0, 0, 0))


def _pool_meta_kernel(u_ref, pg_ref, pw_ref, ps_ref, o_ref):
    avail = lax.broadcasted_iota(jnp.int32, (N_META, 1), 0).astype(F32) + 1.0
    o_ref[N_META:] = jnp.zeros((CHUNK - N_META, D_MODEL), BF16)
    for g, w in enumerate(POOL_WINDOWS):
        sl = slice(g * POOL_GROUP, (g + 1) * POOL_GROUP)
        u = u_ref[:, sl]
        wsum = _window_sum(jnp.concatenate([jnp.zeros_like(u), u], axis=0), w, N_META, N_META)
        inv_cnt = 1.0 / jnp.minimum(float(w), avail)
        o_ref[0:N_META, sl] = _pool_group(g, wsum, u, inv_cnt, pw_ref, ps_ref, pg_ref[:, sl])


def _pool_meta(zm, pool_w, pool_scale, layer):
    return pl.pallas_call(
        _pool_meta_kernel,
        out_shape=jax.ShapeDtypeStruct((CHUNK, D_MODEL), BF16),
        grid=(1,),
        in_specs=[pl.BlockSpec((N_META, D_MODEL), lambda i: (0, SEG_U)),
                  pl.BlockSpec((N_META, D_MODEL), lambda i: (0, SEG_PG)),
                  _pool_w_spec(layer),
                  pl.BlockSpec((1, D_MODEL), lambda i: (0, 0))],
        out_specs=pl.BlockSpec((CHUNK, D_MODEL), lambda i: (0, 0)),
        compiler_params=_cparams("arbitrary"),
        name="pool_meta",
    )(zm, zm, pool_w, pool_scale)


GROUPS_PER_BLOCK = TN_IN // POOL_GROUP


def _proj_pool_kernel(h_ref, wu_ref, wpg_ref, meta_ref, pw_ref, ps_ref, *rest, tiles_per_seq, n_new):
    sample_refs = rest[:N_SAMPLE_IN]
    o_ref, hist_ref, so_ref, sout_ref, carry = rest[-5:]
    c = pl.program_id(0)
    i = pl.program_id(1)
    first = (i % tiles_per_seq) == 0
    last = (i % tiles_per_seq) == tiles_per_seq - 1
    for cc in range(D_MODEL // TN_IN):
        @pl.when(c == cc)
        def _(cc=cc):
            u = _dot(h_ref[...], wu_ref[...])
            pg = _dot(h_ref[...], wpg_ref[...])
            prev = jnp.where(first, meta_ref[...], carry[...])
            for gg in range(GROUPS_PER_BLOCK):
                w = POOL_WINDOWS[cc * GROUPS_PER_BLOCK + gg]
                sl = slice(gg * POOL_GROUP, (gg + 1) * POOL_GROUP)
                wsum = _window_sum(jnp.concatenate([prev[:, sl], u[:, sl]], axis=0), w, N_META, TM_IN)
                pooled = wsum * (1.0 / w) - u[:, sl]
                mixed = _dot(pooled.astype(BF16), pw_ref[gg])
                o_ref[:, sl] = (mixed * ps_ref[:, sl] * _silu(pg[:, sl])).astype(BF16)
            carry[...] = u[TM_IN - N_META:]
            _sample_body(*sample_refs, so_ref, sout_ref, n_new)

            @pl.when(last)
            def _():
                hist_ref[0] = u[TM_IN - POOL_HIST:]


def _proj_pool(h, wb, zm, pool_w, pool_scale, layer, n_batch, seq, sample, new_state):
    assert seq % TM_IN == 0 and TN_IN % POOL_GROUP == 0
    tiles_per_seq = seq // TM_IN
    tiles = n_batch * tiles_per_seq
    step = lambda c, i: c * tiles + i
    in_specs = [pl.BlockSpec((TM_IN, D_MODEL), lambda c, i: (i, 0)),
                pl.BlockSpec((D_MODEL, TN_IN), lambda c, i: (0, SEG_U * PER_SEG + c)),
                pl.BlockSpec((D_MODEL, TN_IN), lambda c, i: (0, SEG_PG * PER_SEG + c)),
                pl.BlockSpec((N_META, TN_IN), lambda c, i: (0, SEG_U * PER_SEG + c)),
                pl.BlockSpec((None, GROUPS_PER_BLOCK, POOL_GROUP, POOL_GROUP), lambda c, i: (layer, c, 0, 0)),
                pl.BlockSpec((1, TN_IN), lambda c, i: (0, c))]
    args = [h, wb, wb, zm, pool_w, pool_scale]
    s_in, s_args, s_shape, s_specs = _sample_operands(sample, step)
    in_specs += s_in
    args += s_args
    total_rows = s_shape[0].shape[0]
    out_shape = [jax.ShapeDtypeStruct((total_rows, D_MODEL), BF16),
                 jax.ShapeDtypeStruct((n_batch, POOL_HIST, D_MODEL), F32)] + s_shape
    out_specs = [pl.BlockSpec((TM_IN, TN_IN), lambda c, i: (i, c)),
                 pl.BlockSpec((1, POOL_HIST, TN_IN), lambda c, i: (i // tiles_per_seq, 0, c))] + s_specs
    aliases = {}
    if new_state is not None:
        aliases[len(args)] = 3
        in_specs.append(pl.BlockSpec(memory_space=pl.ANY))
        args.append(new_state)
    return pl.pallas_call(
        functools.partial(_proj_pool_kernel, tiles_per_seq=tiles_per_seq, n_new=sample[5]),
        out_shape=tuple(out_shape),
        grid=(D_MODEL // TN_IN, tiles),
        in_specs=in_specs,
        out_specs=tuple(out_specs),
        scratch_shapes=[pltpu.VMEM((N_META, TN_IN), F32)],
        input_output_aliases=aliases,
        compiler_params=_cparams("arbitrary", "arbitrary"),
        name="projection_pool",
    )(*args)


LANE = 128
LANE_TILES = POOL_GROUP // LANE


def _pool_sample_kernel(hist_ref, *rest, n_seq, n_new):
    u_refs = rest[:LANE_TILES]
    pg_ref, pw_ref, ps_ref = rest[LANE_TILES:LANE_TILES + 3]
    o_ref, nh_ref, pooled_scr = rest[-3 - LANE_TILES:-LANE_TILES]
    mix_scrs = rest[-LANE_TILES:]
    g = pl.program_id(0)

    def token_rows(t):
        return pl.ds(t, n_seq, stride=n_new)

    u_t = [jnp.concatenate([r[token_rows(t), :] for r in u_refs], axis=1) for t in range(n_new)]
    ext = [hist_ref[j] for j in range(POOL_HIST)] + u_t
    for j in range(POOL_HIST):
        nh_ref[j] = ext[n_new + j]
    for k, w in enumerate(POOL_WINDOWS):
        @pl.when(g == k)
        def _(w=w):
            s = ext
            size = 1
            while size < w:
                s = [s[i + size] + s[i] for i in range(len(s) - size)]
                size *= 2
            for t in range(n_new):
                pooled_scr[t * n_seq:(t + 1) * n_seq] = s[POOL_HIST + 1 + t - w] * (1.0 / w) - u_t[t]
    mixed = _dot(pooled_scr[...].astype(BF16), pw_ref[...])
    for t in range(n_new):
        for c, scr in enumerate(mix_scrs):
            scr[token_rows(t), :] = mixed[t * n_seq:(t + 1) * n_seq, c * LANE:(c + 1) * LANE]
    mixed = jnp.concatenate([scr[...] for scr in mix_scrs], axis=1)
    o_ref[...] = (mixed * ps_ref[...] * _silu(pg_ref[...])).astype(BF16)


def _pool_sample(zs, state_pool_t, a_buf, new_hist, pool_w, pool_scale, layer, n_seq, n_new, row0):
    rows = n_seq * n_new
    assert row0 % rows == 0
    hist_blk = (None, POOL_HIST, n_seq, POOL_GROUP)
    hist_map = lambda g: (layer, 0, 0, g)
    per_seg = D_MODEL // POOL_GROUP

    def u_tile(c):
        return pl.BlockSpec((rows, LANE), lambda g: (0, (SEG_U * per_seg + g) * LANE_TILES + c))

    in_specs = [pl.BlockSpec(hist_blk, hist_map)] + [u_tile(c) for c in range(LANE_TILES)]
    in_specs += [pl.BlockSpec((rows, POOL_GROUP), lambda g: (0, SEG_PG * per_seg + g)),
                 pl.BlockSpec((None, None, POOL_GROUP, POOL_GROUP), lambda g: (layer, g, 0, 0)),
                 pl.BlockSpec((1, POOL_GROUP), lambda g: (0, g)),
                 pl.BlockSpec(memory_space=pl.ANY)]
    args = [state_pool_t] + [zs] * LANE_TILES + [zs, pool_w, pool_scale, a_buf]
    aliases = {len(args) - 1: 0}
    if new_hist is not None:
        aliases[len(args)] = 1
        in_specs.append(pl.BlockSpec(memory_space=pl.ANY))
        args.append(new_hist)
    return pl.pallas_call(
        functools.partial(_pool_sample_kernel, n_seq=n_seq, n_new=n_new),
        out_shape=(jax.ShapeDtypeStruct(a_buf.shape, BF16),
                   jax.ShapeDtypeStruct(state_pool_t.shape, F32)),
        grid=(N_GROUPS,),
        in_specs=in_specs,
        out_specs=(pl.BlockSpec((rows, POOL_GROUP), lambda g: (row0 // rows, g)),
                   pl.BlockSpec(hist_blk, hist_map)),
        scratch_shapes=[pltpu.VMEM((rows, POOL_GROUP), F32)] + [pltpu.VMEM((rows, LANE), F32)] * LANE_TILES,
        input_output_aliases=aliases,
        compiler_params=_cparams("arbitrary"),
        name="pool_sample",
    )(*args)


def _chunk_body(dec_ref, decay_ref, q_ref, k_ref, v_ref, rg_ref, gn_ref, s_scr, o_ref, n_valid,
                chunks=(slice(None),)):
    ridx = lax.broadcasted_iota(jnp.int32, (decay_ref.shape[1], 1), 0).astype(F32)
    for h in range(HEADS):
        hs = slice(h * HEAD_DIM, (h + 1) * HEAD_DIM)
        lg = dec_ref[0, h]
        q_dec = jnp.exp((ridx + 1.0) * lg)
        k_dec = jnp.exp((n_valid - 1.0 - ridx) * lg)
        state = s_scr[h]
        for rows in chunks:
            q = q_ref[rows, hs].astype(BF16)
            k = k_ref[rows, hs]
            v = v_ref[rows, hs].astype(BF16)
            scores = _dot_nt(q, k.astype(BF16)) * decay_ref[h]
            intra = _dot(scores.astype(BF16), v)
            inter = _dot(q, state.astype(BF16)) * q_dec
            state = state * dec_ref[1, h] + _dot_tn((k * k_dec).astype(BF16), v)
            o_ref[rows, hs] = _head_norm_gate(intra + inter, gn_ref[:, hs], rg_ref[rows, hs])
        s_scr[h] = state


def _ret_kernel(dec_ref, decay_ref, q_ref, k_ref, v_ref, rg_ref, gn_ref, s0_ref, *rest, n_valid):
    o_ref, sfin_ref, s_scr = rest[-3:]
    c = pl.program_id(1)

    @pl.when(c == 0)
    def _():
        s_scr[...] = s0_ref[...]

    _chunk_body(dec_ref, decay_ref, q_ref, k_ref, v_ref, rg_ref, gn_ref, s_scr, o_ref, n_valid)

    @pl.when(c == pl.num_programs(1) - 1)
    def _():
        sfin_ref[...] = s_scr[...]


def _retention(srcs, lg, gn, decay, s0, b_buf, fin, fin_shape, fin_index, n_batch, n_chunks, n_valid):
    dec = jnp.stack([lg, jnp.exp(n_valid * lg)])

    def zmap(col):
        return lambda b, c: (b * n_chunks + c, col)

    in_specs = [pl.BlockSpec(memory_space=pltpu.SMEM),
                pl.BlockSpec((HEADS, CHUNK, CHUNK), lambda b, c: (0, 0, 0))]
    in_specs += [pl.BlockSpec((CHUNK, D_MODEL), zmap(col)) for _, col in srcs]
    in_specs += [pl.BlockSpec((1, D_MODEL), lambda b, c: (0, 0)),
                 pl.BlockSpec(_STATE, lambda b, c: (0, 0, 0))]
    args = [dec, decay] + [a for a, _ in srcs] + [gn, s0]
    aliases = {}
    for out_idx, buf in enumerate((b_buf, fin)):
        if buf is not None:
            aliases[len(args)] = out_idx
            in_specs.append(pl.BlockSpec(memory_space=pl.ANY))
            args.append(buf)
    fin_blk = (None,) * (len(fin_shape) - 3) + _STATE
    b_rows = n_batch * n_chunks * CHUNK if b_buf is None else b_buf.shape[0]
    return pl.pallas_call(
        functools.partial(_ret_kernel, n_valid=n_valid),
        out_shape=(jax.ShapeDtypeStruct((b_rows, D_MODEL), BF16),
                   jax.ShapeDtypeStruct(fin_shape, F32)),
        grid=(n_batch, n_chunks),
        in_specs=in_specs,
        out_specs=(pl.BlockSpec((CHUNK, D_MODEL), zmap(0)),
                   pl.BlockSpec(fin_blk, lambda b, c: fin_index(b) + (0, 0, 0))),
        scratch_shapes=[pltpu.VMEM(_STATE, F32)],
        input_output_aliases=aliases,
        compiler_params=_cparams("arbitrary", "arbitrary"),
        name="retention",
    )(*args)


TN_MERGE = 1024


def _merge_kernel(a_ref, b_ref, gpp_ref, grp_ref, gps_ref, grs_ref,
                  am_ref, bm_ref, gpm_ref, grm_ref, wp_ref, wr_ref, o_ref, om_ref, *, n_p, n_s):
    i = pl.program_id(1)

    def tile(a, b, gp, gr, out):
        pool_branch = _dot(a[...], wp_ref[...])
        ret_branch = _dot(b[...], wr_ref[...])
        out[...] = (jax.nn.sigmoid(gp[...]) * pool_branch + jax.nn.sigmoid(gr[...]) * ret_branch).astype(BF16)

    @pl.when(i < n_p)
    def _():
        tile(a_ref, b_ref, gpp_ref, grp_ref, o_ref)

    @pl.when((i >= n_p) & (i < n_p + n_s))
    def _():
        tile(a_ref, b_ref, gps_ref, grs_ref, o_ref)

    @pl.when(i == n_p + n_s)
    def _():
        tile(am_ref, bm_ref, gpm_ref, grm_ref, om_ref)


def _merge(a_in, b_in, z, zs, a_meta, b_meta, zm, proj_pool, proj_ret):
    n_p = z.shape[0] // TM
    n_s = zs.shape[0] // TM
    n_main = n_p + n_s
    assert a_in.shape[0] == n_main * TM and b_in.shape[0] == n_main * TM
    nt = D_MODEL // TN_MERGE

    def p_map(col):
        return lambda j, i: (jnp.minimum(i, n_p - 1), col(j))

    def s_map(col):
        return lambda j, i: (jnp.clip(i - n_p, 0, n_s - 1), col(j))

    row = pl.BlockSpec((TM, D_MODEL), lambda j, i: (jnp.minimum(i, n_main - 1), 0))
    gate = lambda ref_map, seg: pl.BlockSpec((TM, TN_MERGE), ref_map(lambda j: seg * nt + j))
    const = lambda j, i: (0, 0)
    return pl.pallas_call(
        functools.partial(_merge_kernel, n_p=n_p, n_s=n_s),
        out_shape=(jax.ShapeDtypeStruct((n_main * TM, D_MODEL), BF16),
                   jax.ShapeDtypeStruct((CHUNK, D_MODEL), BF16)),
        grid=(nt, n_main + 1),
        in_specs=[row, row,
                  gate(p_map, COL_GP), gate(p_map, COL_GR), gate(s_map, SEG_GP), gate(s_map, SEG_GR),
                  pl.BlockSpec((CHUNK, D_MODEL), const),
                  pl.BlockSpec((CHUNK, D_MODEL), const),
                  pl.BlockSpec((CHUNK, TN_MERGE), lambda j, i: (0, SEG_GP * nt + j)),
                  pl.BlockSpec((CHUNK, TN_MERGE), lambda j, i: (0, SEG_GR * nt + j)),
                  pl.BlockSpec((D_MODEL, TN_MERGE), lambda j, i: (0, j)),
                  pl.BlockSpec((D_MODEL, TN_MERGE), lambda j, i: (0, j))],
        out_specs=(pl.BlockSpec((TM, TN_MERGE), lambda j, i: (jnp.minimum(i, n_main - 1), j)),
                   pl.BlockSpec((CHUNK, TN_MERGE), lambda j, i: (0, j))),
        compiler_params=_cparams("arbitrary", "arbitrary"),
        name="merge_branches",
    )(a_in, b_in, z, z, zs, zs, a_meta, b_meta, zm, zm, proj_pool, proj_ret)


def _out_kernel(m_ref, mm_ref, w_ref, g_ref, *rest, first, last, n_p, n_s):
    rest = list(rest)
    x_refs = [rest.pop(0) for _ in range(3 if first else 2)]
    i = pl.program_id(0)

    def emit(x, merged, outs):
        xn = x + _dot(merged[...], w_ref[...])
        y = _rms(xn, g_ref[...])
        if last:
            outs[0][...] = y
        else:
            outs[0][...] = xn
            outs[1][...] = y.astype(BF16)

    main_outs = [rest[0:1], rest[1:2]] if last else [rest[0:2], rest[0:2]]

    @pl.when(i < n_p)
    def _():
        emit(x_refs[0][...], m_ref, main_outs[0])

    @pl.when((i >= n_p) & (i < n_p + n_s))
    def _():
        emit(x_refs[1 if first else 0][...], m_ref, main_outs[1])

    if not last:
        @pl.when(i == n_p + n_s)
        def _():
            emit(_meta_rows(x_refs[2]) if first else x_refs[1][...], mm_ref, rest[2:4])


def _out_projection(merged, merged_meta, w_out, g_next, x_parts, first, last, n_p, n_s):
    n = n_p + n_s
    main_spec = pl.BlockSpec((TM, D_MODEL), lambda i: (jnp.minimum(i, n - 1), 0))
    meta_spec = pl.BlockSpec((CHUNK, D_MODEL), lambda i: (0, 0))
    in_specs = [main_spec, meta_spec,
                pl.BlockSpec((D_MODEL, D_MODEL), lambda i: (0, 0)),
                pl.BlockSpec((1, D_MODEL), lambda i: (0, 0))]
    if first:
        in_specs += _x_specs(n_p, n_s) + [pl.BlockSpec((N_META, D_MODEL), lambda i: (0, 0))]
    else:
        in_specs += [main_spec, meta_spec]
    if last:
        out_shape = (jax.ShapeDtypeStruct((n_p * TM, D_MODEL), F32),
                     jax.ShapeDtypeStruct((n_s * TM, D_MODEL), F32))
        out_specs = tuple(_x_specs(n_p, n_s))
    else:
        out_shape = (jax.ShapeDtypeStruct((n * TM, D_MODEL), F32), jax.ShapeDtypeStruct((n * TM, D_MODEL), BF16),
                     jax.ShapeDtypeStruct((CHUNK, D_MODEL), F32), jax.ShapeDtypeStruct((CHUNK, D_MODEL), BF16))
        out_specs = (main_spec, main_spec, meta_spec, meta_spec)
    return pl.pallas_call(
        functools.partial(_out_kernel, first=first, last=last, n_p=n_p, n_s=n_s),
        out_shape=out_shape,
        grid=(n if last else n + 1,),
        in_specs=in_specs,
        out_specs=out_specs,
        compiler_params=_cparams("arbitrary"),
        name="out_projection",
    )(merged, merged_meta, w_out, g_next, *x_parts)


def _log_decay():
    return jnp.log1p(-jnp.exp2(-5.0 - jnp.arange(HEADS, dtype=F32)))


def _rope_tables(pos):
    inv_freq = ROPE_BASE ** (-jnp.arange(HALF, dtype=F32) / HALF)
    ang = pos[:, None] * inv_freq[None, :]
    return jnp.cos(ang), jnp.sin(ang)


def _chunk_decay(lg, n):
    idx = jnp.arange(n, dtype=F32)
    diff = idx[:, None] - idx[None, :]
    causal = diff >= 0
    return jnp.where(causal[None], jnp.exp(jnp.where(causal, diff, 0.0)[None] * lg[:, None, None]), 0.0)


def _sample_decay_tables(lg, n_new):
    r = jnp.arange(RB * HEADS * n_new)
    head = (r // n_new) % HEADS
    tok = (r % n_new).astype(F32)
    lg_r = lg[head]
    same = (r[:, None] // n_new) == (r[None, :] // n_new)
    diff = tok[:, None] - tok[None, :]
    keep = same & (diff >= 0)
    dmask = jnp.where(keep, jnp.exp(jnp.where(keep, diff, 0.0) * lg_r[:, None]), 0.0)
    qdec = jnp.broadcast_to(jnp.exp((tok + 1.0) * lg_r)[:, None], (r.shape[0], HEAD_DIM))
    kdec = jnp.broadcast_to(jnp.exp((n_new - 1.0 - tok) * lg_r)[:, None], (r.shape[0], HEAD_DIM))
    return dmask, qdec, kdec


def kernel(x_prompt, x_sample, state_pool, state_ret, meta_tokens, norm_gain, w_in, pool_w, pool_scale,
           ret_gn_gain, proj_pool, proj_ret, w_out, final_norm):
    n_batch, seq = x_prompt.shape[:2]
    n_seq, n_new = x_sample.shape[:2]
    depth = norm_gain.shape[0]
    rows_prompt = n_batch * seq
    rows_sample = n_seq * n_new
    assert seq % CHUNK == 0 and n_seq % RB == 0
    assert n_new == 8, "sample tokens of one sequence must fill one f32 sublane tile"
    assert rows_prompt % TM == 0 and rows_sample % TM == 0
    n_p = rows_prompt // TM
    n_s = rows_sample // TM

    xp = x_prompt.reshape(rows_prompt, D_MODEL).astype(F32)
    xs = x_sample.reshape(rows_sample, D_MODEL).astype(F32)
    meta = meta_tokens.astype(F32)
    state_pool = state_pool.astype(F32)
    state_ret = state_ret.astype(F32)
    w_in = w_in.astype(F32)

    trig_p = _rope_tables(N_META + jnp.arange(seq, dtype=F32))
    trig_s = tuple(jnp.tile(t, (n_seq, 1)) for t in _rope_tables(PAST_LEN + jnp.arange(n_new, dtype=F32)))
    trig_m = _rope_tables(jnp.where(jnp.arange(CHUNK) < N_META, jnp.arange(CHUNK, dtype=F32), 0.0))
    lg = _log_decay()
    decay = _chunk_decay(lg, CHUNK)
    decay_step = _chunk_decay(lg, STEP_CHUNK)
    gam_new = jnp.exp(n_new * lg)
    tables = _sample_decay_tables(lg, n_new)
    zero_state = jnp.zeros(_STATE, F32)
    row = lambda a: a.reshape(1, D_MODEL).astype(F32)

    h, h_meta = _first_norm(xp, xs, meta, row(norm_gain[0]), n_p, n_s)
    x_parts = (xp, xs, meta)
    pool_p = []
    ret_p = pool_s = ret_s = None
    pw = pool_w.astype(BF16)
    out_weights = (proj_pool.astype(F32), proj_ret.astype(F32), w_out.astype(F32))
    state_pool_t = jnp.transpose(state_pool, (0, 2, 1, 3))
    for l in range(depth):
        ps = row(pool_scale[l])
        gn = row(ret_gn_gain[l])

        zs, zm, wb = _in_projection_small(h, h_meta, w_in, l, trig_s, trig_m, rows_prompt, rows_sample)
        b_meta, s_meta = _retention(((zm, SEG_Q), (zm, SEG_K), (zm, SEG_V), (zm, SEG_RG)), lg, gn, decay,
                                    zero_state, None, None, _STATE, lambda b: (), 1, 1, N_META)
        sample = (gam_new, gn, tables, state_ret, l, n_new, zs, rows_prompt)
        n_groups = n_seq // RB
        a_in, hist_p, b_in, ret_s = _proj_pool(h, wb, zm, pw, ps, l, n_batch, seq, sample, ret_s)
        done = (D_MODEL // TN_IN) * (rows_prompt // TM_IN)
        value_groups = min(n_groups - done, len(VALUE_SEGS) * PER_SEG * (rows_prompt // TM_IN))
        zv, b_in, ret_s = _in_projection(h, wb, rows_prompt, VALUE_SEGS,
                                         sample=(sample, done, value_groups, b_in, ret_s))
        done += value_groups
        zr, wp_l, wr_l, wo_l, b_in, ret_s = _in_projection(
            h, wb, rows_prompt, ROPE_SEGS, trig=trig_p, seq=seq, convert=(out_weights, l),
            sample=(sample, done, n_groups - done, b_in, ret_s))
        zg, b_in, ret_p = _in_projection(
            h, wb, rows_prompt, GATE_SEGS,
            chunks=(zr, zv, lg, gn, decay_step, s_meta, b_in, ret_p, (depth, n_batch) + _STATE, l, n_batch,
                    seq // CHUNK))

        a_meta = _pool_meta(zm, pw, ps, l)
        a_in, pool_s = _pool_sample(zs, state_pool_t, a_in, pool_s, pw, ps, l, n_seq, n_new, rows_prompt)
        pool_p.append(hist_p)

        merged, merged_meta = _merge(a_in, b_in, zg, zs, a_meta, b_meta, zm, wp_l, wr_l)
        last = l == depth - 1
        g_next = row(final_norm if last else norm_gain[l + 1])
        outs = _out_projection(merged, merged_meta, wo_l, g_next, x_parts, l == 0, last, n_p, n_s)
        if last:
            y_prompt, y_sample = outs
        else:
            x_new, h, x_meta, h_meta = outs
            x_parts = (x_new, x_meta)

    return (y_prompt.reshape(n_batch, seq, D_MODEL), y_sample.reshape(n_seq, n_new, D_MODEL),
            jnp.stack(pool_p), ret_p, jnp.transpose(pool_s, (0, 2, 1, 3)), ret_s)
```

```python
import functools

import jax
import jax.numpy as jnp
from jax import lax
from jax.experimental import pallas as pl
from jax.experimental.pallas import tpu as pltpu

F32 = jnp.float32
BF16 = jnp.bfloat16

D_MODEL = 2048
N_META = 16
POOL_WINDOWS = (2, 4, 8, 16)
N_GROUPS = len(POOL_WINDOWS)
POOL_GROUP = D_MODEL // N_GROUPS
POOL_HIST = max(POOL_WINDOWS) - 1
HEADS = 8
HEAD_DIM = D_MODEL // HEADS
HALF = HEAD_DIM // 2
CHUNK = 128
ROPE_BASE = 10000.0
EPS = 1e-6
PAST_LEN = 16384
SEG_U, SEG_PG, SEG_Q, SEG_K, SEG_V, SEG_RG, SEG_GP, SEG_GR = range(8)
N_SEG = 8
ROPE_SEGS = (SEG_Q, SEG_K)
COL_Q, COL_K = range(len(ROPE_SEGS))
VALUE_SEGS = (SEG_V, SEG_RG)
COL_V, COL_RG = range(len(VALUE_SEGS))
GATE_SEGS = (SEG_GP, SEG_GR)
COL_GP, COL_GR = range(len(GATE_SEGS))

TM = 512
TM_IN = 1024
TN_IN = 1024
PER_SEG = D_MODEL // TN_IN
VMEM_LIMIT = 56 * 1024 * 1024


VMEM_LIMIT_MAX = 60 * 1024 * 1024


def _cparams(*sem, vmem=VMEM_LIMIT):
    return pltpu.CompilerParams(dimension_semantics=sem, vmem_limit_bytes=vmem)


def _dot(a, b):
    return jnp.dot(a, b, preferred_element_type=F32)


def _dot_nt(a, b):
    return lax.dot_general(a, b, (((1,), (1,)), ((), ())), preferred_element_type=F32)


def _dot_tn(a, b):
    return lax.dot_general(a, b, (((0,), (0,)), ((), ())), preferred_element_type=F32)


def _rms(x, g):
    return x * lax.rsqrt(jnp.mean(x * x, axis=-1, keepdims=True) + EPS) * g


def _silu(x):
    return x * jax.nn.sigmoid(x)


def _x_specs(n_p, n_s):
    return [pl.BlockSpec((TM, D_MODEL), lambda i: (jnp.minimum(i, n_p - 1), 0)),
            pl.BlockSpec((TM, D_MODEL), lambda i: (jnp.clip(i - n_p, 0, n_s - 1), 0))]


def _meta_rows(meta_ref):
    return jnp.concatenate([meta_ref[...], jnp.zeros((CHUNK - N_META, D_MODEL), F32)], axis=0)


def _norm_kernel(xp_ref, xs_ref, meta_ref, g_ref, h_ref, hm_ref, *, n_p, n_s):
    i = pl.program_id(0)

    @pl.when(i < n_p)
    def _():
        h_ref[...] = _rms(xp_ref[...], g_ref[...]).astype(BF16)

    @pl.when((i >= n_p) & (i < n_p + n_s))
    def _():
        h_ref[...] = _rms(xs_ref[...], g_ref[...]).astype(BF16)

    @pl.when(i == n_p + n_s)
    def _():
        hm_ref[...] = _rms(_meta_rows(meta_ref), g_ref[...]).astype(BF16)


def _first_norm(xp, xs, meta, g, n_p, n_s):
    n = n_p + n_s
    return pl.pallas_call(
        functools.partial(_norm_kernel, n_p=n_p, n_s=n_s),
        out_shape=(jax.ShapeDtypeStruct((n * TM, D_MODEL), BF16),
                   jax.ShapeDtypeStruct((CHUNK, D_MODEL), BF16)),
        grid=(n + 1,),
        in_specs=_x_specs(n_p, n_s) + [pl.BlockSpec((N_META, D_MODEL), lambda i: (0, 0)),
                                       pl.BlockSpec((1, D_MODEL), lambda i: (0, 0))],
        out_specs=(pl.BlockSpec((TM, D_MODEL), lambda i: (jnp.minimum(i, n - 1), 0)),
                   pl.BlockSpec((CHUNK, D_MODEL), lambda i: (0, 0))),
        compiler_params=_cparams("arbitrary"),
        name="first_norm",
    )(xp, xs, meta, g)


RB = 2
_STATE = (HEADS, HEAD_DIM, HEAD_DIM)


def _head_norm_gate(o, gain, gate):
    mu = jnp.mean(o, axis=-1, keepdims=True)
    oc = o - mu
    on = oc * lax.rsqrt(jnp.mean(oc * oc, axis=-1, keepdims=True) + EPS)
    return ((on * gain) * _silu(gate)).astype(BF16)


def _sample_body(gam_ref, q_ref, k_ref, v_ref, rg_ref, gn_ref, dmask_ref, qdec_ref, kdec_ref, s_ref,
                 o_ref, sout_ref, n_new):
    pairs = [(b, h) for b in range(RB) for h in range(HEADS)]

    def stack(ref):
        return jnp.concatenate(
            [ref[b * n_new:(b + 1) * n_new, h * HEAD_DIM:(h + 1) * HEAD_DIM] for b, h in pairs], axis=0)

    k = stack(k_ref)
    qb = stack(q_ref).astype(BF16)
    v = stack(v_ref)
    vb = v.astype(BF16)
    scores = _dot_nt(qb, k.astype(BF16)) * dmask_ref[...]
    intra = _dot(scores.astype(BF16), vb)
    k_dec = (k * kdec_ref[...]).astype(BF16)
    rows = lax.broadcasted_iota(jnp.int32, v.shape, 0)
    is_even = (rows & n_new) == 0
    v_even = jnp.where(is_even, v, 0.0).astype(BF16)
    v_odd = jnp.where(is_even, 0.0, v).astype(BF16)
    grp = 2 * n_new
    inter_parts = []
    for idx, (b, h) in enumerate(pairs):
        g0 = (idx // 2) * grp
        off = (idx % 2) * n_new
        s_old = s_ref[b, h]
        inter = _dot(qb[g0:g0 + grp], s_old.astype(BF16))
        inter_parts.append(inter[off:off + n_new])
        v_sel = v_even if idx % 2 == 0 else v_odd
        sout_ref[b, h] = s_old * gam_ref[h] + _dot_tn(k_dec[g0:g0 + grp], v_sel[g0:g0 + grp])
    o = intra + jnp.concatenate(inter_parts, axis=0) * qdec_ref[...]
    gain = jnp.concatenate(
        [jnp.broadcast_to(gn_ref[:, h * HEAD_DIM:(h + 1) * HEAD_DIM], (n_new, HEAD_DIM)) for _, h in pairs],
        axis=0)
    out = _head_norm_gate(o, gain, stack(rg_ref)).astype(F32)
    seqs = []
    for b in range(RB):
        seqs.append(jnp.concatenate(
            [out[(b * HEADS + h) * n_new:(b * HEADS + h + 1) * n_new] for h in range(HEADS)], axis=1))
    o_ref[...] = jnp.concatenate(seqs, axis=0).astype(BF16)


def _rope_store(z, o_ref, cos_ref, sin_ref, scale):
    cos = cos_ref[...]
    sin = sin_ref[...]
    for hh in range(TN_IN // HEAD_DIM):
        lo = slice(hh * HEAD_DIM, hh * HEAD_DIM + HALF)
        hi = slice(hh * HEAD_DIM + HALF, (hh + 1) * HEAD_DIM)
        t1 = z[:, lo]
        t2 = z[:, hi]
        o_ref[:, lo] = (t1 * cos - t2 * sin) * scale
        o_ref[:, hi] = (t2 * cos + t1 * sin) * scale


def _k_scale(is_k):
    return jnp.where(is_k, HEAD_DIM ** -0.5, 1.0).astype(F32)


def _inproj_small_kernel(hs_ref, hm_ref, w_ref, cos_ref, sin_ref, cosm_ref, sinm_ref, zs_ref, zm_ref, wb_ref):
    s = pl.program_id(0)
    wb_ref[...] = w_ref[...].astype(BF16)
    zs_ref[...] = _dot(hs_ref[...], wb_ref[...])
    zm_ref[...] = _dot(hm_ref[...], wb_ref[...])

    @pl.when((s >= SEG_Q * PER_SEG) & (s < (SEG_K + 1) * PER_SEG))
    def _():
        scale = _k_scale(s >= SEG_K * PER_SEG)
        _rope_store(zs_ref[...], zs_ref, cos_ref, sin_ref, scale)
        _rope_store(zm_ref[...], zm_ref, cosm_ref, sinm_ref, scale)


def _in_projection_small(h, h_meta, w_in, layer, trig_s, trig_m, row0, rows):
    assert row0 % rows == 0
    const = lambda s: (0, 0)
    col = lambda s: (0, s)
    return pl.pallas_call(
        _inproj_small_kernel,
        out_shape=(jax.ShapeDtypeStruct((rows, N_SEG * D_MODEL), F32),
                   jax.ShapeDtypeStruct((CHUNK, N_SEG * D_MODEL), F32),
                   jax.ShapeDtypeStruct((N_SEG * PER_SEG, D_MODEL, TN_IN), BF16)),
        grid=(N_SEG * PER_SEG,),
        in_specs=[pl.BlockSpec((rows, D_MODEL), lambda s: (row0 // rows, 0)),
                  pl.BlockSpec((CHUNK, D_MODEL), const),
                  pl.BlockSpec((None, D_MODEL, TN_IN), lambda s: (layer, 0, s)),
                  pl.BlockSpec((rows, HALF), const),
                  pl.BlockSpec((rows, HALF), const),
                  pl.BlockSpec((CHUNK, HALF), const),
                  pl.BlockSpec((CHUNK, HALF), const)],
        out_specs=(pl.BlockSpec((rows, TN_IN), col),
                   pl.BlockSpec((CHUNK, TN_IN), col),
                   pl.BlockSpec((None, D_MODEL, TN_IN), lambda s: (s, 0, 0))),
        compiler_params=_cparams("arbitrary"),
        name="in_projection_small",
    )(h, h_meta, w_in, *trig_s, *trig_m)


N_SAMPLE_IN = 10


def _sample_operands(sample, group_of):
    gam, gn, (dmask, qdec, kdec), state_ret, layer, n_new, zs, prompt_rows = sample
    rows_g = RB * n_new
    stack_rows = RB * HEADS * n_new
    assert prompt_rows % rows_g == 0

    def zmap(seg):
        return lambda *idx: (group_of(*idx), seg)

    const = lambda *idx: (0, 0)
    sblk = (None, RB) + _STATE
    state_map = lambda *idx: (layer, group_of(*idx), 0, 0, 0)
    in_specs = [pl.BlockSpec(memory_space=pltpu.SMEM),
                pl.BlockSpec((rows_g, D_MODEL), zmap(SEG_Q)),
                pl.BlockSpec((rows_g, D_MODEL), zmap(SEG_K)),
                pl.BlockSpec((rows_g, D_MODEL), zmap(SEG_V)),
                pl.BlockSpec((rows_g, D_MODEL), zmap(SEG_RG)),
                pl.BlockSpec((1, D_MODEL), const),
                pl.BlockSpec((stack_rows, stack_rows), const),
                pl.BlockSpec((stack_rows, HEAD_DIM), const),
                pl.BlockSpec((stack_rows, HEAD_DIM), const),
                pl.BlockSpec(sblk, state_map)]
    args = [gam, zs, zs, zs, zs, gn, dmask, qdec, kdec, state_ret]
    out_shape = [jax.ShapeDtypeStruct((prompt_rows + zs.shape[0], D_MODEL), BF16),
                 jax.ShapeDtypeStruct(state_ret.shape, F32)]
    out_specs = [pl.BlockSpec((rows_g, D_MODEL), lambda *idx: (prompt_rows // rows_g + group_of(*idx), 0)),
                 pl.BlockSpec(sblk, state_map)]
    return in_specs, args, out_shape, out_specs


N_CHUNK_IN = 8
CHUNKS_PER_STEP = 2
STEP_CHUNK = CHUNKS_PER_STEP * CHUNK


def _inproj_kernel(h_ref, w_ref, *rest, rope, n_convert, n_new, n_groups, chunk_steps, resident):
    pos = 0
    if rope:
        cos_ref, sin_ref = rest[:2]
        pos = 2
    conv_in = rest[pos:pos + n_convert]
    pos += n_convert
    sample_refs = rest[pos:pos + N_SAMPLE_IN] if n_new is not None else None
    chunk_refs = rest[pos:pos + N_CHUNK_IN] if chunk_steps is not None else None
    n_scr = 0 if chunk_steps is None else 1
    n_out = 1 + n_convert + (0 if n_new is None else 2) + (0 if chunk_steps is None else 2)
    outs = rest[len(rest) - n_out - n_scr:len(rest) - n_scr]
    o_ref = outs[0]
    conv_out = outs[1:1 + n_convert]
    step = pl.program_id(0) * pl.num_programs(1) + pl.program_id(1)

    def tile():
        z = _dot(h_ref[...], w_ref[pl.program_id(1)] if resident else w_ref[...])
        if rope:
            _rope_store(z, o_ref, cos_ref, sin_ref, _k_scale(pl.program_id(0) >= PER_SEG))
        else:
            o_ref[...] = z
        for src, dst in zip(conv_in, conv_out):
            dst[...] = src[...].astype(BF16)

    if chunk_steps is not None:
        dec_ref, decay_ref, q_ref, k_ref, v_ref, rg_ref, gn_ref, s0_ref = chunk_refs
        bo_ref, fin_ref = outs[-2:]
        s_scr = rest[-1]

        @pl.when(step % chunk_steps == 0)
        def _():
            s_scr[...] = s0_ref[...]

        _chunk_body(dec_ref, decay_ref, q_ref, k_ref, v_ref, rg_ref, gn_ref, s_scr, bo_ref, STEP_CHUNK)
        tile()

        @pl.when(step % chunk_steps == chunk_steps - 1)
        def _():
            fin_ref[...] = s_scr[...]
    elif n_new is not None:
        so_ref, sout_ref = outs[-2:]

        @pl.when(step < n_groups)
        def _():
            tile()
            _sample_body(*sample_refs, so_ref, sout_ref, n_new)

        @pl.when(step >= n_groups)
        def _():
            tile()
    else:
        tile()


def _in_projection(h, wb, rows, segs, trig=None, seq=None, convert=None, sample=None, chunks=None,
                   resident_weights=False):
    assert rows % TM_IN == 0 and (sample is None or chunks is None)
    tiles = rows // TM_IN
    nblk = len(segs) * PER_SEG
    steps = nblk * tiles
    imap = (lambda f: lambda a, b: f(b, a)) if resident_weights else (lambda f: f)
    step_of = (lambda s, i: i * nblk + s) if resident_weights else (lambda s, i: s * tiles + i)
    rope = trig is not None
    gap_at = next((j for j in range(1, len(segs)) if segs[j] != segs[j - 1] + 1), len(segs))
    gap = segs[gap_at] - segs[gap_at - 1] - 1 if gap_at < len(segs) else 0
    assert all(segs[j] == segs[0] + j + (gap if j >= gap_at else 0) for j in range(len(segs)))

    def wmap(s, i):
        j = s // PER_SEG
        seg = segs[0] + j + jnp.where(j >= gap_at, gap, 0)
        return (seg * PER_SEG + s % PER_SEG, 0, 0)

    if resident_weights:
        assert gap == 0 and (segs[0] * PER_SEG) % nblk == 0
        w_spec = pl.BlockSpec((nblk, D_MODEL, TN_IN), lambda a, b: (segs[0] * PER_SEG // nblk, 0, 0),
                              pipeline_mode=pl.Buffered(1))
    else:
        w_spec = pl.BlockSpec((None, D_MODEL, TN_IN), imap(wmap))
    in_specs = [pl.BlockSpec((TM_IN, D_MODEL), imap(lambda s, i: (i, 0))), w_spec]
    args = [h, wb]
    out_shape = [jax.ShapeDtypeStruct((rows, len(segs) * D_MODEL), F32)]
    out_specs = [pl.BlockSpec((TM_IN, TN_IN), imap(lambda s, i: (i, s)))]
    alias_bufs = []
    scratch = []
    n_convert = 0
    n_new = n_groups = chunk_steps = None
    if rope:
        assert seq % TM_IN == 0
        in_specs += [pl.BlockSpec((TM_IN, HALF), imap(lambda s, i: (i % (seq // TM_IN), 0))) for _ in range(2)]
        args += list(trig)
    if convert is not None:
        weights, w_layer = convert
        n_convert = len(weights)
        busy = 0 if sample is None or sample[2] == steps else sample[2]
        assert D_MODEL % (steps - busy) == 0
        w_rows = D_MODEL // (steps - busy)
        w_block = lambda s, i: jnp.maximum(step_of(s, i) - busy, 0)
        in_specs += [pl.BlockSpec((None, w_rows, D_MODEL), imap(lambda s, i: (w_layer, w_block(s, i), 0)))
                     for _ in weights]
        args += list(weights)
        out_shape += [jax.ShapeDtypeStruct((D_MODEL, D_MODEL), BF16) for _ in weights]
        out_specs += [pl.BlockSpec((w_rows, D_MODEL), imap(lambda s, i: (w_block(s, i), 0))) for _ in weights]
    if sample is not None:
        sample, first_group, n_groups, b_buf, new_state = sample
        n_new = sample[5]
        assert 0 < n_groups <= steps, "one group of sample sequences per grid step"

        def group(s, i):
            return first_group + jnp.minimum(step_of(s, i), n_groups - 1)

        s_in, s_args, s_shape, s_specs = _sample_operands(sample, imap(group))
        in_specs += s_in
        args += s_args
        alias_bufs += [(len(out_shape), b_buf), (len(out_shape) + 1, new_state)]
        out_shape += s_shape
        out_specs += s_specs
    if chunks is not None:
        qk, vg, lg, gn, decay, s0, b_buf, fin, fin_shape, layer, n_batch, n_chunks = chunks
        assert steps * CHUNKS_PER_STEP == n_batch * n_chunks and n_chunks % CHUNKS_PER_STEP == 0
        chunk_steps = n_chunks // CHUNKS_PER_STEP
        crows = STEP_CHUNK
        assert decay.shape == (HEADS, crows, crows)

        def cmap(col):
            return imap(lambda s, i: (step_of(s, i), col))

        in_specs += [pl.BlockSpec(memory_space=pltpu.SMEM),
                     pl.BlockSpec((HEADS, crows, crows), lambda a, b: (0, 0, 0)),
                     pl.BlockSpec((crows, D_MODEL), cmap(COL_Q)),
                     pl.BlockSpec((crows, D_MODEL), cmap(COL_K)),
                     pl.BlockSpec((crows, D_MODEL), cmap(COL_V)),
                     pl.BlockSpec((crows, D_MODEL), cmap(COL_RG)),
                     pl.BlockSpec((1, D_MODEL), lambda s, i: (0, 0)),
                     pl.BlockSpec(_STATE, lambda s, i: (0, 0, 0))]
        args += [jnp.stack([lg, jnp.exp(crows * lg)]), decay, qk, qk, vg, vg, gn, s0]
        alias_bufs += [(len(out_shape), b_buf)]
        if fin is not None:
            alias_bufs += [(len(out_shape) + 1, fin)]
        out_shape += [jax.ShapeDtypeStruct(b_buf.shape, BF16), jax.ShapeDtypeStruct(fin_shape, F32)]
        out_specs += [pl.BlockSpec((crows, D_MODEL), cmap(0)),
                      pl.BlockSpec((None, None) + _STATE,
                                   imap(lambda s, i: (layer, step_of(s, i) // chunk_steps, 0, 0, 0)))]
        scratch = [pltpu.VMEM(_STATE, F32)]
    aliases = {}
    for out_idx, buf in alias_bufs:
        aliases[len(args)] = out_idx
        in_specs.append(pl.BlockSpec(memory_space=pl.ANY))
        args.append(buf)
    outs = pl.pallas_call(
        functools.partial(_inproj_kernel, rope=rope, n_convert=n_convert, n_new=n_new, n_groups=n_groups,
                          chunk_steps=chunk_steps, resident=resident_weights),
        out_shape=tuple(out_shape),
        grid=(tiles, nblk) if resident_weights else (nblk, tiles),
        in_specs=in_specs,
        out_specs=tuple(out_specs),
        scratch_shapes=scratch,
        input_output_aliases=aliases,
        compiler_params=_cparams("arbitrary", "arbitrary",
                                 vmem=VMEM_LIMIT_MAX if (convert and sample) or chunks else VMEM_LIMIT),
        name="in_projection_" + "_".join(str(g) for g in segs),
    )(*args)
    return outs if len(outs) > 1 else outs[0]


def _window_sum(ext, w, base, rows):
    s = ext
    size = 1
    while size < w:
        s = s[size:] + s[:-size]
        size *= 2
    start = base - (w - 1)
    return s[start:start + rows]


def _pool_group(g, wsum, u, inv_cnt, pw_ref, ps_ref, gate):
    sl = slice(g * POOL_GROUP, (g + 1) * POOL_GROUP)
    pooled = wsum * inv_cnt - u
    mixed = _dot(pooled.astype(BF16), pw_ref[g])
    return (mixed * ps_ref[:, sl] * _silu(gate)).astype(BF16)


def _pool_w_spec(layer):
    return pl.BlockSpec((None, N_GROUPS, POOL_GROUP, POOL_GROUP), lambda *_: (layer, 0, 0, 0))


def _pool_meta_kernel(u_ref, pg_ref, pw_ref, ps_ref, o_ref):
    avail = lax.broadcasted_iota(jnp.int32, (N_META, 1), 0).astype(F32) + 1.0
    o_ref[N_META:] = jnp.zeros((CHUNK - N_META, D_MODEL), BF16)
    for g, w in enumerate(POOL_WINDOWS):
        sl = slice(g * POOL_GROUP, (g + 1) * POOL_GROUP)
        u = u_ref[:, sl]
        wsum = _window_sum(jnp.concatenate([jnp.zeros_like(u), u], axis=0), w, N_META, N_META)
        inv_cnt = 1.0 / jnp.minimum(float(w), avail)
        o_ref[0:N_META, sl] = _pool_group(g, wsum, u, inv_cnt, pw_ref, ps_ref, pg_ref[:, sl])


def _pool_meta(zm, pool_w, pool_scale, layer):
    return pl.pallas_call(
        _pool_meta_kernel,
        out_shape=jax.ShapeDtypeStruct((CHUNK, D_MODEL), BF16),
        grid=(1,),
        in_specs=[pl.BlockSpec((N_META, D_MODEL), lambda i: (0, SEG_U)),
                  pl.BlockSpec((N_META, D_MODEL), lambda i: (0, SEG_PG)),
                  _pool_w_spec(layer),
                  pl.BlockSpec((1, D_MODEL), lambda i: (0, 0))],
        out_specs=pl.BlockSpec((CHUNK, D_MODEL), lambda i: (0, 0)),
        compiler_params=_cparams("arbitrary"),
        name="pool_meta",
    )(zm, zm, pool_w, pool_scale)


GROUPS_PER_BLOCK = TN_IN // POOL_GROUP


def _proj_pool_kernel(h_ref, wu_ref, wpg_ref, meta_ref, pw_ref, ps_ref, *rest, tiles_per_seq, n_new):
    sample_refs = rest[:N_SAMPLE_IN]
    o_ref, hist_ref, so_ref, sout_ref, carry = rest[-5:]
    c = pl.program_id(0)
    i = pl.program_id(1)
    first = (i % tiles_per_seq) == 0
    last = (i % tiles_per_seq) == tiles_per_seq - 1
    for cc in range(D_MODEL // TN_IN):
        @pl.when(c == cc)
        def _(cc=cc):
            u = _dot(h_ref[...], wu_ref[...])
            pg = _dot(h_ref[...], wpg_ref[...])
            prev = jnp.where(first, meta_ref[...], carry[...])
            for gg in range(GROUPS_PER_BLOCK):
                w = POOL_WINDOWS[cc * GROUPS_PER_BLOCK + gg]
                sl = slice(gg * POOL_GROUP, (gg + 1) * POOL_GROUP)
                wsum = _window_sum(jnp.concatenate([prev[:, sl], u[:, sl]], axis=0), w, N_META, TM_IN)
                pooled = wsum * (1.0 / w) - u[:, sl]
                mixed = _dot(pooled.astype(BF16), pw_ref[gg])
                o_ref[:, sl] = (mixed * ps_ref[:, sl] * _silu(pg[:, sl])).astype(BF16)
            carry[...] = u[TM_IN - N_META:]
            _sample_body(*sample_refs, so_ref, sout_ref, n_new)

            @pl.when(last)
            def _():
                hist_ref[0] = u[TM_IN - POOL_HIST:]


def _proj_pool(h, wb, zm, pool_w, pool_scale, layer, n_batch, seq, sample, new_state):
    assert seq % TM_IN == 0 and TN_IN % POOL_GROUP == 0
    tiles_per_seq = seq // TM_IN
    tiles = n_batch * tiles_per_seq
    step = lambda c, i: c * tiles + i
    in_specs = [pl.BlockSpec((TM_IN, D_MODEL), lambda c, i: (i, 0)),
                pl.BlockSpec((None, D_MODEL, TN_IN), lambda c, i: (SEG_U * PER_SEG + c, 0, 0)),
                pl.BlockSpec((None, D_MODEL, TN_IN), lambda c, i: (SEG_PG * PER_SEG + c, 0, 0)),
                pl.BlockSpec((N_META, TN_IN), lambda c, i: (0, SEG_U * PER_SEG + c)),
                pl.BlockSpec((None, GROUPS_PER_BLOCK, POOL_GROUP, POOL_GROUP), lambda c, i: (layer, c, 0, 0)),
                pl.BlockSpec((1, TN_IN), lambda c, i: (0, c))]
    args = [h, wb, wb, zm, pool_w, pool_scale]
    s_in, s_args, s_shape, s_specs = _sample_operands(sample, step)
    in_specs += s_in
    args += s_args
    total_rows = s_shape[0].shape[0]
    out_shape = [jax.ShapeDtypeStruct((total_rows, D_MODEL), BF16),
                 jax.ShapeDtypeStruct((n_batch, POOL_HIST, D_MODEL), F32)] + s_shape
    out_specs = [pl.BlockSpec((TM_IN, TN_IN), lambda c, i: (i, c)),
                 pl.BlockSpec((1, POOL_HIST, TN_IN), lambda c, i: (i // tiles_per_seq, 0, c))] + s_specs
    aliases = {}
    if new_state is not None:
        aliases[len(args)] = 3
        in_specs.append(pl.BlockSpec(memory_space=pl.ANY))
        args.append(new_state)
    return pl.pallas_call(
        functools.partial(_proj_pool_kernel, tiles_per_seq=tiles_per_seq, n_new=sample[5]),
        out_shape=tuple(out_shape),
        grid=(D_MODEL // TN_IN, tiles),
        in_specs=in_specs,
        out_specs=tuple(out_specs),
        scratch_shapes=[pltpu.VMEM((N_META, TN_IN), F32)],
        input_output_aliases=aliases,
        compiler_params=_cparams("arbitrary", "arbitrary"),
        name="projection_pool",
    )(*args)


LANE = 128
LANE_TILES = POOL_GROUP // LANE


def _pool_sample_kernel(hist_ref, *rest, n_seq, n_new):
    u_refs = rest[:LANE_TILES]
    pg_ref, pw_ref, ps_ref = rest[LANE_TILES:LANE_TILES + 3]
    o_ref, nh_ref, pooled_scr = rest[-3 - LANE_TILES:-LANE_TILES]
    mix_scrs = rest[-LANE_TILES:]
    g = pl.program_id(0)

    def token_rows(t):
        return pl.ds(t, n_seq, stride=n_new)

    u_t = [jnp.concatenate([r[token_rows(t), :] for r in u_refs], axis=1) for t in range(n_new)]
    ext = [hist_ref[j] for j in range(POOL_HIST)] + u_t
    for j in range(POOL_HIST):
        nh_ref[j] = ext[n_new + j]
    for k, w in enumerate(POOL_WINDOWS):
        @pl.when(g == k)
        def _(w=w):
            s = ext
            size = 1
            while size < w:
                s = [s[i + size] + s[i] for i in range(len(s) - size)]
                size *= 2
            for t in range(n_new):
                pooled_scr[t * n_seq:(t + 1) * n_seq] = s[POOL_HIST + 1 + t - w] * (1.0 / w) - u_t[t]
    mixed = _dot(pooled_scr[...].astype(BF16), pw_ref[...])
    for t in range(n_new):
        for c, scr in enumerate(mix_scrs):
            scr[token_rows(t), :] = mixed[t * n_seq:(t + 1) * n_seq, c * LANE:(c + 1) * LANE]
    mixed = jnp.concatenate([scr[...] for scr in mix_scrs], axis=1)
    o_ref[...] = (mixed * ps_ref[...] * _silu(pg_ref[...])).astype(BF16)


def _pool_sample(zs, state_pool_t, a_buf, new_hist, pool_w, pool_scale, layer, n_seq, n_new, row0):
    rows = n_seq * n_new
    assert row0 % rows == 0
    hist_blk = (None, POOL_HIST, n_seq, POOL_GROUP)
    hist_map = lambda g: (layer, 0, 0, g)
    per_seg = D_MODEL // POOL_GROUP

    def u_tile(c):
        return pl.BlockSpec((rows, LANE), lambda g: (0, (SEG_U * per_seg + g) * LANE_TILES + c))

    in_specs = [pl.BlockSpec(hist_blk, hist_map)] + [u_tile(c) for c in range(LANE_TILES)]
    in_specs += [pl.BlockSpec((rows, POOL_GROUP), lambda g: (0, SEG_PG * per_seg + g)),
                 pl.BlockSpec((None, None, POOL_GROUP, POOL_GROUP), lambda g: (layer, g, 0, 0)),
                 pl.BlockSpec((1, POOL_GROUP), lambda g: (0, g)),
                 pl.BlockSpec(memory_space=pl.ANY)]
    args = [state_pool_t] + [zs] * LANE_TILES + [zs, pool_w, pool_scale, a_buf]
    aliases = {len(args) - 1: 0}
    if new_hist is not None:
        aliases[len(args)] = 1
        in_specs.append(pl.BlockSpec(memory_space=pl.ANY))
        args.append(new_hist)
    return pl.pallas_call(
        functools.partial(_pool_sample_kernel, n_seq=n_seq, n_new=n_new),
        out_shape=(jax.ShapeDtypeStruct(a_buf.shape, BF16),
                   jax.ShapeDtypeStruct(state_pool_t.shape, F32)),
        grid=(N_GROUPS,),
        in_specs=in_specs,
        out_specs=(pl.BlockSpec((rows, POOL_GROUP), lambda g: (row0 // rows, g)),
                   pl.BlockSpec(hist_blk, hist_map)),
        scratch_shapes=[pltpu.VMEM((rows, POOL_GROUP), F32)] + [pltpu.VMEM((rows, LANE), F32)] * LANE_TILES,
        input_output_aliases=aliases,
        compiler_params=_cparams("arbitrary"),
        name="pool_sample",
    )(*args)


def _chunk_body(dec_ref, decay_ref, q_ref, k_ref, v_ref, rg_ref, gn_ref, s_scr, o_ref, n_valid,
                chunks=(slice(None),)):
    ridx = lax.broadcasted_iota(jnp.int32, (decay_ref.shape[1], 1), 0).astype(F32)
    for h in range(HEADS):
        hs = slice(h * HEAD_DIM, (h + 1) * HEAD_DIM)
        lg = dec_ref[0, h]
        q_dec = jnp.exp((ridx + 1.0) * lg)
        k_dec = jnp.exp((n_valid - 1.0 - ridx) * lg)
        state = s_scr[h]
        for rows in chunks:
            q = q_ref[rows, hs].astype(BF16)
            k = k_ref[rows, hs]
            v = v_ref[rows, hs].astype(BF16)
            scores = _dot_nt(q, k.astype(BF16)) * decay_ref[h]
            intra = _dot(scores.astype(BF16), v)
            inter = _dot(q, state.astype(BF16)) * q_dec
            state = state * dec_ref[1, h] + _dot_tn((k * k_dec).astype(BF16), v)
            o_ref[rows, hs] = _head_norm_gate(intra + inter, gn_ref[:, hs], rg_ref[rows, hs])
        s_scr[h] = state


def _ret_kernel(dec_ref, decay_ref, q_ref, k_ref, v_ref, rg_ref, gn_ref, s0_ref, *rest, n_valid):
    o_ref, sfin_ref, s_scr = rest[-3:]
    c = pl.program_id(1)

    @pl.when(c == 0)
    def _():
        s_scr[...] = s0_ref[...]

    _chunk_body(dec_ref, decay_ref, q_ref, k_ref, v_ref, rg_ref, gn_ref, s_scr, o_ref, n_valid)

    @pl.when(c == pl.num_programs(1) - 1)
    def _():
        sfin_ref[...] = s_scr[...]


def _retention(srcs, lg, gn, decay, s0, b_buf, fin, fin_shape, fin_index, n_batch, n_chunks, n_valid):
    dec = jnp.stack([lg, jnp.exp(n_valid * lg)])

    def zmap(col):
        return lambda b, c: (b * n_chunks + c, col)

    in_specs = [pl.BlockSpec(memory_space=pltpu.SMEM),
                pl.BlockSpec((HEADS, CHUNK, CHUNK), lambda b, c: (0, 0, 0))]
    in_specs += [pl.BlockSpec((CHUNK, D_MODEL), zmap(col)) for _, col in srcs]
    in_specs += [pl.BlockSpec((1, D_MODEL), lambda b, c: (0, 0)),
                 pl.BlockSpec(_STATE, lambda b, c: (0, 0, 0))]
    args = [dec, decay] + [a for a, _ in srcs] + [gn, s0]
    aliases = {}
    for out_idx, buf in enumerate((b_buf, fin)):
        if buf is not None:
            aliases[len(args)] = out_idx
            in_specs.append(pl.BlockSpec(memory_space=pl.ANY))
            args.append(buf)
    fin_blk = (None,) * (len(fin_shape) - 3) + _STATE
    b_rows = n_batch * n_chunks * CHUNK if b_buf is None else b_buf.shape[0]
    return pl.pallas_call(
        functools.partial(_ret_kernel, n_valid=n_valid),
        out_shape=(jax.ShapeDtypeStruct((b_rows, D_MODEL), BF16),
                   jax.ShapeDtypeStruct(fin_shape, F32)),
        grid=(n_batch, n_chunks),
        in_specs=in_specs,
        out_specs=(pl.BlockSpec((CHUNK, D_MODEL), zmap(0)),
                   pl.BlockSpec(fin_blk, lambda b, c: fin_index(b) + (0, 0, 0))),
        scratch_shapes=[pltpu.VMEM(_STATE, F32)],
        input_output_aliases=aliases,
        compiler_params=_cparams("arbitrary", "arbitrary"),
        name="retention",
    )(*args)


TN_MERGE = 1024


def _merge_kernel(a_ref, b_ref, gpp_ref, grp_ref, gps_ref, grs_ref,
                  am_ref, bm_ref, gpm_ref, grm_ref, wp_ref, wr_ref, o_ref, om_ref, *, n_p, n_s):
    i = pl.program_id(1)

    def tile(a, b, gp, gr, out):
        pool_branch = _dot(a[...], wp_ref[...])
        ret_branch = _dot(b[...], wr_ref[...])
        out[...] = (jax.nn.sigmoid(gp[...]) * pool_branch + jax.nn.sigmoid(gr[...]) * ret_branch).astype(BF16)

    @pl.when(i < n_p)
    def _():
        tile(a_ref, b_ref, gpp_ref, grp_ref, o_ref)

    @pl.when((i >= n_p) & (i < n_p + n_s))
    def _():
        tile(a_ref, b_ref, gps_ref, grs_ref, o_ref)

    @pl.when(i == n_p + n_s)
    def _():
        tile(am_ref, bm_ref, gpm_ref, grm_ref, om_ref)


def _merge(a_in, b_in, z, zs, a_meta, b_meta, zm, proj_pool, proj_ret):
    n_p = z.shape[0] // TM
    n_s = zs.shape[0] // TM
    n_main = n_p + n_s
    assert a_in.shape[0] == n_main * TM and b_in.shape[0] == n_main * TM
    nt = D_MODEL // TN_MERGE

    def p_map(col):
        return lambda j, i: (jnp.minimum(i, n_p - 1), col(j))

    def s_map(col):
        return lambda j, i: (jnp.clip(i - n_p, 0, n_s - 1), col(j))

    row = pl.BlockSpec((TM, D_MODEL), lambda j, i: (jnp.minimum(i, n_main - 1), 0))
    gate = lambda ref_map, seg: pl.BlockSpec((TM, TN_MERGE), ref_map(lambda j: seg * nt + j))
    const = lambda j, i: (0, 0)
    return pl.pallas_call(
        functools.partial(_merge_kernel, n_p=n_p, n_s=n_s),
        out_shape=(jax.ShapeDtypeStruct((n_main * TM, D_MODEL), BF16),
                   jax.ShapeDtypeStruct((CHUNK, D_MODEL), BF16)),
        grid=(nt, n_main + 1),
        in_specs=[row, row,
                  gate(p_map, COL_GP), gate(p_map, COL_GR), gate(s_map, SEG_GP), gate(s_map, SEG_GR),
                  pl.BlockSpec((CHUNK, D_MODEL), const),
                  pl.BlockSpec((CHUNK, D_MODEL), const),
                  pl.BlockSpec((CHUNK, TN_MERGE), lambda j, i: (0, SEG_GP * nt + j)),
                  pl.BlockSpec((CHUNK, TN_MERGE), lambda j, i: (0, SEG_GR * nt + j)),
                  pl.BlockSpec((D_MODEL, TN_MERGE), lambda j, i: (0, j)),
                  pl.BlockSpec((D_MODEL, TN_MERGE), lambda j, i: (0, j))],
        out_specs=(pl.BlockSpec((TM, TN_MERGE), lambda j, i: (jnp.minimum(i, n_main - 1), j)),
                   pl.BlockSpec((CHUNK, TN_MERGE), lambda j, i: (0, j))),
        compiler_params=_cparams("arbitrary", "arbitrary"),
        name="merge_branches",
    )(a_in, b_in, z, z, zs, zs, a_meta, b_meta, zm, zm, proj_pool, proj_ret)


def _out_kernel(m_ref, mm_ref, w_ref, g_ref, *rest, first, last, n_p, n_s):
    rest = list(rest)
    x_refs = [rest.pop(0) for _ in range(3 if first else 2)]
    i = pl.program_id(0)

    def emit(x, merged, outs):
        xn = x + _dot(merged[...], w_ref[...])
        y = _rms(xn, g_ref[...])
        if last:
            outs[0][...] = y
        else:
            outs[0][...] = xn
            outs[1][...] = y.astype(BF16)

    main_outs = [rest[0:1], rest[1:2]] if last else [rest[0:2], rest[0:2]]

    @pl.when(i < n_p)
    def _():
        emit(x_refs[0][...], m_ref, main_outs[0])

    @pl.when((i >= n_p) & (i < n_p + n_s))
    def _():
        emit(x_refs[1 if first else 0][...], m_ref, main_outs[1])

    if not last:
        @pl.when(i == n_p + n_s)
        def _():
            emit(_meta_rows(x_refs[2]) if first else x_refs[1][...], mm_ref, rest[2:4])


def _out_projection(merged, merged_meta, w_out, g_next, x_parts, first, last, n_p, n_s):
    n = n_p + n_s
    main_spec = pl.BlockSpec((TM, D_MODEL), lambda i: (jnp.minimum(i, n - 1), 0))
    meta_spec = pl.BlockSpec((CHUNK, D_MODEL), lambda i: (0, 0))
    in_specs = [main_spec, meta_spec,
                pl.BlockSpec((D_MODEL, D_MODEL), lambda i: (0, 0)),
                pl.BlockSpec((1, D_MODEL), lambda i: (0, 0))]
    if first:
        in_specs += _x_specs(n_p, n_s) + [pl.BlockSpec((N_META, D_MODEL), lambda i: (0, 0))]
    else:
        in_specs += [main_spec, meta_spec]
    if last:
        out_shape = (jax.ShapeDtypeStruct((n_p * TM, D_MODEL), F32),
                     jax.ShapeDtypeStruct((n_s * TM, D_MODEL), F32))
        out_specs = tuple(_x_specs(n_p, n_s))
    else:
        out_shape = (jax.ShapeDtypeStruct((n * TM, D_MODEL), F32), jax.ShapeDtypeStruct((n * TM, D_MODEL), BF16),
                     jax.ShapeDtypeStruct((CHUNK, D_MODEL), F32), jax.ShapeDtypeStruct((CHUNK, D_MODEL), BF16))
        out_specs = (main_spec, main_spec, meta_spec, meta_spec)
    return pl.pallas_call(
        functools.partial(_out_kernel, first=first, last=last, n_p=n_p, n_s=n_s),
        out_shape=out_shape,
        grid=(n if last else n + 1,),
        in_specs=in_specs,
        out_specs=out_specs,
        compiler_params=_cparams("arbitrary"),
        name="out_projection",
    )(merged, merged_meta, w_out, g_next, *x_parts)


def _log_decay():
    return jnp.log1p(-jnp.exp2(-5.0 - jnp.arange(HEADS, dtype=F32)))


def _rope_tables(pos):
    inv_freq = ROPE_BASE ** (-jnp.arange(HALF, dtype=F32) / HALF)
    ang = pos[:, None] * inv_freq[None, :]
    return jnp.cos(ang), jnp.sin(ang)


def _chunk_decay(lg, n):
    idx = jnp.arange(n, dtype=F32)
    diff = idx[:, None] - idx[None, :]
    causal = diff >= 0
    return jnp.where(causal[None], jnp.exp(jnp.where(causal, diff, 0.0)[None] * lg[:, None, None]), 0.0)


def _sample_decay_tables(lg, n_new):
    r = jnp.arange(RB * HEADS * n_new)
    head = (r // n_new) % HEADS
    tok = (r % n_new).astype(F32)
    lg_r = lg[head]
    same = (r[:, None] // n_new) == (r[None, :] // n_new)
    diff = tok[:, None] - tok[None, :]
    keep = same & (diff >= 0)
    dmask = jnp.where(keep, jnp.exp(jnp.where(keep, diff, 0.0) * lg_r[:, None]), 0.0)
    qdec = jnp.broadcast_to(jnp.exp((tok + 1.0) * lg_r)[:, None], (r.shape[0], HEAD_DIM))
    kdec = jnp.broadcast_to(jnp.exp((n_new - 1.0 - tok) * lg_r)[:, None], (r.shape[0], HEAD_DIM))
    return dmask, qdec, kdec


def kernel(x_prompt, x_sample, state_pool, state_ret, meta_tokens, norm_gain, w_in, pool_w, pool_scale,
           ret_gn_gain, proj_pool, proj_ret, w_out, final_norm):
    n_batch, seq = x_prompt.shape[:2]
    n_seq, n_new = x_sample.shape[:2]
    depth = norm_gain.shape[0]
    rows_prompt = n_batch * seq
    rows_sample = n_seq * n_new
    assert seq % CHUNK == 0 and n_seq % RB == 0
    assert n_new == 8, "sample tokens of one sequence must fill one f32 sublane tile"
    assert rows_prompt % TM == 0 and rows_sample % TM == 0
    n_p = rows_prompt // TM
    n_s = rows_sample // TM

    xp = x_prompt.reshape(rows_prompt, D_MODEL).astype(F32)
    xs = x_sample.reshape(rows_sample, D_MODEL).astype(F32)
    meta = meta_tokens.astype(F32)
    state_pool = state_pool.astype(F32)
    state_ret = state_ret.astype(F32)
    w_in = w_in.astype(F32)

    trig_p = _rope_tables(N_META + jnp.arange(seq, dtype=F32))
    trig_s = tuple(jnp.tile(t, (n_seq, 1)) for t in _rope_tables(PAST_LEN + jnp.arange(n_new, dtype=F32)))
    trig_m = _rope_tables(jnp.where(jnp.arange(CHUNK) < N_META, jnp.arange(CHUNK, dtype=F32), 0.0))
    lg = _log_decay()
    decay = _chunk_decay(lg, CHUNK)
    decay_step = _chunk_decay(lg, STEP_CHUNK)
    gam_new = jnp.exp(n_new * lg)
    tables = _sample_decay_tables(lg, n_new)
    zero_state = jnp.zeros(_STATE, F32)
    row = lambda a: a.reshape(1, D_MODEL).astype(F32)

    h, h_meta = _first_norm(xp, xs, meta, row(norm_gain[0]), n_p, n_s)
    x_parts = (xp, xs, meta)
    pool_p = []
    ret_p = pool_s = ret_s = None
    pw = pool_w.astype(BF16)
    out_weights = (proj_pool.astype(F32), proj_ret.astype(F32), w_out.astype(F32))
    state_pool_t = jnp.transpose(state_pool, (0, 2, 1, 3))
    for l in range(depth):
        ps = row(pool_scale[l])
        gn = row(ret_gn_gain[l])

        zs, zm, wb = _in_projection_small(h, h_meta, w_in, l, trig_s, trig_m, rows_prompt, rows_sample)
        b_meta, s_meta = _retention(((zm, SEG_Q), (zm, SEG_K), (zm, SEG_V), (zm, SEG_RG)), lg, gn, decay,
                                    zero_state, None, None, _STATE, lambda b: (), 1, 1, N_META)
        sample = (gam_new, gn, tables, state_ret, l, n_new, zs, rows_prompt)
        n_groups = n_seq // RB
        a_in, hist_p, b_in, ret_s = _proj_pool(h, wb, zm, pw, ps, l, n_batch, seq, sample, ret_s)
        done = (D_MODEL // TN_IN) * (rows_prompt // TM_IN)
        value_groups = min(n_groups - done, len(VALUE_SEGS) * PER_SEG * (rows_prompt // TM_IN))
        zv, b_in, ret_s = _in_projection(h, wb, rows_prompt, VALUE_SEGS, resident_weights=True,
                                         sample=(sample, done, value_groups, b_in, ret_s))
        done += value_groups
        zr, wp_l, wr_l, wo_l, b_in, ret_s = _in_projection(
            h, wb, rows_prompt, ROPE_SEGS, trig=trig_p, seq=seq, convert=(out_weights, l),
            sample=(sample, done, n_groups - done, b_in, ret_s))
        zg, b_in, ret_p = _in_projection(
            h, wb, rows_prompt, GATE_SEGS,
            chunks=(zr, zv, lg, gn, decay_step, s_meta, b_in, ret_p, (depth, n_batch) + _STATE, l, n_batch,
                    seq // CHUNK))

        a_meta = _pool_meta(zm, pw, ps, l)
        a_in, pool_s = _pool_sample(zs, state_pool_t, a_in, pool_s, pw, ps, l, n_seq, n_new, rows_prompt)
        pool_p.append(hist_p)

        merged, merged_meta = _merge(a_in, b_in, zg, zs, a_meta, b_meta, zm, wp_l, wr_l)
        last = l == depth - 1
        g_next = row(final_norm if last else norm_gain[l + 1])
        outs = _out_projection(merged, merged_meta, wo_l, g_next, x_parts, l == 0, last, n_p, n_s)
        if last:
            y_prompt, y_sample = outs
        else:
            x_new, h, x_meta, h_meta = outs
            x_parts = (x_new, x_meta)

    return (y_prompt.reshape(n_batch, seq, D_MODEL), y_sample.reshape(n_seq, n_new, D_MODEL),
            jnp.stack(pool_p), ret_p, jnp.transpose(pool_s, (0, 2, 1, 3)), ret_s)
```
